```python
import jax, jax.numpy as jnp
from jax import lax
import numpy as np

D_MODEL = 2048
BATCH = 4
SEQ = 4096
DEPTH = 2

N_EVEN = (DEPTH + 1) // 2
N_ODD = DEPTH // 2
EPS = 1e-6

HG_DK = 128
HG_DV = 128
HG_HEADS = (D_MODEL // 2) // HG_DV
HG_QK = HG_HEADS * HG_DK
HG_V = HG_HEADS * HG_DV
GLA_HEADS = 4
GLA_DV = (D_MODEL // 2) // GLA_HEADS
GLA_DK = GLA_DV // 2
GLA_QK = GLA_HEADS * GLA_DK
GLA_V = GLA_HEADS * GLA_DV
GLA_RANK = 16
GLA_GATE_NORM = 16.0
MIX_WIDTH = HG_V + GLA_V
EVEN_SIZES = (HG_QK, HG_QK, HG_V, HG_V, GLA_QK, GLA_QK, GLA_V, GLA_V, GLA_RANK)
EVEN_SPLITS = tuple(int(c) for c in np.cumsum(EVEN_SIZES)[:-1])
EVEN_IN = int(sum(EVEN_SIZES))
CHUNK = 64

MOBA_HEADS = 16
MOBA_DH = D_MODEL // MOBA_HEADS
MOBA_BLOCK = 256
MOBA_TOPK = 3
MOBA_QC = 16

N_GROUPS = 4
EXPERTS_PER_GROUP = 8
N_EXPERTS = N_GROUPS * EXPERTS_PER_GROUP
TOP_K = 2
D_EXPERT = D_MODEL // 4
MOE_BLOCK = 128

kernel_name = "hybrid_hgrn2_gla_moba_hmoe"


def rms_norm(x, g):
    xf = x.astype(jnp.float32)
    y = xf * lax.rsqrt(jnp.mean(xf * xf, axis=-1, keepdims=True) + EPS)
    return (y * g.astype(jnp.float32)).astype(x.dtype)


def split_heads(t, n_heads):
    b, s, _ = t.shape
    return t.reshape(b, s, n_heads, -1).transpose(0, 2, 1, 3)


def merge_heads(t):
    b, h, s, d = t.shape
    return t.transpose(0, 2, 1, 3).reshape(b, s, h * d)


def chunk_gated_linear_attention(q, k, v, log_a):
    b, h, s, dk = q.shape
    dv = v.shape[-1]
    n = s // CHUNK

    def to_chunks(t):
        t = t.astype(jnp.float32).reshape(b, h, n, CHUNK, t.shape[-1])
        return jnp.moveaxis(t, 2, 0)

    xs = tuple(to_chunks(t) for t in (q, k, v, log_a))
    causal = jnp.tril(jnp.ones((CHUNK, CHUNK), dtype=bool))[:, :, None]

    def step(state, inp):
        qc, kc, vc, ac = inp
        cum = jnp.cumsum(ac, axis=2)
        last = cum[:, :, -1:, :]
        rel = jnp.where(causal, cum[:, :, :, None, :] - cum[:, :, None, :, :], -jnp.inf)
        scores = jnp.einsum('bhid,bhjd,bhijd->bhij', qc, kc, jnp.exp(rel))
        out = (jnp.einsum('bhij,bhjv->bhiv', scores, vc)
               + jnp.einsum('bhid,bhdv->bhiv', qc * jnp.exp(cum), state))
        state = (jnp.exp(last[:, :, 0, :])[..., None] * state
                 + jnp.einsum('bhjd,bhjv->bhdv', kc * jnp.exp(last - cum), vc))
        return state, out

    state0 = jnp.zeros((b, h, dk, dv), jnp.float32)
    _, out = lax.scan(step, state0, xs)
    return jnp.moveaxis(out, 0, 2).reshape(b, h, s, dv).astype(v.dtype)


def hgrn2_gla_mixer(xn, lb, w_in, hg_norm, gla_gk2, gla_gk_bias, gla_norm, w_out):
    f32 = jnp.float32
    proj = xn @ w_in
    hq, hf, hi, hg, gq, gk, gv, gg, glow = jnp.split(proj, EVEN_SPLITS, axis=-1)
    forget = lb + (1.0 - lb) * jax.nn.sigmoid(hf.astype(f32))
    o_hg = chunk_gated_linear_attention(
        split_heads(jax.nn.silu(hq) * HG_DK ** -0.5, HG_HEADS),
        split_heads(1.0 - forget, HG_HEADS),
        split_heads(hi, HG_HEADS),
        split_heads(jnp.log(forget), HG_HEADS))
    o_hg = rms_norm(o_hg, hg_norm) * jax.nn.silu(split_heads(hg, HG_HEADS))
    log_alpha = jax.nn.log_sigmoid((glow @ gla_gk2 + gla_gk_bias).astype(f32)) / GLA_GATE_NORM
    o_gla = chunk_gated_linear_attention(
        split_heads(gq * GLA_DK ** -0.5, GLA_HEADS),
        split_heads(gk, GLA_HEADS),
        split_heads(gv, GLA_HEADS),
        split_heads(log_alpha, GLA_HEADS))
    o_gla = rms_norm(o_gla, gla_norm) * jax.nn.silu(split_heads(gg, GLA_HEADS))
    mixed = jnp.concatenate([merge_heads(o_hg), merge_heads(o_gla)], axis=-1)
    return mixed @ w_out


def moba_attention(xn, w_qkv, w_o):
    f32 = jnp.float32
    b, s, _ = xn.shape
    q, k, v = (split_heads(t, MOBA_HEADS) for t in jnp.split(xn @ w_qkv, 3, axis=-1))
    s_pad = -(-s // MOBA_BLOCK) * MOBA_BLOCK
    n_blocks = s_pad // MOBA_BLOCK
    k_eff = max(1, min(MOBA_TOPK, n_blocks))
    pad = ((0, 0), (0, 0), (0, s_pad - s), (0, 0))
    k_pad, v_pad = jnp.pad(k, pad), jnp.pad(v, pad)
    k_blk = k_pad.reshape(b, MOBA_HEADS, n_blocks, MOBA_BLOCK, MOBA_DH)
    v_blk = v_pad.reshape(b, MOBA_HEADS, n_blocks, MOBA_BLOCK, MOBA_DH)
    k_mean = jnp.mean(k_blk.astype(f32), axis=3)
    gate = jnp.einsum('bhtd,bhnd->bhtn', q.astype(f32), k_mean)
    q_block = jnp.arange(s) // MOBA_BLOCK
    past = jnp.arange(n_blocks)[None, :] < q_block[:, None]
    gate = jnp.where(past, gate, -jnp.inf)
    _, sel = lax.top_k(gate, k_eff)
    slot_valid = jnp.arange(k_eff)[None, :] < q_block[:, None]
    b_idx = jnp.arange(b)[:, None, None, None]
    h_idx = jnp.arange(MOBA_HEADS)[None, :, None, None]
    scale = MOBA_DH ** -0.5
    n_sel = k_eff * MOBA_BLOCK

    def query_chunk(i):
        start = i * MOBA_QC
        qi = lax.dynamic_slice_in_dim(q, start, MOBA_QC, axis=2)
        sel_i = lax.dynamic_slice_in_dim(sel, start, MOBA_QC, axis=2)
        valid_i = lax.dynamic_slice_in_dim(slot_valid, start, MOBA_QC, axis=0)
        own_start = (start // MOBA_BLOCK) * MOBA_BLOCK
        k_own = lax.dynamic_slice_in_dim(k_pad, own_start, MOBA_BLOCK, axis=2)
        v_own = lax.dynamic_slice_in_dim(v_pad, own_start, MOBA_BLOCK, axis=2)
        k_sel = k_blk[b_idx, h_idx, sel_i]
        v_sel = v_blk[b_idx, h_idx, sel_i]
        s_sel = jnp.einsum('bhqd,bhqnkd->bhqnk', qi, k_sel).astype(f32) * scale
        s_sel = jnp.where(valid_i[None, None, :, :, None], s_sel, -jnp.inf)
        s_own = jnp.einsum('bhqd,bhkd->bhqk', qi, k_own).astype(f32) * scale
        q_pos = start + jnp.arange(MOBA_QC)
        k_pos = own_start + jnp.arange(MOBA_BLOCK)
        s_own = jnp.where(k_pos[None, :] <= q_pos[:, None], s_own, -jnp.inf)
        logits = jnp.concatenate(
            [s_sel.reshape(b, MOBA_HEADS, MOBA_QC, n_sel), s_own], axis=-1)
        p = jax.nn.softmax(logits, axis=-1).astype(v.dtype)
        p_sel = p[..., :n_sel].reshape(b, MOBA_HEADS, MOBA_QC, k_eff, MOBA_BLOCK)
        return (jnp.einsum('bhqnk,bhqnkd->bhqd', p_sel, v_sel)
                + jnp.einsum('bhqk,bhkd->bhqd', p[..., n_sel:], v_own))

    out = lax.map(query_chunk, jnp.arange(s // MOBA_QC))
    out = jnp.moveaxis(out, 0, 2).reshape(b, MOBA_HEADS, s, MOBA_DH)
    return merge_heads(out) @ w_o


def hierarchical_moe(xn, w_group, b_group, w_router, b_router, w1, w3, w2):
    f32 = jnp.float32
    b, s, d = xn.shape
    n_tok = b * s
    xt = xn.reshape(n_tok, d)
    g_logits = (xt @ w_group).astype(f32) + b_group
    grp = jnp.argmax(g_logits, axis=-1)
    p_grp = jnp.take_along_axis(jax.nn.softmax(g_logits, axis=-1), grp[:, None], axis=-1)
    e_logits = ((xt @ w_router).astype(f32) + b_router).reshape(n_tok, N_GROUPS, EXPERTS_PER_GROUP)
    e_in = jnp.take_along_axis(e_logits, grp[:, None, None], axis=1)[:, 0]
    top_val, top_j = lax.top_k(e_in, TOP_K)
    weight = p_grp * jax.nn.softmax(top_val, axis=-1)
    eid = grp[:, None] * EXPERTS_PER_GROUP + top_j
    n_assign = n_tok * TOP_K
    eid_f = eid.reshape(n_assign)
    tok_f = jnp.repeat(jnp.arange(n_tok), TOP_K)
    w_f = weight.reshape(n_assign)
    order = jnp.argsort(eid_f)
    e_s, tok_s, w_s = eid_f[order], tok_f[order], w_f[order]
    counts = jnp.bincount(eid_f, length=N_EXPERTS)
    starts = jnp.cumsum(counts) - counts
    padded = (counts + MOE_BLOCK - 1) // MOE_BLOCK * MOE_BLOCK
    pends = jnp.cumsum(padded)
    pstarts = pends - padded
    dest = pstarts[e_s] + (jnp.arange(n_assign) - starts[e_s])
    n_blocks = -(-n_assign // MOE_BLOCK) + N_EXPERTS
    rows = n_blocks * MOE_BLOCK
    x_pad = jnp.zeros((rows, d), xt.dtype).at[dest].set(xt[tok_s])
    block_e = jnp.minimum(
        jnp.searchsorted(pends, jnp.arange(n_blocks) * MOE_BLOCK, side='right'), N_EXPERTS - 1)

    def expert_block(args):
        xb, e = args
        return (jax.nn.silu(xb @ w1[e]) * (xb @ w3[e])) @ w2[e]

    y_pad = lax.map(expert_block, (x_pad.reshape(n_blocks, MOE_BLOCK, d), block_e))
    y = y_pad.reshape(rows, d)[dest] * w_s[:, None].astype(xt.dtype)
    out = jax.ops.segment_sum(y, tok_s, num_segments=n_tok)
    return out.reshape(b, s, d)


def setup_inputs(seed: int = 0) -> dict:
    key = jax.random.key(seed)
    ks = jax.random.split(key, 20)
    f32 = jnp.float32

    def normal(k, shape, scale):
        return jax.random.normal(k, shape, f32) * scale

    def gain(k, shape):
        return 1.0 + 0.02 * jax.random.normal(k, shape, f32)

    return {
        "x": normal(ks[0], (BATCH, SEQ, D_MODEL), 1.0),
        "norm_mix": gain(ks[1], (DEPTH, D_MODEL)),
        "norm_ffn": gain(ks[2], (DEPTH, D_MODEL)),
        "norm_final": gain(ks[3], (D_MODEL,)),
        "w_in_even": normal(ks[4], (N_EVEN, D_MODEL, EVEN_IN), D_MODEL ** -0.5),
        "hg_lb_logits": normal(ks[5], (DEPTH + 1, HG_QK), 1.0),
        "hg_norm": gain(ks[6], (N_EVEN, HG_DV)),
        "gla_gk2": normal(ks[7], (N_EVEN, GLA_RANK, GLA_QK), GLA_RANK ** -0.5),
        "gla_gk_bias": normal(ks[8], (N_EVEN, GLA_QK), 0.1),
        "gla_norm": gain(ks[9], (N_EVEN, GLA_DV)),
        "w_out_even": normal(ks[10], (N_EVEN, MIX_WIDTH, D_MODEL), MIX_WIDTH ** -0.5),
        "w_qkv_odd": normal(ks[11], (N_ODD, D_MODEL, 3 * D_MODEL), D_MODEL ** -0.5),
        "w_o_odd": normal(ks[12], (N_ODD, D_MODEL, D_MODEL), D_MODEL ** -0.5),
        "router_group_w": normal(ks[13], (DEPTH, D_MODEL, N_GROUPS), D_MODEL ** -0.5),
        "router_group_b": normal(ks[14], (DEPTH, N_GROUPS), 0.01),
        "router_expert_w": normal(ks[15], (DEPTH, D_MODEL, N_EXPERTS), D_MODEL ** -0.5),
        "router_expert_b": normal(ks[16], (DEPTH, N_EXPERTS), 0.01),
        "expert_w1": normal(ks[17], (DEPTH, N_EXPERTS, D_MODEL, D_EXPERT), D_MODEL ** -0.5),
        "expert_w3": normal(ks[18], (DEPTH, N_EXPERTS, D_MODEL, D_EXPERT), D_MODEL ** -0.5),
        "expert_w2": normal(ks[19], (DEPTH, N_EXPERTS, D_EXPERT, D_MODEL), D_EXPERT ** -0.5),
    }


def reference(x, norm_mix, norm_ffn, norm_final, w_in_even, hg_lb_logits, hg_norm,
              gla_gk2, gla_gk_bias, gla_norm, w_out_even, w_qkv_odd, w_o_odd,
              router_group_w, router_group_b, router_expert_w, router_expert_b,
              expert_w1, expert_w3, expert_w2):
    lb_table = jnp.cumsum(jax.nn.softmax(hg_lb_logits.astype(jnp.float32), axis=0), axis=0)
    h = x
    for l in range(DEPTH):
        xn = rms_norm(h, norm_mix[l])
        if l % 2 == 0:
            e = l // 2
            h = h + hgrn2_gla_mixer(xn, lb_table[l], w_in_even[e], hg_norm[e], gla_gk2[e],
                                    gla_gk_bias[e], gla_norm[e], w_out_even[e])
        else:
            o = l // 2
            h = h + moba_attention(xn, w_qkv_odd[o], w_o_odd[o])
        h = h + hierarchical_moe(rms_norm(h, norm_ffn[l]), router_group_w[l], router_group_b[l],
                                 router_expert_w[l], router_expert_b[l],
                                 expert_w1[l], expert_w3[l], expert_w2[l])
    return rms_norm(h, norm_final)
```

```python
import functools

import jax
import jax.numpy as jnp
from jax import lax
from jax.experimental import pallas as pl
from jax.experimental.pallas import tpu as pltpu

F32 = jnp.float32
BF16 = jnp.bfloat16
I32 = jnp.int32
HIGHEST = lax.Precision.HIGHEST
NEG_INF = float("-inf")

EPS = 1e-6
D_MODEL = 2048

HG_HEADS, HG_DK, HG_DV = 8, 128, 128
GLA_HEADS, GLA_DK, GLA_DV = 4, 128, 256
GLA_RANK = 16
GLA_GATE_NORM = 16.0
HG_QK = HG_HEADS * HG_DK
MAIN_IN = 4 * HG_QK + 2 * GLA_HEADS * GLA_DK + 2 * GLA_HEADS * GLA_DV
CHUNK = 64
SUB = 16
REC_T = 256

MOBA_HEADS, MOBA_DH = 16, 128
MOBA_BLOCK = 256
MOBA_TOPK = 3
MASK_BIAS = -1e30

N_GROUPS, EXPERTS_PER_GROUP = 4, 8
N_EXPERTS = N_GROUPS * EXPERTS_PER_GROUP
D_EXPERT = D_MODEL // 4
EXPERT_LANE0 = N_GROUPS
ROUTE_TM = 256
MOE_BM = 256
DISPATCH_T = 512
COMBINE_T = 128

LANES = 128
VMEM_LIMIT = 56 * 1024 * 1024

NT_DIMS = (((1,), (1,)), ((), ()))
TN_DIMS = (((0,), (0,)), ((), ()))


def _params(*sem):
    return pltpu.CompilerParams(dimension_semantics=sem, vmem_limit_bytes=VMEM_LIMIT)


def _rms(x, g):
    return x * lax.rsqrt(jnp.mean(x * x, axis=-1, keepdims=True) + EPS) * g


def _silu(x):
    return x * jax.nn.sigmoid(x)


def _norm_matmul_body(*refs, has_side):
    if has_side:
        x_ref, g_ref, w_ref, ws_ref, o_ref, os_ref, xn_ref = refs
    else:
        x_ref, g_ref, w_ref, o_ref, xn_ref = refs

    @pl.when(pl.program_id(1) == 0)
    def _():
        xn_ref[...] = _rms(x_ref[...], g_ref[...]).astype(BF16)
        if has_side:
            os_ref[...] = jnp.dot(xn_ref[...], ws_ref[...], preferred_element_type=F32)

    o_ref[...] = jnp.dot(xn_ref[...], w_ref[...], preferred_element_type=F32).astype(o_ref.dtype)


def norm_matmul(x, g, w, w_side=None, *, tm, tn):
    m, k = x.shape
    n = w.shape[1]
    has_side = w_side is not None
    in_specs = [pl.BlockSpec((tm, k), lambda i, j: (i, 0)),
                pl.BlockSpec((1, k), lambda i, j: (0, 0)),
                pl.BlockSpec((k, tn), lambda i, j: (0, j))]
    out_specs = [pl.BlockSpec((tm, tn), lambda i, j: (i, j))]
    out_shape = [jax.ShapeDtypeStruct((m, n), BF16)]
    args = [x, g.reshape(1, k), w]
    if has_side:
        ns = w_side.shape[1]
        in_specs.append(pl.BlockSpec((k, ns), lambda i, j: (0, 0)))
        out_specs.append(pl.BlockSpec((tm, ns), lambda i, j: (i, 0)))
        out_shape.append(jax.ShapeDtypeStruct((m, ns), F32))
        args.append(w_side)
    outs = pl.pallas_call(
        functools.partial(_norm_matmul_body, has_side=has_side),
        grid=(m // tm, n // tn),
        in_specs=in_specs, out_specs=out_specs, out_shape=out_shape,
        scratch_shapes=[pltpu.VMEM((tm, k), BF16)],
        compiler_params=_params("parallel", "arbitrary"),
        name="norm_matmul_side" if has_side else "norm_matmul",
    )(*args)
    return outs if has_side else outs[0]


def _matmul_res_body(a_ref, w_ref, r_ref, o_ref):
    o_ref[...] = r_ref[...] + jnp.dot(a_ref[...], w_ref[...], preferred_element_type=F32)


def matmul_residual(a, w, res, *, tm):
    m, k = a.shape
    n = w.shape[1]
    return pl.pallas_call(
        _matmul_res_body,
        grid=(m // tm,),
        in_specs=[pl.BlockSpec((tm, k), lambda i: (i, 0)),
                  pl.BlockSpec((k, n), lambda i: (0, 0)),
                  pl.BlockSpec((tm, n), lambda i: (i, 0))],
        out_specs=pl.BlockSpec((tm, n), lambda i: (i, 0)),
        out_shape=jax.ShapeDtypeStruct((m, n), F32),
        compiler_params=_params("parallel"),
        name="matmul_residual",
    )(a, w, res)


def _recurrence_chunk(q, k, v, la, st_ref):
    r_i = lax.broadcasted_iota(I32, (CHUNK, CHUNK), 0)
    c_i = lax.broadcasted_iota(I32, (CHUNK, CHUNK), 1)
    tri = (c_i <= r_i).astype(F32)
    cum = jnp.dot(tri, la, precision=HIGHEST, preferred_element_type=F32)
    last = cum[CHUNK - 1:CHUNK, :]
    qg = (q * jnp.exp(cum)).astype(BF16)
    kg = (k * jnp.exp(last - cum)).astype(BF16)
    vb = v.astype(BF16)
    st = st_ref[...]
    o_inter = lax.dot_general(qg, st.astype(BF16), NT_DIMS, preferred_element_type=F32)
    st_ref[...] = st * jnp.exp(last) + lax.dot_general(vb, kg, TN_DIMS, preferred_element_type=F32)

    rows = lax.broadcasted_iota(I32, (SUB, 1), 0)
    outs = []
    for blk in range(CHUNK // SUB):
        lo = blk * SUB
        q_b, k_b, v_b, c_b = q[lo:lo + SUB], k[lo:lo + SUB], v[lo:lo + SUB], cum[lo:lo + SUB]
        acc = o_inter[lo:lo + SUB]
        if blk > 0:
            ref_pt = cum[lo - 1:lo, :]
            q_t = (q_b * jnp.exp(c_b - ref_pt)).astype(BF16)
            k_t = (k[:lo] * jnp.exp(ref_pt - cum[:lo])).astype(BF16)
            s = lax.dot_general(q_t, k_t, NT_DIMS, preferred_element_type=F32)
            acc = acc + jnp.dot(s.astype(BF16), vb[:lo], preferred_element_type=F32)
        for j in range(SUB):
            e = jnp.exp(jnp.where(rows >= j, c_b - c_b[j:j + 1, :], NEG_INF))
            col = jnp.sum(q_b * k_b[j:j + 1, :] * e, axis=-1, keepdims=True)
            acc = acc + col * v_b[j:j + 1, :]
        outs.append(acc)
    return jnp.concatenate(outs, axis=0)


def _log_sigmoid(z):
    return jnp.minimum(z, 0.0) - jnp.log(1.0 + jnp.exp(-jnp.abs(z)))


def _hgrn2_body(hq_ref, hf_ref, hi_ref, hg_ref, lb_ref, nw_ref, o_ref, st_ref):
    @pl.when(pl.program_id(2) == 0)
    def _():
        st_ref[...] = jnp.zeros_like(st_ref)

    lb = lb_ref[...]
    nw = nw_ref[...]

    def chunk(c, carry):
        rs = pl.ds(pl.multiple_of(c * CHUNK, CHUNK), CHUNK)
        forget = lb + (1.0 - lb) * jax.nn.sigmoid(hf_ref[0, rs, :].astype(F32))
        q = _silu(hq_ref[0, rs, :].astype(F32)) * (HG_DK ** -0.5)
        o = _recurrence_chunk(q, 1.0 - forget, hi_ref[0, rs, :].astype(F32), jnp.log(forget), st_ref)
        o_ref[0, rs, :] = (_rms(o, nw) * _silu(hg_ref[0, rs, :].astype(F32))).astype(o_ref.dtype)
        return carry

    lax.fori_loop(0, REC_T // CHUNK, chunk, 0)


def _gla_body(gq_ref, gk_ref, gv_ref, gg_ref, glow_ref, gk2_ref, gb_ref, nw_ref, o_ref, st_ref):
    @pl.when(pl.program_id(2) == 0)
    def _():
        st_ref[...] = jnp.zeros_like(st_ref)

    gk2 = gk2_ref[...]
    gb = gb_ref[...]
    nw = nw_ref[...]

    def chunk(c, carry):
        rs = pl.ds(pl.multiple_of(c * CHUNK, CHUNK), CHUNK)
        z = jnp.dot(glow_ref[0, rs, :], gk2, precision=HIGHEST, preferred_element_type=F32) + gb
        la = _log_sigmoid(z) / GLA_GATE_NORM
        q = gq_ref[0, rs, :].astype(F32) * (GLA_DK ** -0.5)
        o = _recurrence_chunk(q, gk_ref[0, rs, :].astype(F32), gv_ref[0, rs, :].astype(F32), la, st_ref)
        o_ref[0, rs, :] = (_rms(o, nw) * _silu(gg_ref[0, rs, :].astype(F32))).astype(o_ref.dtype)
        return carry

    lax.fori_loop(0, REC_T // CHUNK, chunk, 0)


def hgrn2_mix(proj, lb, hg_norm):
    b, s, _ = proj.shape

    def col(base):
        return pl.BlockSpec((1, REC_T, HG_DK), lambda bi, h, t: (bi, t, base + h))

    return pl.pallas_call(
        _hgrn2_body,
        grid=(b, HG_HEADS, s // REC_T),
        in_specs=[col(0), col(HG_HEADS), col(2 * HG_HEADS), col(3 * HG_HEADS),
                  pl.BlockSpec((1, HG_DK), lambda bi, h, t: (0, h)),
                  pl.BlockSpec((1, HG_DV), lambda bi, h, t: (0, 0))],
        out_specs=pl.BlockSpec((1, REC_T, HG_DV), lambda bi, h, t: (bi, t, h)),
        out_shape=jax.ShapeDtypeStruct((b, s, HG_HEADS * HG_DV), BF16),
        scratch_shapes=[pltpu.VMEM((HG_DV, HG_DK), F32)],
        compiler_params=_params("parallel", "parallel", "arbitrary"),
        name="hgrn2_mix",
    )(proj, proj, proj, proj, lb.reshape(1, HG_QK), hg_norm.reshape(1, HG_DV))


def gla_mix(proj, glow, gk2_pad, gk_bias, gla_norm):
    b, s, _ = proj.shape
    q0 = 4 * HG_QK // GLA_DK
    k0 = q0 + GLA_HEADS
    v0 = (4 * HG_QK + 2 * GLA_HEADS * GLA_DK) // GLA_DV
    g0 = v0 + GLA_HEADS
    return pl.pallas_call(
        _gla_body,
        grid=(b, GLA_HEADS, s // REC_T),
        in_specs=[pl.BlockSpec((1, REC_T, GLA_DK), lambda bi, h, t: (bi, t, q0 + h)),
                  pl.BlockSpec((1, REC_T, GLA_DK), lambda bi, h, t: (bi, t, k0 + h)),
                  pl.BlockSpec((1, REC_T, GLA_DV), lambda bi, h, t: (bi, t, v0 + h)),
                  pl.BlockSpec((1, REC_T, GLA_DV), lambda bi, h, t: (bi, t, g0 + h)),
                  pl.BlockSpec((1, REC_T, LANES), lambda bi, h, t: (bi, t, 0)),
                  pl.BlockSpec((LANES, GLA_DK), lambda bi, h, t: (0, h)),
                  pl.BlockSpec((1, GLA_DK), lambda bi, h, t: (0, h)),
                  pl.BlockSpec((1, GLA_DV), lambda bi, h, t: (0, 0))],
        out_specs=pl.BlockSpec((1, REC_T, GLA_DV), lambda bi, h, t: (bi, t, h)),
        out_shape=jax.ShapeDtypeStruct((b, s, GLA_HEADS * GLA_DV), BF16),
        scratch_shapes=[pltpu.VMEM((GLA_DV, GLA_DK), F32)],
        compiler_params=_params("parallel", "parallel", "arbitrary"),
        name="gla_mix",
    )(proj, proj, proj, proj, glow, gk2_pad, gk_bias.reshape(1, -1), gla_norm.reshape(1, GLA_DV))


def _moba_body(q_ref, k_ref, v_ref, o_ref, kmean_ref, *, n_blocks):
    qi = pl.program_id(2)
    blk = MOBA_BLOCK

    @pl.when(qi == 0)
    def _():
        kmean_ref[...] = jnp.zeros_like(kmean_ref)
        for n in range(n_blocks):
            kb = k_ref[0, n * blk:(n + 1) * blk, :].astype(F32)
            kmean_ref[n:n + 1, :] = jnp.mean(kb, axis=0, keepdims=True)

    q = q_ref[0]
    lane = lax.broadcasted_iota(I32, (blk, LANES), 1)
    gate = lax.dot_general(q.astype(F32), kmean_ref[...], NT_DIMS, precision=HIGHEST,
                           preferred_element_type=F32)
    past = lane < qi
    gate = jnp.where(past, gate, NEG_INF)
    rank = jnp.zeros((blk, LANES), I32)
    for m in range(n_blocks):
        gm = gate[:, m:m + 1]
        beats = jnp.where(gm > gate, 1, jnp.where(gm == gate, jnp.where(lane > m, 1, 0), 0))
        rank = rank + beats
    keep = jnp.where(past, jnp.where(rank < MOBA_TOPK, 1, 0), 0)
    bias = jnp.where(keep > 0, 0.0, MASK_BIAS).astype(BF16)
    q_aug = jnp.concatenate([q, bias], axis=1)

    scale = MOBA_DH ** -0.5
    own = pl.ds(pl.multiple_of(qi * blk, blk), blk)
    s = lax.dot_general(q, k_ref[0, own, :], NT_DIMS, preferred_element_type=F32) * scale
    r_i = lax.broadcasted_iota(I32, (blk, blk), 0)
    c_i = lax.broadcasted_iota(I32, (blk, blk), 1)
    s = jnp.where(c_i <= r_i, s, NEG_INF)
    m0 = jnp.max(s, axis=-1, keepdims=True)
    p = jnp.exp(s - m0)
    l0 = jnp.sum(p, axis=-1, keepdims=True)
    acc0 = jnp.dot(p.astype(BF16), v_ref[0, own, :], preferred_element_type=F32)

    def past_block(n, carry):
        m_run, l_run, acc = carry
        rs = pl.ds(pl.multiple_of(n * blk, blk), blk)
        onehot = jnp.where(lane == n, 1.0, 0.0).astype(BF16)
        k_aug = jnp.concatenate([k_ref[0, rs, :], onehot], axis=1)
        s = lax.dot_general(q_aug, k_aug, NT_DIMS, preferred_element_type=F32) * scale
        m_new = jnp.maximum(m_run, jnp.max(s, axis=-1, keepdims=True))
        alpha = jnp.exp(m_run - m_new)
        p = jnp.exp(s - m_new)
        l_new = alpha * l_run + jnp.sum(p, axis=-1, keepdims=True)
        acc = alpha * acc + jnp.dot(p.astype(BF16), v_ref[0, rs, :], preferred_element_type=F32)
        return m_new, l_new, acc

    _, l_fin, acc = lax.fori_loop(0, qi, past_block, (m0, l0, acc0))
    o_ref[0] = (acc / l_fin).astype(o_ref.dtype)


def moba_attention(qkv):
    b, s, _ = qkv.shape
    n_blocks = s // MOBA_BLOCK
    return pl.pallas_call(
        functools.partial(_moba_body, n_blocks=n_blocks),
        grid=(b, MOBA_HEADS, n_blocks),
        in_specs=[pl.BlockSpec((1, MOBA_BLOCK, MOBA_DH), lambda bi, h, t: (bi, t, h)),
                  pl.BlockSpec((1, s, MOBA_DH), lambda bi, h, t: (bi, 0, MOBA_HEADS + h)),
                  pl.BlockSpec((1, s, MOBA_DH), lambda bi, h, t: (bi, 0, 2 * MOBA_HEADS + h))],
        out_specs=pl.BlockSpec((1, MOBA_BLOCK, MOBA_DH), lambda bi, h, t: (bi, t, h)),
        out_shape=jax.ShapeDtypeStruct((b, s, D_MODEL), BF16),
        scratch_shapes=[pltpu.VMEM((LANES, MOBA_DH), F32)],
        compiler_params=_params("parallel", "parallel", "arbitrary"),
        name="moba_attention",
    )(qkv, qkv, qkv)


def _router_body(h_ref, g_ref, w_ref, b_ref, idx_ref, wgt_ref, cnt_ref, run_ref):
    step = pl.program_id(0)

    @pl.when(step == 0)
    def _():
        run_ref[...] = jnp.zeros_like(run_ref)

    tm = h_ref.shape[0]
    xn = _rms(h_ref[...], g_ref[...])
    logits = jnp.dot(xn, w_ref[...], precision=HIGHEST, preferred_element_type=F32) + b_ref[...]
    lane = lax.broadcasted_iota(I32, (tm, LANES), 1)

    def first_max(vals):
        top = jnp.max(vals, axis=-1, keepdims=True)
        where = jnp.min(jnp.where(vals == top, lane, LANES), axis=-1, keepdims=True)
        return top, where

    g_logits = jnp.where(lane < N_GROUPS, logits, NEG_INF)
    g_top, grp = first_max(g_logits)
    p_grp = 1.0 / jnp.sum(jnp.exp(g_logits - g_top), axis=-1, keepdims=True)
    lo = EXPERT_LANE0 + grp * EXPERTS_PER_GROUP
    e_logits = jnp.where((lane >= lo) & (lane < lo + EXPERTS_PER_GROUP), logits, NEG_INF)
    v0, j0 = first_max(e_logits)
    v1, j1 = first_max(jnp.where(lane == j0, NEG_INF, e_logits))
    t = jnp.exp(v1 - v0)
    w0 = p_grp / (1.0 + t)
    w1 = p_grp * t / (1.0 + t)

    hit0 = lane == j0
    hit1 = lane == j1
    member = jnp.where(hit0 | hit1, 1.0, 0.0)
    r_i = lax.broadcasted_iota(I32, (tm, tm), 0)
    c_i = lax.broadcasted_iota(I32, (tm, tm), 1)
    before = jnp.where(c_i < r_i, 1.0, 0.0).astype(BF16)
    prior = jnp.dot(before, member.astype(BF16), preferred_element_type=F32) + run_ref[...]
    rank0 = jnp.sum(jnp.where(hit0, prior, 0.0), axis=-1, keepdims=True).astype(I32)
    rank1 = jnp.sum(jnp.where(hit1, prior, 0.0), axis=-1, keepdims=True).astype(I32)
    run_ref[...] = run_ref[...] + jnp.sum(member, axis=0, keepdims=True)
    cnt_ref[...] = run_ref[...]

    idx_ref[...] = jnp.where(lane == 0, j0 - EXPERT_LANE0,
                             jnp.where(lane == 1, j1 - EXPERT_LANE0,
                                       jnp.where(lane == 2, rank0, jnp.where(lane == 3, rank1, 0))))
    wgt_ref[...] = jnp.where(lane == 0, w0, jnp.where(lane == 1, w1, 0.0))


def moe_router(h2d, g, w_cat, b_cat):
    n_tok, d = h2d.shape
    return pl.pallas_call(
        _router_body,
        grid=(n_tok // ROUTE_TM,),
        in_specs=[pl.BlockSpec((ROUTE_TM, d), lambda i: (i, 0)),
                  pl.BlockSpec((1, d), lambda i: (0, 0)),
                  pl.BlockSpec((d, LANES), lambda i: (0, 0)),
                  pl.BlockSpec((1, LANES), lambda i: (0, 0))],
        out_specs=[pl.BlockSpec((ROUTE_TM, LANES), lambda i: (i, 0)),
                   pl.BlockSpec((ROUTE_TM, LANES), lambda i: (i, 0)),
                   pl.BlockSpec((1, LANES), lambda i: (0, 0))],
        out_shape=[jax.ShapeDtypeStruct((n_tok, LANES), I32),
                   jax.ShapeDtypeStruct((n_tok, LANES), F32),
                   jax.ShapeDtypeStruct((1, LANES), F32)],
        scratch_shapes=[pltpu.VMEM((1, LANES), F32)],
        compiler_params=_params("arbitrary"),
        name="moe_router",
    )(h2d, g.reshape(1, d), w_cat, b_cat)


def _row_copy(src_hbm, src_row, dst, dst_row, sem):
    return pltpu.make_async_copy(src_hbm.at[pl.ds(src_row, 1)], dst.at[pl.ds(dst_row, 1)], sem)


def _dispatch_body(dest_ref, h_hbm, zero_hbm, xpad_hbm, sem):
    del zero_hbm
    base = pl.program_id(0) * DISPATCH_T

    def issue(t, carry):
        for slot in range(2):
            _row_copy(h_hbm, base + t, xpad_hbm, dest_ref[0, 0, 2 * t + slot], sem).start()
        return carry

    lax.fori_loop(0, DISPATCH_T, issue, 0)

    def drain(t, carry):
        for slot in range(2):
            _row_copy(h_hbm, base + t, xpad_hbm, dest_ref[0, 0, 2 * t + slot], sem).wait()
        return carry

    lax.fori_loop(0, DISPATCH_T, drain, 0)


def moe_dispatch(h2d, dest, n_rows):
    n_tok, d = h2d.shape
    steps = n_tok // DISPATCH_T
    return pl.pallas_call(
        _dispatch_body,
        grid=(steps,),
        in_specs=[pl.BlockSpec((1, 1, 2 * DISPATCH_T), lambda i: (i, 0, 0), memory_space=pltpu.SMEM),
                  pl.BlockSpec(memory_space=pl.ANY),
                  pl.BlockSpec(memory_space=pl.ANY)],
        out_specs=pl.BlockSpec(memory_space=pl.ANY),
        out_shape=jax.ShapeDtypeStruct((n_rows, d), F32),
        scratch_shapes=[pltpu.SemaphoreType.DMA(())],
        input_output_aliases={2: 0},
        compiler_params=_params("arbitrary"),
        name="moe_dispatch",
    )(dest.reshape(steps, 1, 2 * DISPATCH_T), h2d, jnp.zeros((n_rows, d), F32))


def _expert_body(be_ref, nv_ref, nu_ref, x_ref, g_ref, w1_ref, w3_ref, w2_ref, y_ref,
                 w1_bf, w3_bf, w2_bf):
    i = pl.program_id(0)

    @pl.when(i < nu_ref[0])
    def _():
        @pl.when((i == 0) | (be_ref[i] != be_ref[jnp.maximum(i - 1, 0)]))
        def _():
            w1_bf[...] = w1_ref[...].astype(BF16)
            w3_bf[...] = w3_ref[...].astype(BF16)
            w2_bf[...] = w2_ref[...].astype(BF16)

        rows = lax.broadcasted_iota(I32, (MOE_BM, 1), 0)
        x = jnp.where(rows < nv_ref[i], x_ref[...], 0.0)
        xn = _rms(x, g_ref[...]).astype(BF16)
        h1 = jnp.dot(xn, w1_bf[...], preferred_element_type=F32)
        h3 = jnp.dot(xn, w3_bf[...], preferred_element_type=F32)
        act = (_silu(h1) * h3).astype(BF16)
        y_ref[...] = jnp.dot(act, w2_bf[...], preferred_element_type=F32)

    @pl.when(i >= nu_ref[0])
    def _():
        y_ref[...] = jnp.zeros_like(y_ref)


def moe_experts(x_pad, g, w1, w3, w2, layer, block_e, n_valid, n_used):
    n_rows, d = x_pad.shape
    n_blocks = n_rows // MOE_BM
    f = w1.shape[-1]

    def row_map(i, be, nv, nu):
        return (jnp.minimum(i, nu[0] - 1), 0)

    def w_map(i, be, nv, nu):
        return (layer, be[i], 0, 0)

    grid_spec = pltpu.PrefetchScalarGridSpec(
        num_scalar_prefetch=3,
        grid=(n_blocks,),
        in_specs=[pl.BlockSpec((MOE_BM, d), row_map),
                  pl.BlockSpec((1, d), lambda i, be, nv, nu: (0, 0)),
                  pl.BlockSpec((None, None, d, f), w_map),
                  pl.BlockSpec((None, None, d, f), w_map),
                  pl.BlockSpec((None, None, f, d), w_map)],
        out_specs=pl.BlockSpec((MOE_BM, d), lambda i, be, nv, nu: (i, 0)),
        scratch_shapes=[pltpu.VMEM((d, f), BF16), pltpu.VMEM((d, f), BF16), pltpu.VMEM((f, d), BF16)],
    )
    return pl.pallas_call(
        _expert_body,
        grid_spec=grid_spec,
        out_shape=jax.ShapeDtypeStruct((n_rows, d), F32),
        compiler_params=_params("arbitrary"),
        name="moe_experts",
    )(block_e, n_valid, n_used, x_pad, g.reshape(1, d), w1, w3, w2)


def _combine_body(*refs, final_norm):
    if final_norm:
        dest_ref, h_ref, wgt_ref, gf_ref, y_hbm, o_ref, buf, sem = refs
    else:
        dest_ref, h_ref, wgt_ref, y_hbm, o_ref, buf, sem = refs

    def issue(t, carry):
        for slot in range(2):
            _row_copy(y_hbm, dest_ref[0, 0, 2 * t + slot], buf.at[slot], t, sem).start()
        return carry

    lax.fori_loop(0, COMBINE_T, issue, 0)

    def drain(t, carry):
        for slot in range(2):
            _row_copy(y_hbm, dest_ref[0, 0, 2 * t + slot], buf.at[slot], t, sem).wait()
        return carry

    lax.fori_loop(0, COMBINE_T, drain, 0)

    wgt = wgt_ref[...]
    out = h_ref[...] + wgt[:, 0:1] * buf[0] + wgt[:, 1:2] * buf[1]
    if final_norm:
        out = _rms(out, gf_ref[...])
    o_ref[...] = out


def moe_combine(h2d, wgt, dest, y_pad, g_final=None):
    n_tok, d = h2d.shape
    steps = n_tok // COMBINE_T
    final_norm = g_final is not None
    in_specs = [pl.BlockSpec((1, 1, 2 * COMBINE_T), lambda i: (i, 0, 0), memory_space=pltpu.SMEM),
                pl.BlockSpec((COMBINE_T, d), lambda i: (i, 0)),
                pl.BlockSpec((COMBINE_T, LANES), lambda i: (i, 0))]
    args = [dest.reshape(steps, 1, 2 * COMBINE_T), h2d, wgt]
    if final_norm:
        in_specs.append(pl.BlockSpec((1, d), lambda i: (0, 0)))
        args.append(g_final.reshape(1, d))
    in_specs.append(pl.BlockSpec(memory_space=pl.ANY))
    args.append(y_pad)
    return pl.pallas_call(
        functools.partial(_combine_body, final_norm=final_norm),
        grid=(steps,),
        in_specs=in_specs,
        out_specs=pl.BlockSpec((COMBINE_T, d), lambda i: (i, 0)),
        out_shape=jax.ShapeDtypeStruct((n_tok, d), F32),
        scratch_shapes=[pltpu.VMEM((2, COMBINE_T, d), F32), pltpu.SemaphoreType.DMA(())],
        compiler_params=_params("arbitrary"),
        name="moe_combine_final" if final_norm else "moe_combine",
    )(*args)


def hierarchical_moe(h2d, layer, norm_g, w_group, b_group, w_router, b_router, w1, w3, w2, g_final=None):
    n_tok, d = h2d.shape
    pad_l = LANES - N_GROUPS - N_EXPERTS
    w_cat = jnp.concatenate([w_group, w_router, jnp.zeros((d, pad_l), F32)], axis=1)
    b_cat = jnp.concatenate([b_group, b_router, jnp.zeros((pad_l,), F32)]).reshape(1, LANES)
    idx, wgt, cnt = moe_router(h2d, norm_g, w_cat, b_cat)

    counts = cnt[0, EXPERT_LANE0:EXPERT_LANE0 + N_EXPERTS].astype(I32)
    padded = (counts + MOE_BM - 1) // MOE_BM * MOE_BM
    pends = jnp.cumsum(padded)
    pstarts = pends - padded
    n_blocks = (n_tok * 2) // MOE_BM + N_EXPERTS
    blk_row0 = jnp.arange(n_blocks, dtype=I32) * MOE_BM
    block_e = jnp.minimum(jnp.searchsorted(pends, blk_row0, side="right"), N_EXPERTS - 1).astype(I32)
    n_valid = jnp.clip(pstarts[block_e] + counts[block_e] - blk_row0, 0, MOE_BM).astype(I32)
    n_used = (pends[-1:] // MOE_BM).astype(I32)
    dest = (pstarts[idx[:, 0:2]] + idx[:, 2:4]).astype(I32)

    x_pad = moe_dispatch(h2d, dest, n_blocks * MOE_BM)
    y_pad = moe_experts(x_pad, norm_g, w1, w3, w2, layer, block_e, n_valid, n_used)
    return moe_combine(h2d, wgt, dest, y_pad, g_final)


def kernel(x, norm_mix, norm_ffn, norm_final, w_in_even, hg_lb_logits, hg_norm, gla_gk2, gla_gk_bias,
           gla_norm, w_out_even, w_qkv_odd, w_o_odd, router_group_w, router_group_b, router_expert_w,
           router_expert_b, expert_w1, expert_w3, expert_w2):
    b, s, d = x.shape
    n_tok = b * s
    depth = norm_mix.shape[0]
    lb_table = jnp.cumsum(jax.nn.softmax(hg_lb_logits.astype(F32), axis=0), axis=0)
    h = x.reshape(n_tok, d)
    for l in range(depth):
        if l % 2 == 0:
            e = l // 2
            w_in = w_in_even[e]
            w_low = jnp.pad(w_in[:, MAIN_IN:], ((0, 0), (0, LANES - GLA_RANK))).astype(BF16)
            proj, glow = norm_matmul(h, norm_mix[l], w_in[:, :MAIN_IN].astype(BF16), w_low, tm=512, tn=1024)
            proj = proj.reshape(b, s, MAIN_IN)
            gk2_pad = jnp.pad(gla_gk2[e], ((0, LANES - GLA_RANK), (0, 0)))
            o_hg = hgrn2_mix(proj, lb_table[l], hg_norm[e])
            o_gla = gla_mix(proj, glow.reshape(b, s, LANES), gk2_pad, gla_gk_bias[e], gla_norm[e])
            mixed = jnp.concatenate([o_hg, o_gla], axis=-1).reshape(n_tok, d)
            h = matmul_residual(mixed, w_out_even[e].astype(BF16), h, tm=512)
        else:
            o = l // 2
            qkv = norm_matmul(h, norm_mix[l], w_qkv_odd[o].astype(BF16), tm=512, tn=1024)
            attn = moba_attention(qkv.reshape(b, s, 3 * d))
            h = matmul_residual(attn.reshape(n_tok, d), w_o_odd[o].astype(BF16), h, tm=512)
        h = hierarchical_moe(h, l, norm_ffn[l], router_group_w[l], router_group_b[l], router_expert_w[l],
                             router_expert_b[l], expert_w1, expert_w3, expert_w2,
                             g_final=norm_final if l == depth - 1 else None)
    return h.reshape(b, s, d)
```

```python
import functools

import jax
import jax.numpy as jnp
from jax import lax
from jax.experimental import pallas as pl
from jax.experimental.pallas import tpu as pltpu

F32 = jnp.float32
BF16 = jnp.bfloat16
I32 = jnp.int32
HIGHEST = lax.Precision.HIGHEST
NEG_INF = float("-inf")
LOG2_E = 1.4426950408889634

EPS = 1e-6
D_MODEL = 2048

HG_HEADS, HG_DK, HG_DV = 8, 128, 128
GLA_HEADS, GLA_DK, GLA_DV = 4, 128, 256
GLA_RANK = 16
GLA_GATE_NORM = 16.0
HG_QK = HG_HEADS * HG_DK
MAIN_IN = 4 * HG_QK + 2 * GLA_HEADS * GLA_DK + 2 * GLA_HEADS * GLA_DV
CHUNK = 64
SUB = 16
REC_T = 256

MOBA_HEADS, MOBA_DH = 16, 128
MOBA_BLOCK = 256
MOBA_TOPK = 3
MOBA_HP = 2
MASK_BIAS = -1e30

N_GROUPS, EXPERTS_PER_GROUP = 4, 8
N_EXPERTS = N_GROUPS * EXPERTS_PER_GROUP
D_EXPERT = D_MODEL // 4
EXPERT_LANE0 = N_GROUPS
ROUTE_TM = 256
MOE_BM = 256
COMBINE_T = 128

LANES = 128
VMEM_LIMIT = 56 * 1024 * 1024

NT_DIMS = (((1,), (1,)), ((), ()))
TN_DIMS = (((0,), (0,)), ((), ()))


def _params(*sem):
    return pltpu.CompilerParams(dimension_semantics=sem, vmem_limit_bytes=VMEM_LIMIT)


def _rms(x, g):
    return x * lax.rsqrt(jnp.mean(x * x, axis=-1, keepdims=True) + EPS) * g


def _silu(x):
    return x * jax.nn.sigmoid(x)


def _norm_matmul_body(*refs, has_side):
    if has_side:
        x_ref, g_ref, w_ref, ws_ref, o_ref, os_ref, xn_ref = refs
    else:
        x_ref, g_ref, w_ref, o_ref, xn_ref = refs

    @pl.when(pl.program_id(1) == 0)
    def _():
        xn_ref[...] = _rms(x_ref[...], g_ref[...]).astype(BF16)
        if has_side:
            os_ref[...] = jnp.dot(xn_ref[...], ws_ref[...], preferred_element_type=F32)

    o_ref[...] = jnp.dot(xn_ref[...], w_ref[...], preferred_element_type=F32).astype(o_ref.dtype)


def norm_matmul(x, g, w, w_side=None, *, tm, tn):
    m, k = x.shape
    n = w.shape[1]
    has_side = w_side is not None
    in_specs = [pl.BlockSpec((tm, k), lambda i, j: (i, 0)),
                pl.BlockSpec((1, k), lambda i, j: (0, 0)),
                pl.BlockSpec((k, tn), lambda i, j: (0, j))]
    out_specs = [pl.BlockSpec((tm, tn), lambda i, j: (i, j))]
    out_shape = [jax.ShapeDtypeStruct((m, n), BF16)]
    args = [x, g.reshape(1, k), w]
    if has_side:
        ns = w_side.shape[1]
        in_specs.append(pl.BlockSpec((k, ns), lambda i, j: (0, 0)))
        out_specs.append(pl.BlockSpec((tm, ns), lambda i, j: (i, 0)))
        out_shape.append(jax.ShapeDtypeStruct((m, ns), F32))
        args.append(w_side)
    outs = pl.pallas_call(
        functools.partial(_norm_matmul_body, has_side=has_side),
        grid=(m // tm, n // tn),
        in_specs=in_specs, out_specs=out_specs, out_shape=out_shape,
        scratch_shapes=[pltpu.VMEM((tm, k), BF16)],
        compiler_params=_params("parallel", "arbitrary"),
        name="norm_matmul_side" if has_side else "norm_matmul",
    )(*args)
    return outs if has_side else outs[0]


def _matmul_res_body(a_ref, w_ref, r_ref, o_ref):
    o_ref[...] = r_ref[...] + jnp.dot(a_ref[...], w_ref[...], preferred_element_type=F32)


def matmul_residual(a, w, res, *, tm):
    m, k = a.shape
    n = w.shape[1]
    return pl.pallas_call(
        _matmul_res_body,
        grid=(m // tm,),
        in_specs=[pl.BlockSpec((tm, k), lambda i: (i, 0)),
                  pl.BlockSpec((k, n), lambda i: (0, 0)),
                  pl.BlockSpec((tm, n), lambda i: (i, 0))],
        out_specs=pl.BlockSpec((tm, n), lambda i: (i, 0)),
        out_shape=jax.ShapeDtypeStruct((m, n), F32),
        compiler_params=_params("parallel"),
        name="matmul_residual",
    )(a, w, res)


def _recurrence_chunk(q, k, v, la, st_ref):
    r_i = lax.broadcasted_iota(I32, (CHUNK, CHUNK), 0)
    c_i = lax.broadcasted_iota(I32, (CHUNK, CHUNK), 1)
    tri = (c_i <= r_i).astype(F32)
    cum = jnp.dot(tri, la, precision=HIGHEST, preferred_element_type=F32)
    last = cum[CHUNK - 1:CHUNK, :]
    qg = (q * jnp.exp(cum)).astype(BF16)
    kg = (k * jnp.exp(last - cum)).astype(BF16)
    vb = v.astype(BF16)
    st = st_ref[...]
    o_inter = lax.dot_general(qg, st.astype(BF16), NT_DIMS, preferred_element_type=F32)
    st_ref[...] = st * jnp.exp(last) + lax.dot_general(vb, kg, TN_DIMS, preferred_element_type=F32)

    rows = lax.broadcasted_iota(I32, (SUB, 1), 0)
    outs = []
    for blk in range(CHUNK // SUB):
        lo = blk * SUB
        q_b, k_b, v_b, c_b = q[lo:lo + SUB], k[lo:lo + SUB], v[lo:lo + SUB], cum[lo:lo + SUB]
        acc = o_inter[lo:lo + SUB]
        if blk > 0:
            ref_pt = cum[lo - 1:lo, :]
            q_t = (q_b * jnp.exp(c_b - ref_pt)).astype(BF16)
            k_t = (k[:lo] * jnp.exp(ref_pt - cum[:lo])).astype(BF16)
            s = lax.dot_general(q_t, k_t, NT_DIMS, preferred_element_type=F32)
            acc = acc + jnp.dot(s.astype(BF16), vb[:lo], preferred_element_type=F32)
        for j in range(SUB):
            e = jnp.exp(jnp.where(rows >= j, c_b - c_b[j:j + 1, :], NEG_INF))
            col = jnp.sum(q_b * k_b[j:j + 1, :] * e, axis=-1, keepdims=True)
            acc = acc + col * v_b[j:j + 1, :]
        outs.append(acc)
    return jnp.concatenate(outs, axis=0)


def _log_sigmoid(z):
    return jnp.minimum(z, 0.0) - jnp.log(1.0 + jnp.exp(-jnp.abs(z)))


def _hgrn2_body(hq_ref, hf_ref, hi_ref, hg_ref, lb_ref, nw_ref, o_ref, st_ref):
    @pl.when(pl.program_id(2) == 0)
    def _():
        st_ref[...] = jnp.zeros_like(st_ref)

    lb = lb_ref[...]
    nw = nw_ref[...]

    def chunk(c, carry):
        rs = pl.ds(pl.multiple_of(c * CHUNK, CHUNK), CHUNK)
        forget = lb + (1.0 - lb) * jax.nn.sigmoid(hf_ref[0, rs, :].astype(F32))
        q = _silu(hq_ref[0, rs, :].astype(F32)) * (HG_DK ** -0.5)
        o = _recurrence_chunk(q, 1.0 - forget, hi_ref[0, rs, :].astype(F32), jnp.log(forget), st_ref)
        o_ref[0, rs, :] = (_rms(o, nw) * _silu(hg_ref[0, rs, :].astype(F32))).astype(o_ref.dtype)
        return carry

    lax.fori_loop(0, REC_T // CHUNK, chunk, 0)


def _gla_body(gq_ref, gk_ref, gv_ref, gg_ref, glow_ref, gk2_ref, gb_ref, nw_ref, o_ref, st_ref):
    @pl.when(pl.program_id(2) == 0)
    def _():
        st_ref[...] = jnp.zeros_like(st_ref)

    gk2 = gk2_ref[...]
    gb = gb_ref[...]
    nw = nw_ref[...]

    def chunk(c, carry):
        rs = pl.ds(pl.multiple_of(c * CHUNK, CHUNK), CHUNK)
        z = jnp.dot(glow_ref[0, rs, :], gk2, precision=HIGHEST, preferred_element_type=F32) + gb
        la = _log_sigmoid(z) / GLA_GATE_NORM
        q = gq_ref[0, rs, :].astype(F32) * (GLA_DK ** -0.5)
        o = _recurrence_chunk(q, gk_ref[0, rs, :].astype(F32), gv_ref[0, rs, :].astype(F32), la, st_ref)
        o_ref[0, rs, :] = (_rms(o, nw) * _silu(gg_ref[0, rs, :].astype(F32))).astype(o_ref.dtype)
        return carry

    lax.fori_loop(0, REC_T // CHUNK, chunk, 0)


def hgrn2_mix(proj, lb, hg_norm):
    b, s, _ = proj.shape

    def col(base):
        return pl.BlockSpec((1, REC_T, HG_DK), lambda bi, h, t: (bi, t, base + h))

    return pl.pallas_call(
        _hgrn2_body,
        grid=(b, HG_HEADS, s // REC_T),
        in_specs=[col(0), col(HG_HEADS), col(2 * HG_HEADS), col(3 * HG_HEADS),
                  pl.BlockSpec((1, HG_DK), lambda bi, h, t: (0, h)),
                  pl.BlockSpec((1, HG_DV), lambda bi, h, t: (0, 0))],
        out_specs=pl.BlockSpec((1, REC_T, HG_DV), lambda bi, h, t: (bi, t, h)),
        out_shape=jax.ShapeDtypeStruct((b, s, HG_HEADS * HG_DV), BF16),
        scratch_shapes=[pltpu.VMEM((HG_DV, HG_DK), F32)],
        compiler_params=_params("parallel", "parallel", "arbitrary"),
        name="hgrn2_mix",
    )(proj, proj, proj, proj, lb.reshape(1, HG_QK), hg_norm.reshape(1, HG_DV))


def gla_mix(proj, glow, gk2_pad, gk_bias, gla_norm):
    b, s, _ = proj.shape
    q0 = 4 * HG_QK // GLA_DK
    k0 = q0 + GLA_HEADS
    v0 = (4 * HG_QK + 2 * GLA_HEADS * GLA_DK) // GLA_DV
    g0 = v0 + GLA_HEADS
    return pl.pallas_call(
        _gla_body,
        grid=(b, GLA_HEADS, s // REC_T),
        in_specs=[pl.BlockSpec((1, REC_T, GLA_DK), lambda bi, h, t: (bi, t, q0 + h)),
                  pl.BlockSpec((1, REC_T, GLA_DK), lambda bi, h, t: (bi, t, k0 + h)),
                  pl.BlockSpec((1, REC_T, GLA_DV), lambda bi, h, t: (bi, t, v0 + h)),
                  pl.BlockSpec((1, REC_T, GLA_DV), lambda bi, h, t: (bi, t, g0 + h)),
                  pl.BlockSpec((1, REC_T, LANES), lambda bi, h, t: (bi, t, 0)),
                  pl.BlockSpec((LANES, GLA_DK), lambda bi, h, t: (0, h)),
                  pl.BlockSpec((1, GLA_DK), lambda bi, h, t: (0, h)),
                  pl.BlockSpec((1, GLA_DV), lambda bi, h, t: (0, 0))],
        out_specs=pl.BlockSpec((1, REC_T, GLA_DV), lambda bi, h, t: (bi, t, h)),
        out_shape=jax.ShapeDtypeStruct((b, s, GLA_HEADS * GLA_DV), BF16),
        scratch_shapes=[pltpu.VMEM((GLA_DV, GLA_DK), F32)],
        compiler_params=_params("parallel", "parallel", "arbitrary"),
        name="gla_mix",
    )(proj, proj, proj, proj, glow, gk2_pad, gk_bias.reshape(1, -1), gla_norm.reshape(1, GLA_DV))


def _moba_body(q_ref, k_ref, v_ref, o_ref, kmean_ref, vt_ref, bias_ref, *, n_blocks):
    qi = pl.program_id(2)
    blk, dh = MOBA_BLOCK, MOBA_DH
    heads = [slice(hh * dh, (hh + 1) * dh) for hh in range(MOBA_HP)]

    @pl.when(qi == 0)
    def _():
        for n in range(n_blocks):
            rs = slice(n * blk, (n + 1) * blk)
            for hh, cols in enumerate(heads):
                kmean_ref[hh, n:n + 1, :] = jnp.mean(k_ref[0, rs, cols].astype(F32), axis=0, keepdims=True)
                vt_ref[hh, :, rs] = v_ref[0, rs, cols].astype(F32).T.astype(BF16)

    c = (MOBA_DH ** -0.5) * LOG2_E
    own = pl.ds(pl.multiple_of(qi * blk, blk), blk)
    brow = lax.broadcasted_iota(I32, (n_blocks, blk), 0)
    past = brow < qi
    k_i = lax.broadcasted_iota(I32, (blk, blk), 0)
    q_i = lax.broadcasted_iota(I32, (blk, blk), 1)

    qs, carry0 = [], []
    for hh, cols in enumerate(heads):
        q = q_ref[0, :, cols]
        qs.append(q)
        gate = lax.dot_general(kmean_ref[hh], q.astype(F32), NT_DIMS, precision=HIGHEST,
                               preferred_element_type=F32)
        gate = jnp.where(past, gate, NEG_INF)
        rank = jnp.zeros((n_blocks, blk), I32)
        for m in range(n_blocks):
            gm = gate[m:m + 1, :]
            rank = rank + jnp.where(gm > gate, 1, jnp.where(gm == gate, jnp.where(brow > m, 1, 0), 0))
        keep = jnp.where(past, jnp.where(rank < MOBA_TOPK, 1, 0), 0)
        bias_ref[hh] = jnp.where(keep > 0, 0.0, MASK_BIAS)
        s = lax.dot_general(k_ref[0, own, cols], q, NT_DIMS, preferred_element_type=F32)
        s = jnp.where(k_i <= q_i, s, NEG_INF)
        m0 = jnp.max(s, axis=0, keepdims=True)
        p = jnp.exp2((s - m0) * c)
        l0 = jnp.sum(p, axis=0, keepdims=True)
        acc0 = jnp.dot(vt_ref[hh, :, own], p.astype(BF16), preferred_element_type=F32)
        carry0 += [m0, l0, acc0]

    def past_pair(j, carry):
        n0 = 2 * j
        r0 = pl.ds(pl.multiple_of(n0 * blk, blk), blk)
        r1 = pl.ds(pl.multiple_of(n0 * blk + blk, blk), blk)
        r01 = pl.ds(pl.multiple_of(n0 * blk, 2 * blk), 2 * blk)
        out = []
        for hh, cols in enumerate(heads):
            m_run, l_run, acc = carry[3 * hh:3 * hh + 3]
            s0 = (lax.dot_general(k_ref[0, r0, cols], qs[hh], NT_DIMS, preferred_element_type=F32)
                  + bias_ref[hh, pl.ds(n0, 1), :])
            s1 = (lax.dot_general(k_ref[0, r1, cols], qs[hh], NT_DIMS, preferred_element_type=F32)
                  + bias_ref[hh, pl.ds(n0 + 1, 1), :])
            m_new = jnp.maximum(m_run, jnp.maximum(jnp.max(s0, axis=0, keepdims=True),
                                                   jnp.max(s1, axis=0, keepdims=True)))
            alpha = jnp.exp2((m_run - m_new) * c)
            p0 = jnp.exp2((s0 - m_new) * c)
            p1 = jnp.exp2((s1 - m_new) * c)
            l_new = alpha * l_run + jnp.sum(p0, axis=0, keepdims=True) + jnp.sum(p1, axis=0, keepdims=True)
            p01 = jnp.concatenate([p0.astype(BF16), p1.astype(BF16)], axis=0)
            acc = alpha * acc + jnp.dot(vt_ref[hh, :, r01], p01, preferred_element_type=F32)
            out += [m_new, l_new, acc]
        return tuple(out)

    fin = lax.fori_loop(0, (qi + 1) // 2, past_pair, tuple(carry0))
    for hh, cols in enumerate(heads):
        o_ref[0, :, cols] = (fin[3 * hh + 2] / fin[3 * hh + 1]).T.astype(o_ref.dtype)


def moba_attention(qkv):
    b, s, _ = qkv.shape
    n_blocks = s // MOBA_BLOCK
    hw = MOBA_HP * MOBA_DH
    hsteps = MOBA_HEADS // MOBA_HP
    return pl.pallas_call(
        functools.partial(_moba_body, n_blocks=n_blocks),
        grid=(b, hsteps, n_blocks),
        in_specs=[pl.BlockSpec((1, MOBA_BLOCK, hw), lambda bi, h, t: (bi, t, h)),
                  pl.BlockSpec((1, s, hw), lambda bi, h, t: (bi, 0, hsteps + h)),
                  pl.BlockSpec((1, s, hw), lambda bi, h, t: (bi, 0, 2 * hsteps + h))],
        out_specs=pl.BlockSpec((1, MOBA_BLOCK, hw), lambda bi, h, t: (bi, t, h)),
        out_shape=jax.ShapeDtypeStruct((b, s, D_MODEL), BF16),
        scratch_shapes=[pltpu.VMEM((MOBA_HP, n_blocks, MOBA_DH), F32),
                        pltpu.VMEM((MOBA_HP, MOBA_DH, s), BF16),
                        pltpu.VMEM((MOBA_HP, n_blocks, MOBA_BLOCK), F32)],
        compiler_params=_params("parallel", "parallel", "arbitrary"),
        name="moba_attention",
    )(qkv, qkv, qkv)


def _router_body(h_ref, g_ref, w_ref, b_ref, idx_ref, wgt_ref, cnt_ref, run_ref):
    step = pl.program_id(0)

    @pl.when(step == 0)
    def _():
        run_ref[...] = jnp.zeros_like(run_ref)

    tm = h_ref.shape[0]
    xn = _rms(h_ref[...], g_ref[...])
    logits = jnp.dot(xn, w_ref[...], precision=HIGHEST, preferred_element_type=F32) + b_ref[...]
    lane = lax.broadcasted_iota(I32, (tm, LANES), 1)

    def first_max(vals):
        top = jnp.max(vals, axis=-1, keepdims=True)
        where = jnp.min(jnp.where(vals == top, lane, LANES), axis=-1, keepdims=True)
        return top, where

    g_logits = jnp.where(lane < N_GROUPS, logits, NEG_INF)
    g_top, grp = first_max(g_logits)
    p_grp = 1.0 / jnp.sum(jnp.exp(g_logits - g_top), axis=-1, keepdims=True)
    lo = EXPERT_LANE0 + grp * EXPERTS_PER_GROUP
    e_logits = jnp.where((lane >= lo) & (lane < lo + EXPERTS_PER_GROUP), logits, NEG_INF)
    v0, j0 = first_max(e_logits)
    v1, j1 = first_max(jnp.where(lane == j0, NEG_INF, e_logits))
    t = jnp.exp(v1 - v0)
    w0 = p_grp / (1.0 + t)
    w1 = p_grp * t / (1.0 + t)

    hit0 = lane == j0
    hit1 = lane == j1
    member = jnp.where(hit0 | hit1, 1.0, 0.0)
    r_i = lax.broadcasted_iota(I32, (tm, tm), 0)
    c_i = lax.broadcasted_iota(I32, (tm, tm), 1)
    before = jnp.where(c_i < r_i, 1.0, 0.0).astype(BF16)
    prior = jnp.dot(before, member.astype(BF16), preferred_element_type=F32) + run_ref[...]
    rank0 = jnp.sum(jnp.where(hit0, prior, 0.0), axis=-1, keepdims=True).astype(I32)
    rank1 = jnp.sum(jnp.where(hit1, prior, 0.0), axis=-1, keepdims=True).astype(I32)
    run_ref[...] = run_ref[...] + jnp.sum(member, axis=0, keepdims=True)
    cnt_ref[...] = run_ref[...]

    idx_ref[...] = jnp.where(lane == 0, j0 - EXPERT_LANE0,
                             jnp.where(lane == 1, j1 - EXPERT_LANE0,
                                       jnp.where(lane == 2, rank0, jnp.where(lane == 3, rank1, 0))))
    wgt_ref[...] = jnp.where(lane == 0, w0, jnp.where(lane == 1, w1, 0.0))


def moe_router(h2d, g, w_cat, b_cat):
    n_tok, d = h2d.shape
    return pl.pallas_call(
        _router_body,
        grid=(n_tok // ROUTE_TM,),
        in_specs=[pl.BlockSpec((ROUTE_TM, d), lambda i: (i, 0)),
                  pl.BlockSpec((1, d), lambda i: (0, 0)),
                  pl.BlockSpec((d, LANES), lambda i: (0, 0)),
                  pl.BlockSpec((1, LANES), lambda i: (0, 0))],
        out_specs=[pl.BlockSpec((ROUTE_TM, LANES), lambda i: (i, 0)),
                   pl.BlockSpec((ROUTE_TM, LANES), lambda i: (i, 0)),
                   pl.BlockSpec((1, LANES), lambda i: (0, 0))],
        out_shape=[jax.ShapeDtypeStruct((n_tok, LANES), I32),
                   jax.ShapeDtypeStruct((n_tok, LANES), F32),
                   jax.ShapeDtypeStruct((1, LANES), F32)],
        scratch_shapes=[pltpu.VMEM((1, LANES), F32)],
        compiler_params=_params("arbitrary"),
        name="moe_router",
    )(h2d, g.reshape(1, d), w_cat, b_cat)


def _row_copy(src_hbm, src_row, dst, dst_row, sem):
    return pltpu.make_async_copy(src_hbm.at[pl.ds(src_row, 1)], dst.at[pl.ds(dst_row, 1)], sem)


INV_UNROLL = 8


def _expert_body(be_ref, nv_ref, nu_ref, dest_ref, h_hbm, g_ref, w1_ref, w3_ref, w2_ref, y_ref,
                 inv_ref, xbuf, sems, w1_bf, w3_bf, w2_bf):
    i = pl.program_id(0)
    n_used = nu_ref[0]

    def gather_start(block, slot):
        def body(r, carry):
            _row_copy(h_hbm, inv_ref[block * MOE_BM + r], xbuf.at[slot], r, sems.at[slot]).start()
            return carry
        lax.fori_loop(0, nv_ref[block], body, 0)

    def gather_wait(block, slot):
        def body(r, carry):
            _row_copy(h_hbm, 0, xbuf.at[slot], r, sems.at[slot]).wait()
            return carry
        lax.fori_loop(0, nv_ref[block], body, 0)

    @pl.when(i == 0)
    def _():
        def body(c, carry):
            for u in range(INV_UNROLL):
                a = c * INV_UNROLL + u
                inv_ref[dest_ref[a]] = a // 2
            return carry
        lax.fori_loop(0, dest_ref.shape[0] // INV_UNROLL, body, 0)
        gather_start(0, 0)

    @pl.when(i < n_used)
    def _():
        slot = i % 2

        @pl.when(i + 1 < n_used)
        def _():
            gather_start(i + 1, 1 - slot)

        gather_wait(i, slot)

        @pl.when((i == 0) | (be_ref[i] != be_ref[jnp.maximum(i - 1, 0)]))
        def _():
            w1_bf[...] = w1_ref[...].astype(BF16)
            w3_bf[...] = w3_ref[...].astype(BF16)
            w2_bf[...] = w2_ref[...].astype(BF16)

        rows = lax.broadcasted_iota(I32, (MOE_BM, 1), 0)
        x = jnp.where(rows < nv_ref[i], xbuf[slot], 0.0)
        xn = _rms(x, g_ref[...]).astype(BF16)
        h1 = jnp.dot(xn, w1_bf[...], preferred_element_type=F32)
        h3 = jnp.dot(xn, w3_bf[...], preferred_element_type=F32)
        act = (_silu(h1) * h3).astype(BF16)
        y_ref[...] = jnp.dot(act, w2_bf[...], preferred_element_type=F32)

    @pl.when(i >= nu_ref[0])
    def _():
        y_ref[...] = jnp.zeros_like(y_ref)


def moe_experts(h2d, dest_flat, g, w1, w3, w2, layer, block_e, n_valid, n_used):
    n_tok, d = h2d.shape
    n_blocks = block_e.shape[0]
    n_rows = n_blocks * MOE_BM
    f = w1.shape[-1]

    def w_map(i, be, nv, nu, dest):
        return (layer, be[i], 0, 0)

    grid_spec = pltpu.PrefetchScalarGridSpec(
        num_scalar_prefetch=4,
        grid=(n_blocks,),
        in_specs=[pl.BlockSpec(memory_space=pl.ANY),
                  pl.BlockSpec((1, d), lambda i, be, nv, nu, dest: (0, 0)),
                  pl.BlockSpec((None, None, d, f), w_map),
                  pl.BlockSpec((None, None, d, f), w_map),
                  pl.BlockSpec((None, None, f, d), w_map)],
        out_specs=pl.BlockSpec((MOE_BM, d), lambda i, be, nv, nu, dest: (i, 0)),
        scratch_shapes=[pltpu.SMEM((n_rows,), I32),
                        pltpu.VMEM((2, MOE_BM, d), F32),
                        pltpu.SemaphoreType.DMA((2,)),
                        pltpu.VMEM((d, f), BF16), pltpu.VMEM((d, f), BF16), pltpu.VMEM((f, d), BF16)],
    )
    return pl.pallas_call(
        _expert_body,
        grid_spec=grid_spec,
        out_shape=jax.ShapeDtypeStruct((n_rows, d), F32),
        compiler_params=_params("arbitrary"),
        name="moe_experts",
    )(block_e, n_valid, n_used, dest_flat, h2d, g.reshape(1, d), w1, w3, w2)


def _combine_body(*refs, final_norm):
    if final_norm:
        dest_ref, h_ref, wgt_ref, gf_ref, y_hbm, o_ref, buf, sem = refs
    else:
        dest_ref, h_ref, wgt_ref, y_hbm, o_ref, buf, sem = refs

    def issue(t, carry):
        for slot in range(2):
            _row_copy(y_hbm, dest_ref[0, 0, 2 * t + slot], buf.at[slot], t, sem).start()
        return carry

    lax.fori_loop(0, COMBINE_T, issue, 0)

    def drain(t, carry):
        for slot in range(2):
            _row_copy(y_hbm, dest_ref[0, 0, 2 * t + slot], buf.at[slot], t, sem).wait()
        return carry

    lax.fori_loop(0, COMBINE_T, drain, 0)

    wgt = wgt_ref[...]
    out = h_ref[...] + wgt[:, 0:1] * buf[0] + wgt[:, 1:2] * buf[1]
    if final_norm:
        out = _rms(out, gf_ref[...])
    o_ref[...] = out


def moe_combine(h2d, wgt, dest, y_pad, g_final=None):
    n_tok, d = h2d.shape
    steps = n_tok // COMBINE_T
    final_norm = g_final is not None
    in_specs = [pl.BlockSpec((1, 1, 2 * COMBINE_T), lambda i: (i, 0, 0), memory_space=pltpu.SMEM),
                pl.BlockSpec((COMBINE_T, d), lambda i: (i, 0)),
                pl.BlockSpec((COMBINE_T, LANES), lambda i: (i, 0))]
    args = [dest.reshape(steps, 1, 2 * COMBINE_T), h2d, wgt]
    if final_norm:
        in_specs.append(pl.BlockSpec((1, d), lambda i: (0, 0)))
        args.append(g_final.reshape(1, d))
    in_specs.append(pl.BlockSpec(memory_space=pl.ANY))
    args.append(y_pad)
    return pl.pallas_call(
        functools.partial(_combine_body, final_norm=final_norm),
        grid=(steps,),
        in_specs=in_specs,
        out_specs=pl.BlockSpec((COMBINE_T, d), lambda i: (i, 0)),
        out_shape=jax.ShapeDtypeStruct((n_tok, d), F32),
        scratch_shapes=[pltpu.VMEM((2, COMBINE_T, d), F32), pltpu.SemaphoreType.DMA(())],
        compiler_params=_params("arbitrary"),
        name="moe_combine_final" if final_norm else "moe_combine",
    )(*args)


def hierarchical_moe(h2d, layer, norm_g, w_group, b_group, w_router, b_router, w1, w3, w2, g_final=None):
    n_tok, d = h2d.shape
    pad_l = LANES - N_GROUPS - N_EXPERTS
    w_cat = jnp.concatenate([w_group, w_router, jnp.zeros((d, pad_l), F32)], axis=1)
    b_cat = jnp.concatenate([b_group, b_router, jnp.zeros((pad_l,), F32)]).reshape(1, LANES)
    idx, wgt, cnt = moe_router(h2d, norm_g, w_cat, b_cat)

    counts = cnt[0, EXPERT_LANE0:EXPERT_LANE0 + N_EXPERTS].astype(I32)
    padded = (counts + MOE_BM - 1) // MOE_BM * MOE_BM
    pends = jnp.cumsum(padded)
    pstarts = pends - padded
    n_blocks = (n_tok * 2) // MOE_BM + N_EXPERTS
    blk_row0 = jnp.arange(n_blocks, dtype=I32) * MOE_BM
    block_e = jnp.minimum(jnp.sum(pends[None, :] <= blk_row0[:, None], axis=1), N_EXPERTS - 1).astype(I32)
    n_valid = jnp.clip(pstarts[block_e] + counts[block_e] - blk_row0, 0, MOE_BM).astype(I32)
    n_used = (pends[-1:] // MOE_BM).astype(I32)
    e_iota = jnp.arange(N_EXPERTS, dtype=I32)
    row0 = jnp.sum(jnp.where(idx[:, 0:2, None] == e_iota, pstarts, 0), axis=-1)
    dest = (row0 + idx[:, 2:4]).astype(I32)

    y_pad = moe_experts(h2d, dest.reshape(-1), norm_g, w1, w3, w2, layer, block_e, n_valid, n_used)
    return moe_combine(h2d, wgt, dest, y_pad, g_final)


def kernel(x, norm_mix, norm_ffn, norm_final, w_in_even, hg_lb_logits, hg_norm, gla_gk2, gla_gk_bias,
           gla_norm, w_out_even, w_qkv_odd, w_o_odd, router_group_w, router_group_b, router_expert_w,
           router_expert_b, expert_w1, expert_w3, expert_w2):
    b, s, d = x.shape
    n_tok = b * s
    depth = norm_mix.shape[0]
    lb_table = jnp.cumsum(jax.nn.softmax(hg_lb_logits.astype(F32), axis=0), axis=0)
    h = x.reshape(n_tok, d)
    for l in range(depth):
        if l % 2 == 0:
            e = l // 2
            w_in = w_in_even[e]
            w_low = jnp.pad(w_in[:, MAIN_IN:], ((0, 0), (0, LANES - GLA_RANK))).astype(BF16)
            proj, glow = norm_matmul(h, norm_mix[l], w_in[:, :MAIN_IN].astype(BF16), w_low, tm=512, tn=1024)
            proj = proj.reshape(b, s, MAIN_IN)
            gk2_pad = jnp.pad(gla_gk2[e], ((0, LANES - GLA_RANK), (0, 0)))
            o_hg = hgrn2_mix(proj, lb_table[l], hg_norm[e])
            o_gla = gla_mix(proj, glow.reshape(b, s, LANES), gk2_pad, gla_gk_bias[e], gla_norm[e])
            mixed = jnp.concatenate([o_hg, o_gla], axis=-1).reshape(n_tok, d)
            h = matmul_residual(mixed, w_out_even[e].astype(BF16), h, tm=512)
        else:
            o = l // 2
            qkv = norm_matmul(h, norm_mix[l], w_qkv_odd[o].astype(BF16), tm=512, tn=1024)
            attn = moba_attention(qkv.reshape(b, s, 3 * d))
            h = matmul_residual(attn.reshape(n_tok, d), w_o_odd[o].astype(BF16), h, tm=512)
        h = hierarchical_moe(h, l, norm_ffn[l], router_group_w[l], router_group_b[l], router_expert_w[l],
                             router_expert_b[l], expert_w1, expert_w3, expert_w2,
                             g_final=norm_final if l == depth - 1 else None)
    return h.reshape(b, s, d)
```

```python
import functools

import jax
import jax.numpy as jnp
from jax import lax
from jax.experimental import pallas as pl
from jax.experimental.pallas import tpu as pltpu

F32 = jnp.float32
BF16 = jnp.bfloat16
I32 = jnp.int32
HIGHEST = lax.Precision.HIGHEST
NEG_INF = float("-inf")
LOG2_E = 1.4426950408889634

EPS = 1e-6
D_MODEL = 2048

HG_HEADS, HG_DK, HG_DV = 8, 128, 128
GLA_HEADS, GLA_DK, GLA_DV = 4, 128, 256
GLA_RANK = 16
GLA_GATE_NORM = 16.0
HG_QK = HG_HEADS * HG_DK
MAIN_IN = 4 * HG_QK + 2 * GLA_HEADS * GLA_DK + 2 * GLA_HEADS * GLA_DV
CHUNK = 64
SUB = 16
REC_T = 256

MOBA_HEADS, MOBA_DH = 16, 128
MOBA_BLOCK = 256
MOBA_TOPK = 3
MOBA_HP = 2
MASK_BIAS = -1e30

N_GROUPS, EXPERTS_PER_GROUP = 4, 8
N_EXPERTS = N_GROUPS * EXPERTS_PER_GROUP
D_EXPERT = D_MODEL // 4
EXPERT_LANE0 = N_GROUPS
ROUTE_TM = 256
MOE_BM = 256
COMBINE_T = 128

LANES = 128
VMEM_LIMIT = 56 * 1024 * 1024

NT_DIMS = (((1,), (1,)), ((), ()))
TN_DIMS = (((0,), (0,)), ((), ()))


def _params(*sem):
    return pltpu.CompilerParams(dimension_semantics=sem, vmem_limit_bytes=VMEM_LIMIT)


def _rms(x, g):
    return x * lax.rsqrt(jnp.mean(x * x, axis=-1, keepdims=True) + EPS) * g


def _silu(x):
    return x * jax.nn.sigmoid(x)


def _norm_matmul_body(*refs, has_side):
    if has_side:
        x_ref, g_ref, w_ref, ws_ref, o_ref, os_ref, xn_ref = refs
    else:
        x_ref, g_ref, w_ref, o_ref, xn_ref = refs

    @pl.when(pl.program_id(1) == 0)
    def _():
        xn_ref[...] = _rms(x_ref[...], g_ref[...]).astype(BF16)
        if has_side:
            os_ref[...] = jnp.dot(xn_ref[...], ws_ref[...], preferred_element_type=F32)

    o_ref[...] = jnp.dot(xn_ref[...], w_ref[...], preferred_element_type=F32).astype(o_ref.dtype)


def norm_matmul(x, g, w, w_side=None, *, tm, tn):
    m, k = x.shape
    n = w.shape[1]
    has_side = w_side is not None
    in_specs = [pl.BlockSpec((tm, k), lambda i, j: (i, 0)),
                pl.BlockSpec((1, k), lambda i, j: (0, 0)),
                pl.BlockSpec((k, tn), lambda i, j: (0, j))]
    out_specs = [pl.BlockSpec((tm, tn), lambda i, j: (i, j))]
    out_shape = [jax.ShapeDtypeStruct((m, n), BF16)]
    args = [x, g.reshape(1, k), w]
    if has_side:
        ns = w_side.shape[1]
        in_specs.append(pl.BlockSpec((k, ns), lambda i, j: (0, 0)))
        out_specs.append(pl.BlockSpec((tm, ns), lambda i, j: (i, 0)))
        out_shape.append(jax.ShapeDtypeStruct((m, ns), F32))
        args.append(w_side)
    outs = pl.pallas_call(
        functools.partial(_norm_matmul_body, has_side=has_side),
        grid=(m // tm, n // tn),
        in_specs=in_specs, out_specs=out_specs, out_shape=out_shape,
        scratch_shapes=[pltpu.VMEM((tm, k), BF16)],
        compiler_params=_params("parallel", "arbitrary"),
        name="norm_matmul_side" if has_side else "norm_matmul",
    )(*args)
    return outs if has_side else outs[0]


def _matmul_res_body(a_ref, w_ref, r_ref, o_ref):
    o_ref[...] = r_ref[...] + jnp.dot(a_ref[...], w_ref[...], preferred_element_type=F32)


def matmul_residual(a, w, res, *, tm):
    m, k = a.shape
    n = w.shape[1]
    return pl.pallas_call(
        _matmul_res_body,
        grid=(m // tm,),
        in_specs=[pl.BlockSpec((tm, k), lambda i: (i, 0)),
                  pl.BlockSpec((k, n), lambda i: (0, 0)),
                  pl.BlockSpec((tm, n), lambda i: (i, 0))],
        out_specs=pl.BlockSpec((tm, n), lambda i: (i, 0)),
        out_shape=jax.ShapeDtypeStruct((m, n), F32),
        compiler_params=_params("parallel"),
        name="matmul_residual",
    )(a, w, res)


def _recurrence_chunk(q, k, v, la, st_ref):
    r_i = lax.broadcasted_iota(I32, (CHUNK, CHUNK), 0)
    c_i = lax.broadcasted_iota(I32, (CHUNK, CHUNK), 1)
    tri = (c_i <= r_i).astype(F32)
    cum = jnp.dot(tri, la, precision=HIGHEST, preferred_element_type=F32)
    last = cum[CHUNK - 1:CHUNK, :]
    qg = (q * jnp.exp(cum)).astype(BF16)
    kg = (k * jnp.exp(last - cum)).astype(BF16)
    vb = v.astype(BF16)
    st = st_ref[...]
    o_inter = lax.dot_general(qg, st.astype(BF16), NT_DIMS, preferred_element_type=F32)
    st_ref[...] = st * jnp.exp(last) + lax.dot_general(vb, kg, TN_DIMS, preferred_element_type=F32)

    rows = lax.broadcasted_iota(I32, (SUB, 1), 0)
    outs = []
    for blk in range(CHUNK // SUB):
        lo = blk * SUB
        q_b, k_b, v_b, c_b = q[lo:lo + SUB], k[lo:lo + SUB], v[lo:lo + SUB], cum[lo:lo + SUB]
        acc = o_inter[lo:lo + SUB]
        if blk > 0:
            ref_pt = cum[lo - 1:lo, :]
            q_t = (q_b * jnp.exp(c_b - ref_pt)).astype(BF16)
            k_t = (k[:lo] * jnp.exp(ref_pt - cum[:lo])).astype(BF16)
            s = lax.dot_general(q_t, k_t, NT_DIMS, preferred_element_type=F32)
            acc = acc + jnp.dot(s.astype(BF16), vb[:lo], preferred_element_type=F32)
        for j in range(SUB):
            e = jnp.exp(jnp.where(rows >= j, c_b - c_b[j:j + 1, :], NEG_INF))
            col = jnp.sum(q_b * k_b[j:j + 1, :] * e, axis=-1, keepdims=True)
            acc = acc + col * v_b[j:j + 1, :]
        outs.append(acc)
    return jnp.concatenate(outs, axis=0)


def _log_sigmoid(z):
    return jnp.minimum(z, 0.0) - jnp.log(1.0 + jnp.exp(-jnp.abs(z)))


def _hgrn2_body(hq_ref, hf_ref, hi_ref, hg_ref, lb_ref, nw_ref, o_ref, st_ref):
    @pl.when(pl.program_id(2) == 0)
    def _():
        st_ref[...] = jnp.zeros_like(st_ref)

    lb = lb_ref[...]
    nw = nw_ref[...]

    def chunk(c, carry):
        rs = pl.ds(pl.multiple_of(c * CHUNK, CHUNK), CHUNK)
        forget = lb + (1.0 - lb) * jax.nn.sigmoid(hf_ref[0, rs, :].astype(F32))
        q = _silu(hq_ref[0, rs, :].astype(F32)) * (HG_DK ** -0.5)
        o = _recurrence_chunk(q, 1.0 - forget, hi_ref[0, rs, :].astype(F32), jnp.log(forget), st_ref)
        o_ref[0, rs, :] = (_rms(o, nw) * _silu(hg_ref[0, rs, :].astype(F32))).astype(o_ref.dtype)
        return carry

    lax.fori_loop(0, REC_T // CHUNK, chunk, 0)


def _gla_body(gq_ref, gk_ref, gv_ref, gg_ref, glow_ref, gk2_ref, gb_ref, nw_ref, o_ref, st_ref):
    @pl.when(pl.program_id(2) == 0)
    def _():
        st_ref[...] = jnp.zeros_like(st_ref)

    gk2 = gk2_ref[...]
    gb = gb_ref[...]
    nw = nw_ref[...]

    def chunk(c, carry):
        rs = pl.ds(pl.multiple_of(c * CHUNK, CHUNK), CHUNK)
        z = jnp.dot(glow_ref[0, rs, :], gk2, precision=HIGHEST, preferred_element_type=F32) + gb
        la = _log_sigmoid(z) / GLA_GATE_NORM
        q = gq_ref[0, rs, :].astype(F32) * (GLA_DK ** -0.5)
        o = _recurrence_chunk(q, gk_ref[0, rs, :].astype(F32), gv_ref[0, rs, :].astype(F32), la, st_ref)
        o_ref[0, rs, :] = (_rms(o, nw) * _silu(gg_ref[0, rs, :].astype(F32))).astype(o_ref.dtype)
        return carry

    lax.fori_loop(0, REC_T // CHUNK, chunk, 0)


def hgrn2_mix(proj, lb, hg_norm):
    b, s, _ = proj.shape

    def col(base):
        return pl.BlockSpec((1, REC_T, HG_DK), lambda bi, h, t: (bi, t, base + h))

    return pl.pallas_call(
        _hgrn2_body,
        grid=(b, HG_HEADS, s // REC_T),
        in_specs=[col(0), col(HG_HEADS), col(2 * HG_HEADS), col(3 * HG_HEADS),
                  pl.BlockSpec((1, HG_DK), lambda bi, h, t: (0, h)),
                  pl.BlockSpec((1, HG_DV), lambda bi, h, t: (0, 0))],
        out_specs=pl.BlockSpec((1, REC_T, HG_DV), lambda bi, h, t: (bi, t, h)),
        out_shape=jax.ShapeDtypeStruct((b, s, HG_HEADS * HG_DV), BF16),
        scratch_shapes=[pltpu.VMEM((HG_DV, HG_DK), F32)],
        compiler_params=_params("parallel", "parallel", "arbitrary"),
        name="hgrn2_mix",
    )(proj, proj, proj, proj, lb.reshape(1, HG_QK), hg_norm.reshape(1, HG_DV))


def gla_mix(proj, glow, gk2_pad, gk_bias, gla_norm):
    b, s, _ = proj.shape
    q0 = 4 * HG_QK // GLA_DK
    k0 = q0 + GLA_HEADS
    v0 = (4 * HG_QK + 2 * GLA_HEADS * GLA_DK) // GLA_DV
    g0 = v0 + GLA_HEADS
    return pl.pallas_call(
        _gla_body,
        grid=(b, GLA_HEADS, s // REC_T),
        in_specs=[pl.BlockSpec((1, REC_T, GLA_DK), lambda bi, h, t: (bi, t, q0 + h)),
                  pl.BlockSpec((1, REC_T, GLA_DK), lambda bi, h, t: (bi, t, k0 + h)),
                  pl.BlockSpec((1, REC_T, GLA_DV), lambda bi, h, t: (bi, t, v0 + h)),
                  pl.BlockSpec((1, REC_T, GLA_DV), lambda bi, h, t: (bi, t, g0 + h)),
                  pl.BlockSpec((1, REC_T, LANES), lambda bi, h, t: (bi, t, 0)),
                  pl.BlockSpec((LANES, GLA_DK), lambda bi, h, t: (0, h)),
                  pl.BlockSpec((1, GLA_DK), lambda bi, h, t: (0, h)),
                  pl.BlockSpec((1, GLA_DV), lambda bi, h, t: (0, 0))],
        out_specs=pl.BlockSpec((1, REC_T, GLA_DV), lambda bi, h, t: (bi, t, h)),
        out_shape=jax.ShapeDtypeStruct((b, s, GLA_HEADS * GLA_DV), BF16),
        scratch_shapes=[pltpu.VMEM((GLA_DV, GLA_DK), F32)],
        compiler_params=_params("parallel", "parallel", "arbitrary"),
        name="gla_mix",
    )(proj, proj, proj, proj, glow, gk2_pad, gk_bias.reshape(1, -1), gla_norm.reshape(1, GLA_DV))


def _moba_body(q_ref, k_ref, v_ref, o_ref, kmean_ref, vt_ref, bias_ref, *, n_blocks):
    qi = pl.program_id(2)
    blk, dh = MOBA_BLOCK, MOBA_DH
    heads = [slice(hh * dh, (hh + 1) * dh) for hh in range(MOBA_HP)]

    @pl.when(qi == 0)
    def _():
        for n in range(n_blocks):
            rs = slice(n * blk, (n + 1) * blk)
            for hh, cols in enumerate(heads):
                kmean_ref[hh, n:n + 1, :] = jnp.mean(k_ref[0, rs, cols].astype(F32), axis=0, keepdims=True)
                vt_ref[hh, :, rs] = v_ref[0, rs, cols].astype(F32).T.astype(BF16)

    c = (MOBA_DH ** -0.5) * LOG2_E
    own = pl.ds(pl.multiple_of(qi * blk, blk), blk)
    brow = lax.broadcasted_iota(I32, (n_blocks, blk), 0)
    past = brow < qi
    k_i = lax.broadcasted_iota(I32, (blk, blk), 0)
    q_i = lax.broadcasted_iota(I32, (blk, blk), 1)

    qs, carry0 = [], []
    for hh, cols in enumerate(heads):
        q = q_ref[0, :, cols]
        qs.append(q)
        gate = lax.dot_general(kmean_ref[hh], q.astype(F32), NT_DIMS, precision=HIGHEST,
                               preferred_element_type=F32)
        gate = jnp.where(past, gate, NEG_INF)
        rank = jnp.zeros((n_blocks, blk), I32)
        for m in range(n_blocks):
            gm = gate[m:m + 1, :]
            rank = rank + jnp.where(gm > gate, 1, jnp.where(gm == gate, jnp.where(brow > m, 1, 0), 0))
        keep = jnp.where(past, jnp.where(rank < MOBA_TOPK, 1, 0), 0)
        bias_ref[hh] = jnp.where(keep > 0, 0.0, MASK_BIAS)
        s = lax.dot_general(k_ref[0, own, cols], q, NT_DIMS, preferred_element_type=F32)
        s = jnp.where(k_i <= q_i, s, NEG_INF)
        m0 = jnp.max(s, axis=0, keepdims=True)
        p = jnp.exp2((s - m0) * c)
        l0 = jnp.sum(p, axis=0, keepdims=True)
        acc0 = jnp.dot(vt_ref[hh, :, own], p.astype(BF16), preferred_element_type=F32)
        carry0 += [m0, l0, acc0]

    def past_pair(j, carry):
        n0 = 2 * j
        r0 = pl.ds(pl.multiple_of(n0 * blk, blk), blk)
        r1 = pl.ds(pl.multiple_of(n0 * blk + blk, blk), blk)
        r01 = pl.ds(pl.multiple_of(n0 * blk, 2 * blk), 2 * blk)
        out = []
        for hh, cols in enumerate(heads):
            m_run, l_run, acc = carry[3 * hh:3 * hh + 3]
            s0 = (lax.dot_general(k_ref[0, r0, cols], qs[hh], NT_DIMS, preferred_element_type=F32)
                  + bias_ref[hh, pl.ds(n0, 1), :])
            s1 = (lax.dot_general(k_ref[0, r1, cols], qs[hh], NT_DIMS, preferred_element_type=F32)
                  + bias_ref[hh, pl.ds(n0 + 1, 1), :])
            m_new = jnp.maximum(m_run, jnp.maximum(jnp.max(s0, axis=0, keepdims=True),
                                                   jnp.max(s1, axis=0, keepdims=True)))
            alpha = jnp.exp2((m_run - m_new) * c)
            p0 = jnp.exp2((s0 - m_new) * c)
            p1 = jnp.exp2((s1 - m_new) * c)
            l_new = alpha * l_run + jnp.sum(p0, axis=0, keepdims=True) + jnp.sum(p1, axis=0, keepdims=True)
            p01 = jnp.concatenate([p0.astype(BF16), p1.astype(BF16)], axis=0)
            acc = alpha * acc + jnp.dot(vt_ref[hh, :, r01], p01, preferred_element_type=F32)
            out += [m_new, l_new, acc]
        return tuple(out)

    fin = lax.fori_loop(0, (qi + 1) // 2, past_pair, tuple(carry0))
    for hh, cols in enumerate(heads):
        o_ref[0, :, cols] = (fin[3 * hh + 2] / fin[3 * hh + 1]).T.astype(o_ref.dtype)


def moba_attention(qkv):
    b, s, _ = qkv.shape
    n_blocks = s // MOBA_BLOCK
    hw = MOBA_HP * MOBA_DH
    hsteps = MOBA_HEADS // MOBA_HP
    return pl.pallas_call(
        functools.partial(_moba_body, n_blocks=n_blocks),
        grid=(b, hsteps, n_blocks),
        in_specs=[pl.BlockSpec((1, MOBA_BLOCK, hw), lambda bi, h, t: (bi, t, h)),
                  pl.BlockSpec((1, s, hw), lambda bi, h, t: (bi, 0, hsteps + h)),
                  pl.BlockSpec((1, s, hw), lambda bi, h, t: (bi, 0, 2 * hsteps + h))],
        out_specs=pl.BlockSpec((1, MOBA_BLOCK, hw), lambda bi, h, t: (bi, t, h)),
        out_shape=jax.ShapeDtypeStruct((b, s, D_MODEL), BF16),
        scratch_shapes=[pltpu.VMEM((MOBA_HP, n_blocks, MOBA_DH), F32),
                        pltpu.VMEM((MOBA_HP, MOBA_DH, s), BF16),
                        pltpu.VMEM((MOBA_HP, n_blocks, MOBA_BLOCK), F32)],
        compiler_params=_params("parallel", "parallel", "arbitrary"),
        name="moba_attention",
    )(qkv, qkv, qkv)


def _router_body(h_ref, g_ref, w_ref, b_ref, idx_ref, wgt_ref, cnt_ref, run_ref):
    step = pl.program_id(0)

    @pl.when(step == 0)
    def _():
        run_ref[...] = jnp.zeros_like(run_ref)

    tm = h_ref.shape[0]
    xn = _rms(h_ref[...], g_ref[...])
    logits = jnp.dot(xn, w_ref[...], precision=HIGHEST, preferred_element_type=F32) + b_ref[...]
    lane = lax.broadcasted_iota(I32, (tm, LANES), 1)

    def first_max(vals):
        top = jnp.max(vals, axis=-1, keepdims=True)
        where = jnp.min(jnp.where(vals == top, lane, LANES), axis=-1, keepdims=True)
        return top, where

    g_logits = jnp.where(lane < N_GROUPS, logits, NEG_INF)
    g_top, grp = first_max(g_logits)
    p_grp = 1.0 / jnp.sum(jnp.exp(g_logits - g_top), axis=-1, keepdims=True)
    lo = EXPERT_LANE0 + grp * EXPERTS_PER_GROUP
    e_logits = jnp.where((lane >= lo) & (lane < lo + EXPERTS_PER_GROUP), logits, NEG_INF)
    v0, j0 = first_max(e_logits)
    v1, j1 = first_max(jnp.where(lane == j0, NEG_INF, e_logits))
    t = jnp.exp(v1 - v0)
    w0 = p_grp / (1.0 + t)
    w1 = p_grp * t / (1.0 + t)

    hit0 = lane == j0
    hit1 = lane == j1
    member = jnp.where(hit0 | hit1, 1.0, 0.0)
    r_i = lax.broadcasted_iota(I32, (tm, tm), 0)
    c_i = lax.broadcasted_iota(I32, (tm, tm), 1)
    before = jnp.where(c_i < r_i, 1.0, 0.0).astype(BF16)
    prior = jnp.dot(before, member.astype(BF16), preferred_element_type=F32) + run_ref[...]
    rank0 = jnp.sum(jnp.where(hit0, prior, 0.0), axis=-1, keepdims=True).astype(I32)
    rank1 = jnp.sum(jnp.where(hit1, prior, 0.0), axis=-1, keepdims=True).astype(I32)
    run_ref[...] = run_ref[...] + jnp.sum(member, axis=0, keepdims=True)
    cnt_ref[...] = run_ref[...]

    idx_ref[...] = jnp.where(lane == 0, j0 - EXPERT_LANE0,
                             jnp.where(lane == 1, j1 - EXPERT_LANE0,
                                       jnp.where(lane == 2, rank0, jnp.where(lane == 3, rank1, 0))))
    wgt_ref[...] = jnp.where(lane == 0, w0, jnp.where(lane == 1, w1, 0.0))


def moe_router(h2d, g, w_cat, b_cat):
    n_tok, d = h2d.shape
    return pl.pallas_call(
        _router_body,
        grid=(n_tok // ROUTE_TM,),
        in_specs=[pl.BlockSpec((ROUTE_TM, d), lambda i: (i, 0)),
                  pl.BlockSpec((1, d), lambda i: (0, 0)),
                  pl.BlockSpec((d, LANES), lambda i: (0, 0)),
                  pl.BlockSpec((1, LANES), lambda i: (0, 0))],
        out_specs=[pl.BlockSpec((ROUTE_TM, LANES), lambda i: (i, 0)),
                   pl.BlockSpec((ROUTE_TM, LANES), lambda i: (i, 0)),
                   pl.BlockSpec((1, LANES), lambda i: (0, 0))],
        out_shape=[jax.ShapeDtypeStruct((n_tok, LANES), I32),
                   jax.ShapeDtypeStruct((n_tok, LANES), F32),
                   jax.ShapeDtypeStruct((1, LANES), F32)],
        scratch_shapes=[pltpu.VMEM((1, LANES), F32)],
        compiler_params=_params("arbitrary"),
        name="moe_router",
    )(h2d, g.reshape(1, d), w_cat, b_cat)


INV_UNROLL = 16
DUMP_ROWS = MOE_BM


def _expert_body(be_ref, nu_ref, dest_ref, fill_hbm, h_hbm, g_ref, w1_ref, w3_ref, w2_ref, o2_hbm,
                 inv_ref, xbuf, ybuf, gsem, ssem, isem, w1_bf, w3_bf, w2_bf, *, n_tok, n_blocks):
    i = pl.program_id(0)
    n_used = nu_ref[0]
    slot = i % 2
    dummy = 2 * n_tok

    def gather_row(block, r, to_slot):
        a = inv_ref[block * MOE_BM + r]
        tok = jnp.minimum(a >> 1, n_tok - 1)
        return pltpu.make_async_copy(h_hbm.at[pl.ds(tok, 1)], xbuf.at[to_slot, pl.ds(r, 1)], gsem.at[to_slot])

    def scatter_row(block, r, from_slot):
        a = inv_ref[block * MOE_BM + r]
        row = jnp.where(a == dummy, n_tok + r, a >> 1)
        return pltpu.make_async_copy(ybuf.at[from_slot, pl.ds(r, 1)], o2_hbm.at[a & 1, pl.ds(row, 1)],
                                     ssem.at[from_slot])

    def gather_wait(of_slot):
        pltpu.make_async_copy(h_hbm.at[pl.ds(0, MOE_BM)], xbuf.at[of_slot], gsem.at[of_slot]).wait()

    def scatter_wait(of_slot):
        pltpu.make_async_copy(ybuf.at[of_slot], o2_hbm.at[0, pl.ds(0, MOE_BM)], ssem.at[of_slot]).wait()

    @pl.when(i == 0)
    def _():
        fill = pltpu.make_async_copy(fill_hbm, inv_ref, isem)
        fill.start()
        fill.wait()

        def body(c, carry):
            base = c * INV_UNROLL
            rows = [dest_ref[base + u] for u in range(INV_UNROLL)]
            for u in range(INV_UNROLL):
                inv_ref[rows[u]] = base + u
            return carry
        lax.fori_loop(0, dest_ref.shape[0] // INV_UNROLL, body, 0)
        ybuf[...] = jnp.zeros_like(ybuf)
        for plane in range(2):
            init = pltpu.make_async_copy(ybuf.at[plane], o2_hbm.at[plane, pl.ds(n_tok, MOE_BM)], isem)
            init.start()
            init.wait()
        for r in range(MOE_BM):
            gather_row(0, r, 0).start()

    @pl.when(i < n_used)
    def _():
        gather_wait(slot)

        @pl.when((i == 0) | (be_ref[i] != be_ref[jnp.maximum(i - 1, 0)]))
        def _():
            w1_bf[...] = w1_ref[...].astype(BF16)
            w3_bf[...] = w3_ref[...].astype(BF16)
            w2_bf[...] = w2_ref[...].astype(BF16)

        @pl.when(i >= 1)
        def _():
            scatter_wait(slot)

        xn = _rms(xbuf[slot], g_ref[...]).astype(BF16)
        prev = jnp.where(i == 0, n_blocks - 1, i - 1)
        for r in range(MOE_BM):
            gather_row(i + 1, r, 1 - slot).start()
            scatter_row(prev, r, 1 - slot).start()
        h1 = jnp.dot(xn, w1_bf[...], preferred_element_type=F32)
        h3 = jnp.dot(xn, w3_bf[...], preferred_element_type=F32)
        act = (_silu(h1) * h3).astype(BF16)
        ybuf[slot] = jnp.dot(act, w2_bf[...], preferred_element_type=F32)

    @pl.when(i == n_used)
    def _():
        for r in range(MOE_BM):
            scatter_row(i - 1, r, 1 - slot).start()
        gather_wait(slot)
        scatter_wait(slot)
        scatter_wait(1 - slot)


def moe_experts(h2d, dest_flat, g, w1, w3, w2, layer, block_e, n_used):
    n_tok, d = h2d.shape
    n_blocks = block_e.shape[0]
    n_rows = n_blocks * MOE_BM
    f = w1.shape[-1]

    def w_map(i, be, nu, dest):
        return (layer, be[i], 0, 0)

    grid_spec = pltpu.PrefetchScalarGridSpec(
        num_scalar_prefetch=3,
        grid=(n_blocks,),
        in_specs=[pl.BlockSpec(memory_space=pl.ANY),
                  pl.BlockSpec(memory_space=pl.ANY),
                  pl.BlockSpec((1, d), lambda i, be, nu, dest: (0, 0)),
                  pl.BlockSpec((None, None, d, f), w_map),
                  pl.BlockSpec((None, None, d, f), w_map),
                  pl.BlockSpec((None, None, f, d), w_map)],
        out_specs=pl.BlockSpec(memory_space=pl.ANY),
        scratch_shapes=[pltpu.SMEM((n_rows,), I32),
                        pltpu.VMEM((2, MOE_BM, d), F32),
                        pltpu.VMEM((2, MOE_BM, d), F32),
                        pltpu.SemaphoreType.DMA((2,)),
                        pltpu.SemaphoreType.DMA((2,)),
                        pltpu.SemaphoreType.DMA(()),
                        pltpu.VMEM((d, f), BF16), pltpu.VMEM((d, f), BF16), pltpu.VMEM((f, d), BF16)],
    )
    return pl.pallas_call(
        functools.partial(_expert_body, n_tok=n_tok, n_blocks=n_blocks),
        grid_spec=grid_spec,
        out_shape=jax.ShapeDtypeStruct((2, n_tok + DUMP_ROWS, d), F32),
        compiler_params=_params("arbitrary"),
        name="moe_experts",
    )(block_e, n_used, dest_flat, jnp.full((n_rows,), 2 * n_tok, I32), h2d, g.reshape(1, d), w1, w3, w2)


def _combine_body(*refs, final_norm):
    if final_norm:
        h_ref, wgt_ref, o2_ref, gf_ref, o_ref = refs
    else:
        h_ref, wgt_ref, o2_ref, o_ref = refs
    wgt = wgt_ref[...]
    out = h_ref[...] + wgt[:, 0:1] * o2_ref[0] + wgt[:, 1:2] * o2_ref[1]
    if final_norm:
        out = _rms(out, gf_ref[...])
    o_ref[...] = out


def moe_combine(h2d, wgt, o2, g_final=None):
    n_tok, d = h2d.shape
    final_norm = g_final is not None
    in_specs = [pl.BlockSpec((COMBINE_T, d), lambda i: (i, 0)),
                pl.BlockSpec((COMBINE_T, LANES), lambda i: (i, 0)),
                pl.BlockSpec((2, COMBINE_T, d), lambda i: (0, i, 0))]
    args = [h2d, wgt, o2]
    if final_norm:
        in_specs.append(pl.BlockSpec((1, d), lambda i: (0, 0)))
        args.append(g_final.reshape(1, d))
    return pl.pallas_call(
        functools.partial(_combine_body, final_norm=final_norm),
        grid=(n_tok // COMBINE_T,),
        in_specs=in_specs,
        out_specs=pl.BlockSpec((COMBINE_T, d), lambda i: (i, 0)),
        out_shape=jax.ShapeDtypeStruct((n_tok, d), F32),
        compiler_params=_params("parallel"),
        name="moe_combine_final" if final_norm else "moe_combine",
    )(*args)


def hierarchical_moe(h2d, layer, norm_g, w_group, b_group, w_router, b_router, w1, w3, w2, g_final=None):
    n_tok, d = h2d.shape
    pad_l = LANES - N_GROUPS - N_EXPERTS
    w_cat = jnp.concatenate([w_group, w_router, jnp.zeros((d, pad_l), F32)], axis=1)
    b_cat = jnp.concatenate([b_group, b_router, jnp.zeros((pad_l,), F32)]).reshape(1, LANES)
    idx, wgt, cnt = moe_router(h2d, norm_g, w_cat, b_cat)

    counts = cnt[0, EXPERT_LANE0:EXPERT_LANE0 + N_EXPERTS].astype(I32)
    padded = (counts + MOE_BM - 1) // MOE_BM * MOE_BM
    pends = jnp.cumsum(padded)
    pstarts = pends - padded
    n_blocks = (n_tok * 2) // MOE_BM + N_EXPERTS
    blk_row0 = jnp.arange(n_blocks, dtype=I32) * MOE_BM
    block_e = jnp.minimum(jnp.sum(pends[None, :] <= blk_row0[:, None], axis=1), N_EXPERTS - 1).astype(I32)
    n_used = (pends[-1:] // MOE_BM).astype(I32)
    e_iota = jnp.arange(N_EXPERTS, dtype=I32)
    row0 = jnp.sum(jnp.where(idx[:, 0:2, None] == e_iota, pstarts, 0), axis=-1)
    dest = (row0 + idx[:, 2:4]).astype(I32).reshape(-1)

    o2 = moe_experts(h2d, dest, norm_g, w1, w3, w2, layer, block_e, n_used)
    return moe_combine(h2d, wgt, o2, g_final)


def kernel(x, norm_mix, norm_ffn, norm_final, w_in_even, hg_lb_logits, hg_norm, gla_gk2, gla_gk_bias,
           gla_norm, w_out_even, w_qkv_odd, w_o_odd, router_group_w, router_group_b, router_expert_w,
           router_expert_b, expert_w1, expert_w3, expert_w2):
    b, s, d = x.shape
    n_tok = b * s
    depth = norm_mix.shape[0]
    lb_table = jnp.cumsum(jax.nn.softmax(hg_lb_logits.astype(F32), axis=0), axis=0)
    h = x.reshape(n_tok, d)
    for l in range(depth):
        if l % 2 == 0:
            e = l // 2
            w_in = w_in_even[e]
            w_low = jnp.pad(w_in[:, MAIN_IN:], ((0, 0), (0, LANES - GLA_RANK))).astype(BF16)
            proj, glow = norm_matmul(h, norm_mix[l], w_in[:, :MAIN_IN].astype(BF16), w_low, tm=512, tn=1024)
            proj = proj.reshape(b, s, MAIN_IN)
            gk2_pad = jnp.pad(gla_gk2[e], ((0, LANES - GLA_RANK), (0, 0)))
            o_hg = hgrn2_mix(proj, lb_table[l], hg_norm[e])
            o_gla = gla_mix(proj, glow.reshape(b, s, LANES), gk2_pad, gla_gk_bias[e], gla_norm[e])
            mixed = jnp.concatenate([o_hg, o_gla], axis=-1).reshape(n_tok, d)
            h = matmul_residual(mixed, w_out_even[e].astype(BF16), h, tm=512)
        else:
            o = l // 2
            qkv = norm_matmul(h, norm_mix[l], w_qkv_odd[o].astype(BF16), tm=512, tn=1024)
            attn = moba_attention(qkv.reshape(b, s, 3 * d))
            h = matmul_residual(attn.reshape(n_tok, d), w_o_odd[o].astype(BF16), h, tm=512)
        h = hierarchical_moe(h, l, norm_ffn[l], router_group_w[l], router_group_b[l], router_expert_w[l],
                             router_expert_b[l], expert_w1, expert_w3, expert_w2,
                             g_final=norm_final if l == depth - 1 else None)
    return h.reshape(b, s, d)
```

```python
import functools

import jax
import jax.numpy as jnp
from jax import lax
from jax.experimental import pallas as pl
from jax.experimental.pallas import tpu as pltpu

F32 = jnp.float32
BF16 = jnp.bfloat16
I32 = jnp.int32
HIGHEST = lax.Precision.HIGHEST
NEG_INF = float("-inf")
LOG2_E = 1.4426950408889634

EPS = 1e-6
D_MODEL = 2048

HG_HEADS, HG_DK, HG_DV = 8, 128, 128
GLA_HEADS, GLA_DK, GLA_DV = 4, 128, 256
GLA_RANK = 16
GLA_GATE_NORM = 16.0
HG_QK = HG_HEADS * HG_DK
MAIN_IN = 4 * HG_QK + 2 * GLA_HEADS * GLA_DK + 2 * GLA_HEADS * GLA_DV
CHUNK = 64
SUB = 8
REC_T = 256

MOBA_HEADS, MOBA_DH = 16, 128
MOBA_BLOCK = 256
MOBA_TOPK = 3
MOBA_HP = 2
MASK_BIAS = -1e30

N_GROUPS, EXPERTS_PER_GROUP = 4, 8
N_EXPERTS = N_GROUPS * EXPERTS_PER_GROUP
D_EXPERT = D_MODEL // 4
EXPERT_LANE0 = N_GROUPS
ROUTE_TM = 256
MOE_BM = 256
COMBINE_T = 128

LANES = 128
VMEM_LIMIT = 56 * 1024 * 1024

NT_DIMS = (((1,), (1,)), ((), ()))
TN_DIMS = (((0,), (0,)), ((), ()))


def _params(*sem):
    return pltpu.CompilerParams(dimension_semantics=sem, vmem_limit_bytes=VMEM_LIMIT)


def _rms(x, g):
    return x * lax.rsqrt(jnp.mean(x * x, axis=-1, keepdims=True) + EPS) * g


def _silu(x):
    return x * jax.nn.sigmoid(x)


def _norm_matmul_body(*refs, has_side):
    if has_side:
        x_ref, g_ref, w_ref, ws_ref, o_ref, os_ref, xn_ref = refs
    else:
        x_ref, g_ref, w_ref, o_ref, xn_ref = refs

    @pl.when(pl.program_id(1) == 0)
    def _():
        xn_ref[...] = _rms(x_ref[...], g_ref[...]).astype(BF16)
        if has_side:
            os_ref[...] = jnp.dot(xn_ref[...], ws_ref[...], preferred_element_type=F32)

    o_ref[...] = jnp.dot(xn_ref[...], w_ref[...], preferred_element_type=F32).astype(o_ref.dtype)


def norm_matmul(x, g, w, w_side=None, *, tm, tn):
    m, k = x.shape
    n = w.shape[1]
    has_side = w_side is not None
    in_specs = [pl.BlockSpec((tm, k), lambda i, j: (i, 0)),
                pl.BlockSpec((1, k), lambda i, j: (0, 0)),
                pl.BlockSpec((k, tn), lambda i, j: (0, j))]
    out_specs = [pl.BlockSpec((tm, tn), lambda i, j: (i, j))]
    out_shape = [jax.ShapeDtypeStruct((m, n), BF16)]
    args = [x, g.reshape(1, k), w]
    if has_side:
        ns = w_side.shape[1]
        in_specs.append(pl.BlockSpec((k, ns), lambda i, j: (0, 0)))
        out_specs.append(pl.BlockSpec((tm, ns), lambda i, j: (i, 0)))
        out_shape.append(jax.ShapeDtypeStruct((m, ns), F32))
        args.append(w_side)
    outs = pl.pallas_call(
        functools.partial(_norm_matmul_body, has_side=has_side),
        grid=(m // tm, n // tn),
        in_specs=in_specs, out_specs=out_specs, out_shape=out_shape,
        scratch_shapes=[pltpu.VMEM((tm, k), BF16)],
        compiler_params=_params("parallel", "arbitrary"),
        name="norm_matmul_side" if has_side else "norm_matmul",
    )(*args)
    return outs if has_side else outs[0]


def _matmul_res_body(a_ref, w_ref, r_ref, o_ref):
    o_ref[...] = r_ref[...] + jnp.dot(a_ref[...], w_ref[...], preferred_element_type=F32)


def matmul_residual(a, w, res, *, tm):
    m, k = a.shape
    n = w.shape[1]
    return pl.pallas_call(
        _matmul_res_body,
        grid=(m // tm,),
        in_specs=[pl.BlockSpec((tm, k), lambda i: (i, 0)),
                  pl.BlockSpec((k, n), lambda i: (0, 0)),
                  pl.BlockSpec((tm, n), lambda i: (i, 0))],
        out_specs=pl.BlockSpec((tm, n), lambda i: (i, 0)),
        out_shape=jax.ShapeDtypeStruct((m, n), F32),
        compiler_params=_params("parallel"),
        name="matmul_residual",
    )(a, w, res)


def _recurrence_step(q, k, v, la, st_ref):
    n_ch = REC_T // CHUNK
    n_sub = CHUNK // SUB
    dv = v.shape[1]
    r_i = lax.broadcasted_iota(I32, (CHUNK, CHUNK), 0)
    c_i = lax.broadcasted_iota(I32, (CHUNK, CHUNK), 1)
    tri = (c_i <= r_i).astype(F32)
    local = [jnp.dot(tri, la[c * CHUNK:(c + 1) * CHUNK], precision=HIGHEST, preferred_element_type=F32)
             for c in range(n_ch)]
    cums = [local[0]]
    for c in range(1, n_ch):
        cums.append(local[c] + cums[-1][CHUNK - 1:CHUNK, :])
    cum = jnp.concatenate(cums, axis=0)
    last = cum[REC_T - 1:REC_T, :]
    vb = v.astype(BF16)
    st = st_ref[...]

    o_inter = lax.dot_general((q * jnp.exp(cum)).astype(BF16), st.astype(BF16), NT_DIMS,
                              preferred_element_type=F32)
    kv = lax.dot_general(vb, (k * jnp.exp(last - cum)).astype(BF16), TN_DIMS, preferred_element_type=F32)
    rows = lax.broadcasted_iota(I32, (SUB, 1), 0)
    ones = jnp.ones((q.shape[1], LANES), BF16)
    s_cross, s_sub, diag = {}, {}, {}
    for c in range(n_ch):
        c0 = c * CHUNK
        if c > 0:
            ref_pt = cum[c0 - 1:c0, :]
            q_t = (q[c0:c0 + CHUNK] * jnp.exp(cum[c0:c0 + CHUNK] - ref_pt)).astype(BF16)
            k_t = (k[:c0] * jnp.exp(ref_pt - cum[:c0])).astype(BF16)
            s_cross[c] = lax.dot_general(q_t, k_t, NT_DIMS, preferred_element_type=F32)
        for b in range(n_sub):
            lo = c0 + b * SUB
            q_b, k_b, c_b = q[lo:lo + SUB], k[lo:lo + SUB], cum[lo:lo + SUB]
            if b > 0:
                ref_pt = cum[lo - 1:lo, :]
                q_t = (q_b * jnp.exp(c_b - ref_pt)).astype(BF16)
                k_t = (k[c0:lo] * jnp.exp(ref_pt - cum[c0:lo])).astype(BF16)
                s_sub[c, b] = lax.dot_general(q_t, k_t, NT_DIMS, preferred_element_type=F32)
            terms = []
            for j in range(SUB):
                e = jnp.exp(jnp.where(rows >= j, c_b - c_b[j:j + 1, :], NEG_INF))
                terms.append(q_b * k_b[j:j + 1, :] * e)
            diag[c, b] = jnp.dot(jnp.concatenate(terms, axis=0).astype(BF16), ones, preferred_element_type=F32)

    cross = {c: jnp.dot(s_cross[c].astype(BF16), vb[:c * CHUNK], preferred_element_type=F32) for c in s_cross}
    sub = {cb: jnp.dot(s_sub[cb].astype(BF16), v[cb[0] * CHUNK:cb[0] * CHUNK + cb[1] * SUB].astype(BF16),
                       preferred_element_type=F32) for cb in s_sub}
    st_ref[...] = st * jnp.exp(last) + kv

    outs = []
    for c in range(n_ch):
        for b in range(n_sub):
            lo = c * CHUNK + b * SUB
            acc = o_inter[lo:lo + SUB]
            if c > 0:
                acc = acc + cross[c][b * SUB:(b + 1) * SUB]
            if b > 0:
                acc = acc + sub[c, b]
            v_b = v[lo:lo + SUB]
            for j in range(SUB):
                col = diag[c, b][j * SUB:(j + 1) * SUB]
                if dv > LANES:
                    col = jnp.concatenate([col] * (dv // LANES), axis=1)
                acc = acc + col * v_b[j:j + 1, :]
            outs.append(acc)
    return jnp.concatenate(outs, axis=0)


def _log_sigmoid(z):
    return jnp.minimum(z, 0.0) - jnp.log(1.0 + jnp.exp(-jnp.abs(z)))


def _hgrn2_body(hq_ref, hf_ref, hi_ref, hg_ref, lb_ref, nw_ref, o_ref, st_ref):
    @pl.when(pl.program_id(2) == 0)
    def _():
        st_ref[...] = jnp.zeros_like(st_ref)

    lb = lb_ref[...]
    forget = lb + (1.0 - lb) * jax.nn.sigmoid(hf_ref[0].astype(F32))
    q = _silu(hq_ref[0].astype(F32)) * (HG_DK ** -0.5)
    o = _recurrence_step(q, 1.0 - forget, hi_ref[0].astype(F32), jnp.log(forget), st_ref)
    o_ref[0] = (_rms(o, nw_ref[...]) * _silu(hg_ref[0].astype(F32))).astype(o_ref.dtype)


def _gla_body(gq_ref, gk_ref, gv_ref, gg_ref, glow_ref, gk2_ref, gb_ref, nw_ref, o_ref, st_ref):
    @pl.when(pl.program_id(2) == 0)
    def _():
        st_ref[...] = jnp.zeros_like(st_ref)

    z = jnp.dot(glow_ref[0], gk2_ref[...], precision=HIGHEST, preferred_element_type=F32) + gb_ref[...]
    la = _log_sigmoid(z) / GLA_GATE_NORM
    q = gq_ref[0].astype(F32) * (GLA_DK ** -0.5)
    o = _recurrence_step(q, gk_ref[0].astype(F32), gv_ref[0].astype(F32), la, st_ref)
    o_ref[0] = (_rms(o, nw_ref[...]) * _silu(gg_ref[0].astype(F32))).astype(o_ref.dtype)


def hgrn2_mix(proj, lb, hg_norm):
    b, s, _ = proj.shape

    def col(base):
        return pl.BlockSpec((1, REC_T, HG_DK), lambda bi, h, t: (bi, t, base + h))

    return pl.pallas_call(
        _hgrn2_body,
        grid=(b, HG_HEADS, s // REC_T),
        in_specs=[col(0), col(HG_HEADS), col(2 * HG_HEADS), col(3 * HG_HEADS),
                  pl.BlockSpec((1, HG_DK), lambda bi, h, t: (0, h)),
                  pl.BlockSpec((1, HG_DV), lambda bi, h, t: (0, 0))],
        out_specs=pl.BlockSpec((1, REC_T, HG_DV), lambda bi, h, t: (bi, t, h)),
        out_shape=jax.ShapeDtypeStruct((b, s, HG_HEADS * HG_DV), BF16),
        scratch_shapes=[pltpu.VMEM((HG_DV, HG_DK), F32)],
        compiler_params=_params("parallel", "parallel", "arbitrary"),
        name="hgrn2_mix",
    )(proj, proj, proj, proj, lb.reshape(1, HG_QK), hg_norm.reshape(1, HG_DV))


def gla_mix(proj, glow, gk2_pad, gk_bias, gla_norm):
    b, s, _ = proj.shape
    q0 = 4 * HG_QK // GLA_DK
    k0 = q0 + GLA_HEADS
    v0 = (4 * HG_QK + 2 * GLA_HEADS * GLA_DK) // GLA_DV
    g0 = v0 + GLA_HEADS
    return pl.pallas_call(
        _gla_body,
        grid=(b, GLA_HEADS, s // REC_T),
        in_specs=[pl.BlockSpec((1, REC_T, GLA_DK), lambda bi, h, t: (bi, t, q0 + h)),
                  pl.BlockSpec((1, REC_T, GLA_DK), lambda bi, h, t: (bi, t, k0 + h)),
                  pl.BlockSpec((1, REC_T, GLA_DV), lambda bi, h, t: (bi, t, v0 + h)),
                  pl.BlockSpec((1, REC_T, GLA_DV), lambda bi, h, t: (bi, t, g0 + h)),
                  pl.BlockSpec((1, REC_T, LANES), lambda bi, h, t: (bi, t, 0)),
                  pl.BlockSpec((LANES, GLA_DK), lambda bi, h, t: (0, h)),
                  pl.BlockSpec((1, GLA_DK), lambda bi, h, t: (0, h)),
                  pl.BlockSpec((1, GLA_DV), lambda bi, h, t: (0, 0))],
        out_specs=pl.BlockSpec((1, REC_T, GLA_DV), lambda bi, h, t: (bi, t, h)),
        out_shape=jax.ShapeDtypeStruct((b, s, GLA_HEADS * GLA_DV), BF16),
        scratch_shapes=[pltpu.VMEM((GLA_DV, GLA_DK), F32)],
        compiler_params=_params("parallel", "parallel", "arbitrary"),
        name="gla_mix",
    )(proj, proj, proj, proj, glow, gk2_pad, gk_bias.reshape(1, -1), gla_norm.reshape(1, GLA_DV))


def _moba_body(q_ref, k_ref, v_ref, o_ref, kmean_ref, vt_ref, bias_ref, sa_ref, sb_ref, *, n_blocks):
    qi = pl.program_id(2)
    blk, dh = MOBA_BLOCK, MOBA_DH
    heads = [slice(hh * dh, (hh + 1) * dh) for hh in range(MOBA_HP)]

    @pl.when(qi == 0)
    def _():
        for n in range(n_blocks):
            rs = slice(n * blk, (n + 1) * blk)
            for hh, cols in enumerate(heads):
                kmean_ref[hh, n:n + 1, :] = jnp.mean(k_ref[0, rs, cols].astype(F32), axis=0, keepdims=True)
                vt_ref[hh, :, rs] = v_ref[0, rs, cols].astype(F32).T.astype(BF16)

    c = (MOBA_DH ** -0.5) * LOG2_E
    own = pl.ds(pl.multiple_of(qi * blk, blk), blk)
    brow = lax.broadcasted_iota(I32, (n_blocks, blk), 0)
    past = brow < qi
    k_i = lax.broadcasted_iota(I32, (blk, blk), 0)
    q_i = lax.broadcasted_iota(I32, (blk, blk), 1)

    qs, carry0 = [], []
    for hh, cols in enumerate(heads):
        q = q_ref[0, :, cols]
        qs.append(q)
        gate = lax.dot_general(kmean_ref[hh], q.astype(F32), NT_DIMS, precision=HIGHEST,
                               preferred_element_type=F32)
        gate = jnp.where(past, gate, NEG_INF)
        rank = jnp.zeros((n_blocks, blk), I32)
        for m in range(n_blocks):
            gm = gate[m:m + 1, :]
            rank = rank + jnp.where(gm > gate, 1, jnp.where(gm == gate, jnp.where(brow > m, 1, 0), 0))
        keep = jnp.where(past, jnp.where(rank < MOBA_TOPK, 1, 0), 0)
        bias_ref[hh] = jnp.where(keep > 0, 0.0, MASK_BIAS)
        s = lax.dot_general(k_ref[0, own, cols], q, NT_DIMS, preferred_element_type=F32)
        s = jnp.where(k_i <= q_i, s, NEG_INF)
        m0 = jnp.max(s, axis=0, keepdims=True)
        p = jnp.exp2((s - m0) * c)
        l0 = jnp.sum(p, axis=0, keepdims=True)
        acc0 = jnp.dot(vt_ref[hh, :, own], p.astype(BF16), preferred_element_type=F32)
        carry0 += [m0, l0, acc0]

    n_pairs = (qi + 1) // 2

    def score_pair(pair, dst_ref):
        p = jnp.minimum(pair, n_blocks // 2 - 1)
        r01 = pl.ds(pl.multiple_of(p * 2 * blk, 2 * blk), 2 * blk)
        for hh, cols in enumerate(heads):
            dst_ref[hh] = lax.dot_general(k_ref[0, r01, cols], qs[hh], NT_DIMS, preferred_element_type=F32)

    def absorb_pair(pair, src_ref, carry):
        n0 = 2 * pair
        r01 = pl.ds(pl.multiple_of(n0 * blk, 2 * blk), 2 * blk)
        out = []
        for hh, cols in enumerate(heads):
            m_run, l_run, acc = carry[3 * hh:3 * hh + 3]
            s0 = src_ref[hh, 0:blk, :] + bias_ref[hh, pl.ds(n0, 1), :]
            s1 = src_ref[hh, blk:2 * blk, :] + bias_ref[hh, pl.ds(n0 + 1, 1), :]
            m_new = jnp.maximum(m_run, jnp.maximum(jnp.max(s0, axis=0, keepdims=True),
                                                   jnp.max(s1, axis=0, keepdims=True)))
            alpha = jnp.exp2((m_run - m_new) * c)
            p0 = jnp.exp2((s0 - m_new) * c)
            p1 = jnp.exp2((s1 - m_new) * c)
            l_new = alpha * l_run + jnp.sum(p0, axis=0, keepdims=True) + jnp.sum(p1, axis=0, keepdims=True)
            p01 = jnp.concatenate([p0.astype(BF16), p1.astype(BF16)], axis=0)
            acc = alpha * acc + jnp.dot(vt_ref[hh, :, r01], p01, preferred_element_type=F32)
            out += [m_new, l_new, acc]
        return tuple(out)

    def two_pairs(t, carry):
        score_pair(2 * t + 1, sb_ref)
        carry = absorb_pair(2 * t, sa_ref, carry)

        def second(carry):
            score_pair(2 * t + 2, sa_ref)
            return absorb_pair(2 * t + 1, sb_ref, carry)

        return lax.cond(2 * t + 1 < n_pairs, second, lambda carry: carry, carry)

    score_pair(0, sa_ref)
    fin = lax.fori_loop(0, (n_pairs + 1) // 2, two_pairs, tuple(carry0))
    for hh, cols in enumerate(heads):
        o_ref[0, :, cols] = (fin[3 * hh + 2] / fin[3 * hh + 1]).T.astype(o_ref.dtype)


def moba_attention(qkv):
    b, s, _ = qkv.shape
    n_blocks = s // MOBA_BLOCK
    hw = MOBA_HP * MOBA_DH
    hsteps = MOBA_HEADS // MOBA_HP
    return pl.pallas_call(
        functools.partial(_moba_body, n_blocks=n_blocks),
        grid=(b, hsteps, n_blocks),
        in_specs=[pl.BlockSpec((1, MOBA_BLOCK, hw), lambda bi, h, t: (bi, t, h)),
                  pl.BlockSpec((1, s, hw), lambda bi, h, t: (bi, 0, hsteps + h)),
                  pl.BlockSpec((1, s, hw), lambda bi, h, t: (bi, 0, 2 * hsteps + h))],
        out_specs=pl.BlockSpec((1, MOBA_BLOCK, hw), lambda bi, h, t: (bi, t, h)),
        out_shape=jax.ShapeDtypeStruct((b, s, D_MODEL), BF16),
        scratch_shapes=[pltpu.VMEM((MOBA_HP, n_blocks, MOBA_DH), F32),
                        pltpu.VMEM((MOBA_HP, MOBA_DH, s), BF16),
                        pltpu.VMEM((MOBA_HP, n_blocks, MOBA_BLOCK), F32),
                        pltpu.VMEM((MOBA_HP, 2 * MOBA_BLOCK, MOBA_BLOCK), F32),
                        pltpu.VMEM((MOBA_HP, 2 * MOBA_BLOCK, MOBA_BLOCK), F32)],
        compiler_params=_params("parallel", "parallel", "arbitrary"),
        name="moba_attention",
    )(qkv, qkv, qkv)


def _router_body(h_ref, g_ref, w_ref, b_ref, idx_ref, wgt_ref, cnt_ref, run_ref):
    step = pl.program_id(0)

    @pl.when(step == 0)
    def _():
        run_ref[...] = jnp.zeros_like(run_ref)

    tm = h_ref.shape[0]
    xn = _rms(h_ref[...], g_ref[...])
    logits = jnp.dot(xn, w_ref[...], precision=HIGHEST, preferred_element_type=F32) + b_ref[...]
    lane = lax.broadcasted_iota(I32, (tm, LANES), 1)

    def first_max(vals):
        top = jnp.max(vals, axis=-1, keepdims=True)
        where = jnp.min(jnp.where(vals == top, lane, LANES), axis=-1, keepdims=True)
        return top, where

    g_logits = jnp.where(lane < N_GROUPS, logits, NEG_INF)
    g_top, grp = first_max(g_logits)
    p_grp = 1.0 / jnp.sum(jnp.exp(g_logits - g_top), axis=-1, keepdims=True)
    lo = EXPERT_LANE0 + grp * EXPERTS_PER_GROUP
    e_logits = jnp.where((lane >= lo) & (lane < lo + EXPERTS_PER_GROUP), logits, NEG_INF)
    v0, j0 = first_max(e_logits)
    v1, j1 = first_max(jnp.where(lane == j0, NEG_INF, e_logits))
    t = jnp.exp(v1 - v0)
    w0 = p_grp / (1.0 + t)
    w1 = p_grp * t / (1.0 + t)

    hit0 = lane == j0
    hit1 = lane == j1
    member = jnp.where(hit0 | hit1, 1.0, 0.0)
    r_i = lax.broadcasted_iota(I32, (tm, tm), 0)
    c_i = lax.broadcasted_iota(I32, (tm, tm), 1)
    before = jnp.where(c_i < r_i, 1.0, 0.0).astype(BF16)
    prior = jnp.dot(before, member.astype(BF16), preferred_element_type=F32) + run_ref[...]
    rank0 = jnp.sum(jnp.where(hit0, prior, 0.0), axis=-1, keepdims=True).astype(I32)
    rank1 = jnp.sum(jnp.where(hit1, prior, 0.0), axis=-1, keepdims=True).astype(I32)
    run_ref[...] = run_ref[...] + jnp.sum(member, axis=0, keepdims=True)
    cnt_ref[...] = run_ref[...]

    idx_ref[...] = jnp.where(lane == 0, j0 - EXPERT_LANE0,
                             jnp.where(lane == 1, j1 - EXPERT_LANE0,
                                       jnp.where(lane == 2, rank0, jnp.where(lane == 3, rank1, 0))))
    wgt_ref[...] = jnp.where(lane == 0, w0, jnp.where(lane == 1, w1, 0.0))


def moe_router(h2d, g, w_cat, b_cat):
    n_tok, d = h2d.shape
    return pl.pallas_call(
        _router_body,
        grid=(n_tok // ROUTE_TM,),
        in_specs=[pl.BlockSpec((ROUTE_TM, d), lambda i: (i, 0)),
                  pl.BlockSpec((1, d), lambda i: (0, 0)),
                  pl.BlockSpec((d, LANES), lambda i: (0, 0)),
                  pl.BlockSpec((1, LANES), lambda i: (0, 0))],
        out_specs=[pl.BlockSpec((ROUTE_TM, LANES), lambda i: (i, 0)),
                   pl.BlockSpec((ROUTE_TM, LANES), lambda i: (i, 0)),
                   pl.BlockSpec((1, LANES), lambda i: (0, 0))],
        out_shape=[jax.ShapeDtypeStruct((n_tok, LANES), I32),
                   jax.ShapeDtypeStruct((n_tok, LANES), F32),
                   jax.ShapeDtypeStruct((1, LANES), F32)],
        scratch_shapes=[pltpu.VMEM((1, LANES), F32)],
        compiler_params=_params("arbitrary"),
        name="moe_router",
    )(h2d, g.reshape(1, d), w_cat, b_cat)


INV_UNROLL = 16
DUMP_ROWS = MOE_BM


def _expert_body(be_ref, nu_ref, dest_ref, fill_hbm, h_hbm, g_ref, w1_ref, w3_ref, w2_ref, o2_hbm,
                 inv_ref, xbuf, ybuf, gsem, ssem, isem, w1_bf, w3_bf, w2_bf, *, n_tok, n_blocks):
    i = pl.program_id(0)
    n_used = nu_ref[0]
    slot = i % 2
    dummy = 2 * n_tok

    def gather_row(block, r, to_slot):
        a = inv_ref[block * MOE_BM + r]
        tok = jnp.minimum(a >> 1, n_tok - 1)
        return pltpu.make_async_copy(h_hbm.at[pl.ds(tok, 1)], xbuf.at[to_slot, pl.ds(r, 1)], gsem.at[to_slot])

    def scatter_row(block, r, from_slot):
        a = inv_ref[block * MOE_BM + r]
        row = jnp.where(a == dummy, n_tok + r, a >> 1)
        return pltpu.make_async_copy(ybuf.at[from_slot, pl.ds(r, 1)], o2_hbm.at[a & 1, pl.ds(row, 1)],
                                     ssem.at[from_slot])

    def gather_wait(of_slot):
        pltpu.make_async_copy(h_hbm.at[pl.ds(0, MOE_BM)], xbuf.at[of_slot], gsem.at[of_slot]).wait()

    def scatter_wait(of_slot):
        pltpu.make_async_copy(ybuf.at[of_slot], o2_hbm.at[0, pl.ds(0, MOE_BM)], ssem.at[of_slot]).wait()

    @pl.when(i == 0)
    def _():
        fill = pltpu.make_async_copy(fill_hbm, inv_ref, isem)
        fill.start()
        fill.wait()

        def body(c, carry):
            base = c * INV_UNROLL
            rows = [dest_ref[base + u] for u in range(INV_UNROLL)]
            for u in range(INV_UNROLL):
                inv_ref[rows[u]] = base + u
            return carry
        lax.fori_loop(0, dest_ref.shape[0] // INV_UNROLL, body, 0)
        ybuf[...] = jnp.zeros_like(ybuf)
        for plane in range(2):
            init = pltpu.make_async_copy(ybuf.at[plane], o2_hbm.at[plane, pl.ds(n_tok, MOE_BM)], isem)
            init.start()
            init.wait()
        for r in range(MOE_BM):
            gather_row(0, r, 0).start()

    @pl.when(i < n_used)
    def _():
        gather_wait(slot)

        @pl.when((i == 0) | (be_ref[i] != be_ref[jnp.maximum(i - 1, 0)]))
        def _():
            w1_bf[...] = w1_ref[...].astype(BF16)
            w3_bf[...] = w3_ref[...].astype(BF16)
            w2_bf[...] = w2_ref[...].astype(BF16)

        @pl.when(i >= 1)
        def _():
            scatter_wait(slot)

        xn = _rms(xbuf[slot], g_ref[...]).astype(BF16)
        prev = jnp.where(i == 0, n_blocks - 1, i - 1)
        for r in range(MOE_BM):
            gather_row(i + 1, r, 1 - slot).start()
            scatter_row(prev, r, 1 - slot).start()
        h1 = jnp.dot(xn, w1_bf[...], preferred_element_type=F32)
        h3 = jnp.dot(xn, w3_bf[...], preferred_element_type=F32)
        act = (_silu(h1) * h3).astype(BF16)
        ybuf[slot] = jnp.dot(act, w2_bf[...], preferred_element_type=F32)

    @pl.when(i == n_used)
    def _():
        for r in range(MOE_BM):
            scatter_row(i - 1, r, 1 - slot).start()
        gather_wait(slot)
        scatter_wait(slot)
        scatter_wait(1 - slot)


def moe_experts(h2d, dest_flat, g, w1, w3, w2, layer, block_e, n_used):
    n_tok, d = h2d.shape
    n_blocks = block_e.shape[0]
    n_rows = n_blocks * MOE_BM
    f = w1.shape[-1]

    def w_map(i, be, nu, dest):
        return (layer, be[i], 0, 0)

    grid_spec = pltpu.PrefetchScalarGridSpec(
        num_scalar_prefetch=3,
        grid=(n_blocks,),
        in_specs=[pl.BlockSpec(memory_space=pl.ANY),
                  pl.BlockSpec(memory_space=pl.ANY),
                  pl.BlockSpec((1, d), lambda i, be, nu, dest: (0, 0)),
                  pl.BlockSpec((None, None, d, f), w_map),
                  pl.BlockSpec((None, None, d, f), w_map),
                  pl.BlockSpec((None, None, f, d), w_map)],
        out_specs=pl.BlockSpec(memory_space=pl.ANY),
        scratch_shapes=[pltpu.SMEM((n_rows,), I32),
                        pltpu.VMEM((2, MOE_BM, d), F32),
                        pltpu.VMEM((2, MOE_BM, d), F32),
                        pltpu.SemaphoreType.DMA((2,)),
                        pltpu.SemaphoreType.DMA((2,)),
                        pltpu.SemaphoreType.DMA(()),
                        pltpu.VMEM((d, f), BF16), pltpu.VMEM((d, f), BF16), pltpu.VMEM((f, d), BF16)],
    )
    return pl.pallas_call(
        functools.partial(_expert_body, n_tok=n_tok, n_blocks=n_blocks),
        grid_spec=grid_spec,
        out_shape=jax.ShapeDtypeStruct((2, n_tok + DUMP_ROWS, d), F32),
        compiler_params=_params("arbitrary"),
        name="moe_experts",
    )(block_e, n_used, dest_flat, jnp.full((n_rows,), 2 * n_tok, I32), h2d, g.reshape(1, d), w1, w3, w2)


def _combine_body(*refs, final_norm):
    if final_norm:
        h_ref, wgt_ref, o2_ref, gf_ref, o_ref = refs
    else:
        h_ref, wgt_ref, o2_ref, o_ref = refs
    wgt = wgt_ref[...]
    out = h_ref[...] + wgt[:, 0:1] * o2_ref[0] + wgt[:, 1:2] * o2_ref[1]
    if final_norm:
        out = _rms(out, gf_ref[...])
    o_ref[...] = out


def moe_combine(h2d, wgt, o2, g_final=None):
    n_tok, d = h2d.shape
    final_norm = g_final is not None
    in_specs = [pl.BlockSpec((COMBINE_T, d), lambda i: (i, 0)),
                pl.BlockSpec((COMBINE_T, LANES), lambda i: (i, 0)),
                pl.BlockSpec((2, COMBINE_T, d), lambda i: (0, i, 0))]
    args = [h2d, wgt, o2]
    if final_norm:
        in_specs.append(pl.BlockSpec((1, d), lambda i: (0, 0)))
        args.append(g_final.reshape(1, d))
    return pl.pallas_call(
        functools.partial(_combine_body, final_norm=final_norm),
        grid=(n_tok // COMBINE_T,),
        in_specs=in_specs,
        out_specs=pl.BlockSpec((COMBINE_T, d), lambda i: (i, 0)),
        out_shape=jax.ShapeDtypeStruct((n_tok, d), F32),
        compiler_params=_params("parallel"),
        name="moe_combine_final" if final_norm else "moe_combine",
    )(*args)


def hierarchical_moe(h2d, layer, norm_g, w_group, b_group, w_router, b_router, w1, w3, w2, g_final=None):
    n_tok, d = h2d.shape
    pad_l = LANES - N_GROUPS - N_EXPERTS
    w_cat = jnp.concatenate([w_group, w_router, jnp.zeros((d, pad_l), F32)], axis=1)
    b_cat = jnp.concatenate([b_group, b_router, jnp.zeros((pad_l,), F32)]).reshape(1, LANES)
    idx, wgt, cnt = moe_router(h2d, norm_g, w_cat, b_cat)

    counts = cnt[0, EXPERT_LANE0:EXPERT_LANE0 + N_EXPERTS].astype(I32)
    padded = (counts + MOE_BM - 1) // MOE_BM * MOE_BM
    pends = jnp.cumsum(padded)
    pstarts = pends - padded
    n_blocks = (n_tok * 2) // MOE_BM + N_EXPERTS
    blk_row0 = jnp.arange(n_blocks, dtype=I32) * MOE_BM
    block_e = jnp.minimum(jnp.sum(pends[None, :] <= blk_row0[:, None], axis=1), N_EXPERTS - 1).astype(I32)
    n_used = (pends[-1:] // MOE_BM).astype(I32)
    e_iota = jnp.arange(N_EXPERTS, dtype=I32)
    row0 = jnp.sum(jnp.where(idx[:, 0:2, None] == e_iota, pstarts, 0), axis=-1)
    dest = (row0 + idx[:, 2:4]).astype(I32).reshape(-1)

    o2 = moe_experts(h2d, dest, norm_g, w1, w3, w2, layer, block_e, n_used)
    return moe_combine(h2d, wgt, o2, g_final)


def kernel(x, norm_mix, norm_ffn, norm_final, w_in_even, hg_lb_logits, hg_norm, gla_gk2, gla_gk_bias,
           gla_norm, w_out_even, w_qkv_odd, w_o_odd, router_group_w, router_group_b, router_expert_w,
           router_expert_b, expert_w1, expert_w3, expert_w2):
    b, s, d = x.shape
    n_tok = b * s
    depth = norm_mix.shape[0]
    lb_table = jnp.cumsum(jax.nn.softmax(hg_lb_logits.astype(F32), axis=0), axis=0)
    h = x.reshape(n_tok, d)
    for l in range(depth):
        if l % 2 == 0:
            e = l // 2
            w_in = w_in_even[e]
            w_low = jnp.pad(w_in[:, MAIN_IN:], ((0, 0), (0, LANES - GLA_RANK))).astype(BF16)
            proj, glow = norm_matmul(h, norm_mix[l], w_in[:, :MAIN_IN].astype(BF16), w_low, tm=512, tn=1024)
            proj = proj.reshape(b, s, MAIN_IN)
            gk2_pad = jnp.pad(gla_gk2[e], ((0, LANES - GLA_RANK), (0, 0)))
            o_hg = hgrn2_mix(proj, lb_table[l], hg_norm[e])
            o_gla = gla_mix(proj, glow.reshape(b, s, LANES), gk2_pad, gla_gk_bias[e], gla_norm[e])
            mixed = jnp.concatenate([o_hg, o_gla], axis=-1).reshape(n_tok, d)
            h = matmul_residual(mixed, w_out_even[e].astype(BF16), h, tm=512)
        else:
            o = l // 2
            qkv = norm_matmul(h, norm_mix[l], w_qkv_odd[o].astype(BF16), tm=512, tn=1024)
            attn = moba_attention(qkv.reshape(b, s, 3 * d))
            h = matmul_residual(attn.reshape(n_tok, d), w_o_odd[o].astype(BF16), h, tm=512)
        h = hierarchical_moe(h, l, norm_ffn[l], router_group_w[l], router_group_b[l], router_expert_w[l],
                             router_expert_b[l], expert_w1, expert_w3, expert_w2,
                             g_final=norm_final if l == depth - 1 else None)
    return h.reshape(b, s, d)
```

```python
import functools

import jax
import jax.numpy as jnp
from jax import lax
from jax.experimental import pallas as pl
from jax.experimental.pallas import tpu as pltpu

F32 = jnp.float32
BF16 = jnp.bfloat16
I32 = jnp.int32
HIGHEST = lax.Precision.HIGHEST
NEG_INF = float("-inf")
LOG2_E = 1.4426950408889634

EPS = 1e-6
D_MODEL = 2048

HG_HEADS, HG_DK, HG_DV = 8, 128, 128
GLA_HEADS, GLA_DK, GLA_DV = 4, 128, 256
GLA_RANK = 16
GLA_GATE_NORM = 16.0
HG_QK = HG_HEADS * HG_DK
MAIN_IN = 4 * HG_QK + 2 * GLA_HEADS * GLA_DK + 2 * GLA_HEADS * GLA_DV
CHUNK = 64
SUB = 8
REC_T = 256

MOBA_HEADS, MOBA_DH = 16, 128
MOBA_BLOCK = 256
MOBA_TOPK = 3
MOBA_HP = 2
MASK_BIAS = -1e30

N_GROUPS, EXPERTS_PER_GROUP = 4, 8
N_EXPERTS = N_GROUPS * EXPERTS_PER_GROUP
D_EXPERT = D_MODEL // 4
EXPERT_LANE0 = N_GROUPS
ROUTE_TM = 256
MOE_BM = 256
COMBINE_T = 128
ROW_CHUNKS = D_MODEL // 128
BUF_PITCH = ROW_CHUNKS + 8

LANES = 128
VMEM_LIMIT = 56 * 1024 * 1024

NT_DIMS = (((1,), (1,)), ((), ()))
TN_DIMS = (((0,), (0,)), ((), ()))


def _params(*sem):
    return pltpu.CompilerParams(dimension_semantics=sem, vmem_limit_bytes=VMEM_LIMIT)


def _rms(x, g):
    return x * lax.rsqrt(jnp.mean(x * x, axis=-1, keepdims=True) + EPS) * g


def _silu(x):
    return x * jax.nn.sigmoid(x)


def _norm_matmul_body(*refs, has_side):
    if has_side:
        x_ref, g_ref, w_ref, ws_ref, o_ref, os_ref, xn_ref = refs
    else:
        x_ref, g_ref, w_ref, o_ref, xn_ref = refs

    @pl.when(pl.program_id(1) == 0)
    def _():
        xn_ref[...] = _rms(x_ref[...], g_ref[...]).astype(BF16)
        if has_side:
            os_ref[...] = jnp.dot(xn_ref[...], ws_ref[...], preferred_element_type=F32)

    o_ref[...] = jnp.dot(xn_ref[...], w_ref[...], preferred_element_type=F32).astype(o_ref.dtype)


def norm_matmul(x, g, w, w_side=None, *, tm, tn):
    m, k = x.shape
    n = w.shape[1]
    has_side = w_side is not None
    in_specs = [pl.BlockSpec((tm, k), lambda i, j: (i, 0)),
                pl.BlockSpec((1, k), lambda i, j: (0, 0)),
                pl.BlockSpec((k, tn), lambda i, j: (0, j))]
    out_specs = [pl.BlockSpec((tm, tn), lambda i, j: (i, j))]
    out_shape = [jax.ShapeDtypeStruct((m, n), BF16)]
    args = [x, g.reshape(1, k), w]
    if has_side:
        ns = w_side.shape[1]
        in_specs.append(pl.BlockSpec((k, ns), lambda i, j: (0, 0)))
        out_specs.append(pl.BlockSpec((tm, ns), lambda i, j: (i, 0)))
        out_shape.append(jax.ShapeDtypeStruct((m, ns), F32))
        args.append(w_side)
    outs = pl.pallas_call(
        functools.partial(_norm_matmul_body, has_side=has_side),
        grid=(m // tm, n // tn),
        in_specs=in_specs, out_specs=out_specs, out_shape=out_shape,
        scratch_shapes=[pltpu.VMEM((tm, k), BF16)],
        compiler_params=_params("parallel", "arbitrary"),
        name="norm_matmul_side" if has_side else "norm_matmul",
    )(*args)
    return outs if has_side else outs[0]


def _matmul_res_body(a_ref, w_ref, r_ref, o_ref):
    o_ref[...] = r_ref[...] + jnp.dot(a_ref[...], w_ref[...], preferred_element_type=F32)


def matmul_residual(a, w, res, *, tm):
    m, k = a.shape
    n = w.shape[1]
    return pl.pallas_call(
        _matmul_res_body,
        grid=(m // tm,),
        in_specs=[pl.BlockSpec((tm, k), lambda i: (i, 0)),
                  pl.BlockSpec((k, n), lambda i: (0, 0)),
                  pl.BlockSpec((tm, n), lambda i: (i, 0))],
        out_specs=pl.BlockSpec((tm, n), lambda i: (i, 0)),
        out_shape=jax.ShapeDtypeStruct((m, n), F32),
        compiler_params=_params("parallel"),
        name="matmul_residual",
    )(a, w, res)


def _recurrence_step(q, k, v, la, st_ref):
    n_ch = REC_T // CHUNK
    n_sub = CHUNK // SUB
    dv = v.shape[1]
    r_i = lax.broadcasted_iota(I32, (CHUNK, CHUNK), 0)
    c_i = lax.broadcasted_iota(I32, (CHUNK, CHUNK), 1)
    tri = (c_i <= r_i).astype(F32)
    local = [jnp.dot(tri, la[c * CHUNK:(c + 1) * CHUNK], precision=HIGHEST, preferred_element_type=F32)
             for c in range(n_ch)]
    cums = [local[0]]
    for c in range(1, n_ch):
        cums.append(local[c] + cums[-1][CHUNK - 1:CHUNK, :])
    cum = jnp.concatenate(cums, axis=0)
    last = cum[REC_T - 1:REC_T, :]
    vb = v.astype(BF16)
    st = st_ref[...]

    o_inter = lax.dot_general((q * jnp.exp(cum)).astype(BF16), st.astype(BF16), NT_DIMS,
                              preferred_element_type=F32)
    kv = lax.dot_general(vb, (k * jnp.exp(last - cum)).astype(BF16), TN_DIMS, preferred_element_type=F32)
    rows = lax.broadcasted_iota(I32, (SUB, 1), 0)
    ones = jnp.ones((q.shape[1], LANES), BF16)
    s_cross, s_sub, diag = {}, {}, {}
    for c in range(n_ch):
        c0 = c * CHUNK
        if c > 0:
            ref_pt = cum[c0 - 1:c0, :]
            q_t = (q[c0:c0 + CHUNK] * jnp.exp(cum[c0:c0 + CHUNK] - ref_pt)).astype(BF16)
            k_t = (k[:c0] * jnp.exp(ref_pt - cum[:c0])).astype(BF16)
            s_cross[c] = lax.dot_general(q_t, k_t, NT_DIMS, preferred_element_type=F32)
        for b in range(n_sub):
            lo = c0 + b * SUB
            q_b, k_b, c_b = q[lo:lo + SUB], k[lo:lo + SUB], cum[lo:lo + SUB]
            if b > 0:
                ref_pt = cum[lo - 1:lo, :]
                q_t = (q_b * jnp.exp(c_b - ref_pt)).astype(BF16)
                k_t = (k[c0:lo] * jnp.exp(ref_pt - cum[c0:lo])).astype(BF16)
                s_sub[c, b] = lax.dot_general(q_t, k_t, NT_DIMS, preferred_element_type=F32)
            terms = []
            for j in range(SUB):
                e = jnp.exp(jnp.where(rows >= j, c_b - c_b[j:j + 1, :], NEG_INF))
                terms.append(q_b * k_b[j:j + 1, :] * e)
            diag[c, b] = jnp.dot(jnp.concatenate(terms, axis=0).astype(BF16), ones, preferred_element_type=F32)

    cross = {c: jnp.dot(s_cross[c].astype(BF16), vb[:c * CHUNK], preferred_element_type=F32) for c in s_cross}
    sub = {cb: jnp.dot(s_sub[cb].astype(BF16), v[cb[0] * CHUNK:cb[0] * CHUNK + cb[1] * SUB].astype(BF16),
                       preferred_element_type=F32) for cb in s_sub}
    st_ref[...] = st * jnp.exp(last) + kv

    outs = []
    for c in range(n_ch):
        for b in range(n_sub):
            lo = c * CHUNK + b * SUB
            acc = o_inter[lo:lo + SUB]
            if c > 0:
                acc = acc + cross[c][b * SUB:(b + 1) * SUB]
            if b > 0:
                acc = acc + sub[c, b]
            v_b = v[lo:lo + SUB]
            for j in range(SUB):
                col = diag[c, b][j * SUB:(j + 1) * SUB]
                if dv > LANES:
                    col = jnp.concatenate([col] * (dv // LANES), axis=1)
                acc = acc + col * v_b[j:j + 1, :]
            outs.append(acc)
    return jnp.concatenate(outs, axis=0)


def _log_sigmoid(z):
    return jnp.minimum(z, 0.0) - jnp.log(1.0 + jnp.exp(-jnp.abs(z)))


def _hgrn2_body(hq_ref, hf_ref, hi_ref, hg_ref, lb_ref, nw_ref, o_ref, st_ref):
    @pl.when(pl.program_id(2) == 0)
    def _():
        st_ref[...] = jnp.zeros_like(st_ref)

    lb = lb_ref[...]
    forget = lb + (1.0 - lb) * jax.nn.sigmoid(hf_ref[0].astype(F32))
    q = _silu(hq_ref[0].astype(F32)) * (HG_DK ** -0.5)
    o = _recurrence_step(q, 1.0 - forget, hi_ref[0].astype(F32), jnp.log(forget), st_ref)
    o_ref[0] = (_rms(o, nw_ref[...]) * _silu(hg_ref[0].astype(F32))).astype(o_ref.dtype)


def _gla_body(gq_ref, gk_ref, gv_ref, gg_ref, glow_ref, gk2_ref, gb_ref, nw_ref, o_ref, st_ref):
    @pl.when(pl.program_id(2) == 0)
    def _():
        st_ref[...] = jnp.zeros_like(st_ref)

    z = jnp.dot(glow_ref[0], gk2_ref[...], precision=HIGHEST, preferred_element_type=F32) + gb_ref[...]
    la = _log_sigmoid(z) / GLA_GATE_NORM
    q = gq_ref[0].astype(F32) * (GLA_DK ** -0.5)
    o = _recurrence_step(q, gk_ref[0].astype(F32), gv_ref[0].astype(F32), la, st_ref)
    o_ref[0] = (_rms(o, nw_ref[...]) * _silu(gg_ref[0].astype(F32))).astype(o_ref.dtype)


def hgrn2_mix(proj, lb, hg_norm):
    b, s, _ = proj.shape

    def col(base):
        return pl.BlockSpec((1, REC_T, HG_DK), lambda bi, h, t: (bi, t, base + h))

    return pl.pallas_call(
        _hgrn2_body,
        grid=(b, HG_HEADS, s // REC_T),
        in_specs=[col(0), col(HG_HEADS), col(2 * HG_HEADS), col(3 * HG_HEADS),
                  pl.BlockSpec((1, HG_DK), lambda bi, h, t: (0, h)),
                  pl.BlockSpec((1, HG_DV), lambda bi, h, t: (0, 0))],
        out_specs=pl.BlockSpec((1, REC_T, HG_DV), lambda bi, h, t: (bi, t, h)),
        out_shape=jax.ShapeDtypeStruct((b, s, HG_HEADS * HG_DV), BF16),
        scratch_shapes=[pltpu.VMEM((HG_DV, HG_DK), F32)],
        compiler_params=_params("parallel", "parallel", "arbitrary"),
        name="hgrn2_mix",
    )(proj, proj, proj, proj, lb.reshape(1, HG_QK), hg_norm.reshape(1, HG_DV))


def gla_mix(proj, glow, gk2_pad, gk_bias, gla_norm):
    b, s, _ = proj.shape
    q0 = 4 * HG_QK // GLA_DK
    k0 = q0 + GLA_HEADS
    v0 = (4 * HG_QK + 2 * GLA_HEADS * GLA_DK) // GLA_DV
    g0 = v0 + GLA_HEADS
    return pl.pallas_call(
        _gla_body,
        grid=(b, GLA_HEADS, s // REC_T),
        in_specs=[pl.BlockSpec((1, REC_T, GLA_DK), lambda bi, h, t: (bi, t, q0 + h)),
                  pl.BlockSpec((1, REC_T, GLA_DK), lambda bi, h, t: (bi, t, k0 + h)),
                  pl.BlockSpec((1, REC_T, GLA_DV), lambda bi, h, t: (bi, t, v0 + h)),
                  pl.BlockSpec((1, REC_T, GLA_DV), lambda bi, h, t: (bi, t, g0 + h)),
                  pl.BlockSpec((1, REC_T, LANES), lambda bi, h, t: (bi, t, 0)),
                  pl.BlockSpec((LANES, GLA_DK), lambda bi, h, t: (0, h)),
                  pl.BlockSpec((1, GLA_DK), lambda bi, h, t: (0, h)),
                  pl.BlockSpec((1, GLA_DV), lambda bi, h, t: (0, 0))],
        out_specs=pl.BlockSpec((1, REC_T, GLA_DV), lambda bi, h, t: (bi, t, h)),
        out_shape=jax.ShapeDtypeStruct((b, s, GLA_HEADS * GLA_DV), BF16),
        scratch_shapes=[pltpu.VMEM((GLA_DV, GLA_DK), F32)],
        compiler_params=_params("parallel", "parallel", "arbitrary"),
        name="gla_mix",
    )(proj, proj, proj, proj, glow, gk2_pad, gk_bias.reshape(1, -1), gla_norm.reshape(1, GLA_DV))


def _moba_body(q_ref, k_ref, v_ref, o_ref, kmean_ref, vt_ref, bias_ref, sa_ref, sb_ref, *, n_blocks):
    qi = pl.program_id(2)
    blk, dh = MOBA_BLOCK, MOBA_DH
    heads = [slice(hh * dh, (hh + 1) * dh) for hh in range(MOBA_HP)]

    @pl.when(qi == 0)
    def _():
        for n in range(n_blocks):
            rs = slice(n * blk, (n + 1) * blk)
            for hh, cols in enumerate(heads):
                kmean_ref[hh, n:n + 1, :] = jnp.mean(k_ref[0, rs, cols].astype(F32), axis=0, keepdims=True)
                vt_ref[hh, :, rs] = v_ref[0, rs, cols].astype(F32).T.astype(BF16)

    c = (MOBA_DH ** -0.5) * LOG2_E
    own = pl.ds(pl.multiple_of(qi * blk, blk), blk)
    brow = lax.broadcasted_iota(I32, (n_blocks, blk), 0)
    past = brow < qi
    k_i = lax.broadcasted_iota(I32, (blk, blk), 0)
    q_i = lax.broadcasted_iota(I32, (blk, blk), 1)

    qs, carry0 = [], []
    for hh, cols in enumerate(heads):
        q = q_ref[0, :, cols]
        qs.append(q)
        gate = lax.dot_general(kmean_ref[hh], q.astype(F32), NT_DIMS, precision=HIGHEST,
                               preferred_element_type=F32)
        gate = jnp.where(past, gate, NEG_INF)
        rank = jnp.zeros((n_blocks, blk), I32)
        for m in range(n_blocks):
            gm = gate[m:m + 1, :]
            rank = rank + jnp.where(gm > gate, 1, jnp.where(gm == gate, jnp.where(brow > m, 1, 0), 0))
        keep = jnp.where(past, jnp.where(rank < MOBA_TOPK, 1, 0), 0)
        bias_ref[hh] = jnp.where(keep > 0, 0.0, MASK_BIAS)
        s = lax.dot_general(k_ref[0, own, cols], q, NT_DIMS, preferred_element_type=F32)
        s = jnp.where(k_i <= q_i, s, NEG_INF)
        m0 = jnp.max(s, axis=0, keepdims=True)
        p = jnp.exp2((s - m0) * c)
        l0 = jnp.sum(p, axis=0, keepdims=True)
        acc0 = jnp.dot(vt_ref[hh, :, own], p.astype(BF16), preferred_element_type=F32)
        carry0 += [m0, l0, acc0]

    n_pairs = (qi + 1) // 2

    def score_pair(pair, dst_ref):
        p = jnp.minimum(pair, n_blocks // 2 - 1)
        r01 = pl.ds(pl.multiple_of(p * 2 * blk, 2 * blk), 2 * blk)
        for hh, cols in enumerate(heads):
            dst_ref[hh] = lax.dot_general(k_ref[0, r01, cols], qs[hh], NT_DIMS, preferred_element_type=F32)

    def absorb_pair(pair, src_ref, carry):
        n0 = 2 * pair
        r01 = pl.ds(pl.multiple_of(n0 * blk, 2 * blk), 2 * blk)
        out = []
        for hh, cols in enumerate(heads):
            m_run, l_run, acc = carry[3 * hh:3 * hh + 3]
            s0 = src_ref[hh, 0:blk, :] + bias_ref[hh, pl.ds(n0, 1), :]
            s1 = src_ref[hh, blk:2 * blk, :] + bias_ref[hh, pl.ds(n0 + 1, 1), :]
            m_new = jnp.maximum(m_run, jnp.maximum(jnp.max(s0, axis=0, keepdims=True),
                                                   jnp.max(s1, axis=0, keepdims=True)))
            alpha = jnp.exp2((m_run - m_new) * c)
            p0 = jnp.exp2((s0 - m_new) * c)
            p1 = jnp.exp2((s1 - m_new) * c)
            l_new = alpha * l_run + jnp.sum(p0, axis=0, keepdims=True) + jnp.sum(p1, axis=0, keepdims=True)
            p01 = jnp.concatenate([p0.astype(BF16), p1.astype(BF16)], axis=0)
            acc = alpha * acc + jnp.dot(vt_ref[hh, :, r01], p01, preferred_element_type=F32)
            out += [m_new, l_new, acc]
        return tuple(out)

    def two_pairs(t, carry):
        score_pair(2 * t + 1, sb_ref)
        carry = absorb_pair(2 * t, sa_ref, carry)

        def second(carry):
            score_pair(2 * t + 2, sa_ref)
            return absorb_pair(2 * t + 1, sb_ref, carry)

        return lax.cond(2 * t + 1 < n_pairs, second, lambda carry: carry, carry)

    score_pair(0, sa_ref)
    fin = lax.fori_loop(0, (n_pairs + 1) // 2, two_pairs, tuple(carry0))
    for hh, cols in enumerate(heads):
        o_ref[0, :, cols] = (fin[3 * hh + 2] / fin[3 * hh + 1]).T.astype(o_ref.dtype)


def moba_attention(qkv):
    b, s, _ = qkv.shape
    n_blocks = s // MOBA_BLOCK
    hw = MOBA_HP * MOBA_DH
    hsteps = MOBA_HEADS // MOBA_HP
    return pl.pallas_call(
        functools.partial(_moba_body, n_blocks=n_blocks),
        grid=(b, hsteps, n_blocks),
        in_specs=[pl.BlockSpec((1, MOBA_BLOCK, hw), lambda bi, h, t: (bi, t, h)),
                  pl.BlockSpec((1, s, hw), lambda bi, h, t: (bi, 0, hsteps + h)),
                  pl.BlockSpec((1, s, hw), lambda bi, h, t: (bi, 0, 2 * hsteps + h))],
        out_specs=pl.BlockSpec((1, MOBA_BLOCK, hw), lambda bi, h, t: (bi, t, h)),
        out_shape=jax.ShapeDtypeStruct((b, s, D_MODEL), BF16),
        scratch_shapes=[pltpu.VMEM((MOBA_HP, n_blocks, MOBA_DH), F32),
                        pltpu.VMEM((MOBA_HP, MOBA_DH, s), BF16),
                        pltpu.VMEM((MOBA_HP, n_blocks, MOBA_BLOCK), F32),
                        pltpu.VMEM((MOBA_HP, 2 * MOBA_BLOCK, MOBA_BLOCK), F32),
                        pltpu.VMEM((MOBA_HP, 2 * MOBA_BLOCK, MOBA_BLOCK), F32)],
        compiler_params=_params("parallel", "parallel", "arbitrary"),
        name="moba_attention",
    )(qkv, qkv, qkv)


def _router_body(h_ref, g_ref, w_ref, b_ref, idx_ref, wgt_ref, cnt_ref, x3_ref, run_ref):
    step = pl.program_id(0)

    @pl.when(step == 0)
    def _():
        run_ref[...] = jnp.zeros_like(run_ref)

    tm = h_ref.shape[0]
    xn = _rms(h_ref[...], g_ref[...])
    for ch in range(ROW_CHUNKS):
        x3_ref[pl.ds(ch, tm, stride=ROW_CHUNKS), :] = xn[:, ch * LANES:(ch + 1) * LANES]
    logits = jnp.dot(xn, w_ref[...], precision=HIGHEST, preferred_element_type=F32) + b_ref[...]
    lane = lax.broadcasted_iota(I32, (tm, LANES), 1)

    def first_max(vals):
        top = jnp.max(vals, axis=-1, keepdims=True)
        where = jnp.min(jnp.where(vals == top, lane, LANES), axis=-1, keepdims=True)
        return top, where

    g_logits = jnp.where(lane < N_GROUPS, logits, NEG_INF)
    g_top, grp = first_max(g_logits)
    p_grp = 1.0 / jnp.sum(jnp.exp(g_logits - g_top), axis=-1, keepdims=True)
    lo = EXPERT_LANE0 + grp * EXPERTS_PER_GROUP
    e_logits = jnp.where((lane >= lo) & (lane < lo + EXPERTS_PER_GROUP), logits, NEG_INF)
    v0, j0 = first_max(e_logits)
    v1, j1 = first_max(jnp.where(lane == j0, NEG_INF, e_logits))
    t = jnp.exp(v1 - v0)
    w0 = p_grp / (1.0 + t)
    w1 = p_grp * t / (1.0 + t)

    hit0 = lane == j0
    hit1 = lane == j1
    member = jnp.where(hit0 | hit1, 1.0, 0.0)
    r_i = lax.broadcasted_iota(I32, (tm, tm), 0)
    c_i = lax.broadcasted_iota(I32, (tm, tm), 1)
    before = jnp.where(c_i < r_i, 1.0, 0.0).astype(BF16)
    prior = jnp.dot(before, member.astype(BF16), preferred_element_type=F32) + run_ref[...]
    rank0 = jnp.sum(jnp.where(hit0, prior, 0.0), axis=-1, keepdims=True).astype(I32)
    rank1 = jnp.sum(jnp.where(hit1, prior, 0.0), axis=-1, keepdims=True).astype(I32)
    run_ref[...] = run_ref[...] + jnp.sum(member, axis=0, keepdims=True)
    cnt_ref[...] = run_ref[...]

    idx_ref[...] = jnp.where(lane == 0, j0 - EXPERT_LANE0,
                             jnp.where(lane == 1, j1 - EXPERT_LANE0,
                                       jnp.where(lane == 2, rank0, jnp.where(lane == 3, rank1, 0))))
    wgt_ref[...] = jnp.where(lane == 0, w0, jnp.where(lane == 1, w1, 0.0))


def moe_router(h2d, g, w_cat, b_cat):
    n_tok, d = h2d.shape
    return pl.pallas_call(
        _router_body,
        grid=(n_tok // ROUTE_TM,),
        in_specs=[pl.BlockSpec((ROUTE_TM, d), lambda i: (i, 0)),
                  pl.BlockSpec((1, d), lambda i: (0, 0)),
                  pl.BlockSpec((d, LANES), lambda i: (0, 0)),
                  pl.BlockSpec((1, LANES), lambda i: (0, 0))],
        out_specs=[pl.BlockSpec((ROUTE_TM, LANES), lambda i: (i, 0)),
                   pl.BlockSpec((ROUTE_TM, LANES), lambda i: (i, 0)),
                   pl.BlockSpec((1, LANES), lambda i: (0, 0)),
                   pl.BlockSpec((ROUTE_TM * ROW_CHUNKS, LANES), lambda i: (i, 0))],
        out_shape=[jax.ShapeDtypeStruct((n_tok, LANES), I32),
                   jax.ShapeDtypeStruct((n_tok, LANES), F32),
                   jax.ShapeDtypeStruct((1, LANES), F32),
                   jax.ShapeDtypeStruct((n_tok * ROW_CHUNKS, LANES), F32)],
        scratch_shapes=[pltpu.VMEM((1, LANES), F32)],
        compiler_params=_params("arbitrary"),
        name="moe_router",
    )(h2d, g.reshape(1, d), w_cat, b_cat)


INV_UNROLL = 16
DUMP_ROWS = MOE_BM


def _expert_body(be_ref, nu_ref, dest_ref, fill_hbm, x3_hbm, w1_ref, w3_ref, w2_ref, o2_hbm,
                 inv_ref, xbuf, ybuf, gsem, ssem, isem, w1_bf, w3_bf, w2_bf, *, n_tok, n_blocks):
    i = pl.program_id(0)
    n_used = nu_ref[0]
    slot = i % 2
    dummy = 2 * n_tok

    def gather_row(block, r, to_slot):
        a = inv_ref[block * MOE_BM + r]
        tok = jnp.minimum(a >> 1, n_tok - 1)
        return pltpu.make_async_copy(x3_hbm.at[pl.ds(tok * ROW_CHUNKS, ROW_CHUNKS)],
                                     xbuf.at[to_slot, pl.ds(r * BUF_PITCH, ROW_CHUNKS)], gsem.at[to_slot])

    def scatter_row(block, r, from_slot):
        a = inv_ref[block * MOE_BM + r]
        row = jnp.where(a == dummy, n_tok + r, a >> 1)
        return pltpu.make_async_copy(ybuf.at[from_slot, pl.ds(r * BUF_PITCH, ROW_CHUNKS)],
                                     o2_hbm.at[a & 1, pl.ds(row * ROW_CHUNKS, ROW_CHUNKS)], ssem.at[from_slot])

    block_rows = MOE_BM * ROW_CHUNKS

    def gather_wait(of_slot):
        pltpu.make_async_copy(x3_hbm.at[pl.ds(0, block_rows)], xbuf.at[of_slot, pl.ds(0, block_rows)],
                              gsem.at[of_slot]).wait()

    def scatter_wait(of_slot):
        pltpu.make_async_copy(ybuf.at[of_slot, pl.ds(0, block_rows)], o2_hbm.at[0, pl.ds(0, block_rows)],
                              ssem.at[of_slot]).wait()

    @pl.when(i == 0)
    def _():
        fill = pltpu.make_async_copy(fill_hbm, inv_ref, isem)
        fill.start()
        fill.wait()

        def body(c, carry):
            base = c * INV_UNROLL
            rows = [dest_ref[base + u] for u in range(INV_UNROLL)]
            for u in range(INV_UNROLL):
                inv_ref[rows[u]] = base + u
            return carry
        lax.fori_loop(0, dest_ref.shape[0] // INV_UNROLL, body, 0)
        ybuf[...] = jnp.zeros_like(ybuf)
        for plane in range(2):
            init = pltpu.make_async_copy(ybuf.at[plane, pl.ds(0, block_rows)],
                                         o2_hbm.at[plane, pl.ds(n_tok * ROW_CHUNKS, block_rows)], isem)
            init.start()
            init.wait()
        for r in range(MOE_BM):
            gather_row(0, r, 0).start()

    @pl.when(i < n_used)
    def _():
        gather_wait(slot)

        @pl.when((i == 0) | (be_ref[i] != be_ref[jnp.maximum(i - 1, 0)]))
        def _():
            w1_bf[...] = w1_ref[...].astype(BF16)
            w3_bf[...] = w3_ref[...].astype(BF16)
            w2_bf[...] = w2_ref[...].astype(BF16)

        @pl.when(i >= 1)
        def _():
            scatter_wait(slot)

        xn = jnp.concatenate([xbuf[slot, pl.ds(ch, MOE_BM, stride=BUF_PITCH), :] for ch in range(ROW_CHUNKS)],
                             axis=1).astype(BF16)
        prev = jnp.where(i == 0, n_blocks - 1, i - 1)
        for r in range(MOE_BM):
            gather_row(i + 1, r, 1 - slot).start()
            scatter_row(prev, r, 1 - slot).start()
        h1 = jnp.dot(xn, w1_bf[...], preferred_element_type=F32)
        h3 = jnp.dot(xn, w3_bf[...], preferred_element_type=F32)
        act = (_silu(h1) * h3).astype(BF16)
        y = jnp.dot(act, w2_bf[...], preferred_element_type=F32)
        for ch in range(ROW_CHUNKS):
            ybuf[slot, pl.ds(ch, MOE_BM, stride=BUF_PITCH), :] = y[:, ch * LANES:(ch + 1) * LANES]

    @pl.when(i == n_used)
    def _():
        for r in range(MOE_BM):
            scatter_row(i - 1, r, 1 - slot).start()
        gather_wait(slot)
        scatter_wait(slot)
        scatter_wait(1 - slot)


def moe_experts(x3, dest_flat, w1, w3, w2, layer, block_e, n_used):
    n_tok = x3.shape[0] // ROW_CHUNKS
    d = D_MODEL
    n_blocks = block_e.shape[0]
    n_rows = n_blocks * MOE_BM
    f = w1.shape[-1]

    def w_map(i, be, nu, dest):
        return (layer, be[i], 0, 0)

    grid_spec = pltpu.PrefetchScalarGridSpec(
        num_scalar_prefetch=3,
        grid=(n_blocks,),
        in_specs=[pl.BlockSpec(memory_space=pl.ANY),
                  pl.BlockSpec(memory_space=pl.ANY),
                  pl.BlockSpec((None, None, d, f), w_map),
                  pl.BlockSpec((None, None, d, f), w_map),
                  pl.BlockSpec((None, None, f, d), w_map)],
        out_specs=pl.BlockSpec(memory_space=pl.ANY),
        scratch_shapes=[pltpu.SMEM((n_rows,), I32),
                        pltpu.VMEM((2, MOE_BM * BUF_PITCH, LANES), F32),
                        pltpu.VMEM((2, MOE_BM * BUF_PITCH, LANES), F32),
                        pltpu.SemaphoreType.DMA((2,)),
                        pltpu.SemaphoreType.DMA((2,)),
                        pltpu.SemaphoreType.DMA(()),
                        pltpu.VMEM((d, f), BF16), pltpu.VMEM((d, f), BF16), pltpu.VMEM((f, d), BF16)],
    )
    return pl.pallas_call(
        functools.partial(_expert_body, n_tok=n_tok, n_blocks=n_blocks),
        grid_spec=grid_spec,
        out_shape=jax.ShapeDtypeStruct((2, (n_tok + DUMP_ROWS) * ROW_CHUNKS, LANES), F32),
        compiler_params=_params("arbitrary"),
        name="moe_experts",
    )(block_e, n_used, dest_flat, jnp.full((n_rows,), 2 * n_tok, I32), x3, w1, w3, w2)


def _combine_body(*refs, final_norm):
    if final_norm:
        h_ref, wgt_ref, o2_ref, gf_ref, o_ref = refs
    else:
        h_ref, wgt_ref, o2_ref, o_ref = refs
    wgt = wgt_ref[...]
    y0, y1 = (jnp.concatenate([o2_ref[s, pl.ds(ch, COMBINE_T, stride=ROW_CHUNKS), :] for ch in range(ROW_CHUNKS)], axis=1)
              for s in range(2))
    out = h_ref[...] + wgt[:, 0:1] * y0 + wgt[:, 1:2] * y1
    if final_norm:
        out = _rms(out, gf_ref[...])
    o_ref[...] = out


def moe_combine(h2d, wgt, o2, g_final=None):
    n_tok, d = h2d.shape
    final_norm = g_final is not None
    in_specs = [pl.BlockSpec((COMBINE_T, d), lambda i: (i, 0)),
                pl.BlockSpec((COMBINE_T, LANES), lambda i: (i, 0)),
                pl.BlockSpec((2, COMBINE_T * ROW_CHUNKS, LANES), lambda i: (0, i, 0))]
    args = [h2d, wgt, o2]
    if final_norm:
        in_specs.append(pl.BlockSpec((1, d), lambda i: (0, 0)))
        args.append(g_final.reshape(1, d))
    return pl.pallas_call(
        functools.partial(_combine_body, final_norm=final_norm),
        grid=(n_tok // COMBINE_T,),
        in_specs=in_specs,
        out_specs=pl.BlockSpec((COMBINE_T, d), lambda i: (i, 0)),
        out_shape=jax.ShapeDtypeStruct((n_tok, d), F32),
        compiler_params=_params("parallel"),
        name="moe_combine_final" if final_norm else "moe_combine",
    )(*args)


def hierarchical_moe(h2d, layer, norm_g, w_group, b_group, w_router, b_router, w1, w3, w2, g_final=None):
    n_tok, d = h2d.shape
    pad_l = LANES - N_GROUPS - N_EXPERTS
    w_cat = jnp.concatenate([w_group, w_router, jnp.zeros((d, pad_l), F32)], axis=1)
    b_cat = jnp.concatenate([b_group, b_router, jnp.zeros((pad_l,), F32)]).reshape(1, LANES)
    idx, wgt, cnt, x3 = moe_router(h2d, norm_g, w_cat, b_cat)

    counts = cnt[0, EXPERT_LANE0:EXPERT_LANE0 + N_EXPERTS].astype(I32)
    padded = (counts + MOE_BM - 1) // MOE_BM * MOE_BM
    pends = jnp.cumsum(padded)
    pstarts = pends - padded
    n_blocks = (n_tok * 2) // MOE_BM + N_EXPERTS
    blk_row0 = jnp.arange(n_blocks, dtype=I32) * MOE_BM
    block_e = jnp.minimum(jnp.sum(pends[None, :] <= blk_row0[:, None], axis=1), N_EXPERTS - 1).astype(I32)
    n_used = (pends[-1:] // MOE_BM).astype(I32)
    e_iota = jnp.arange(N_EXPERTS, dtype=I32)
    row0 = jnp.sum(jnp.where(idx[:, 0:2, None] == e_iota, pstarts, 0), axis=-1)
    dest = (row0 + idx[:, 2:4]).astype(I32).reshape(-1)

    o2 = moe_experts(x3, dest, w1, w3, w2, layer, block_e, n_used)
    return moe_combine(h2d, wgt, o2, g_final)


def kernel(x, norm_mix, norm_ffn, norm_final, w_in_even, hg_lb_logits, hg_norm, gla_gk2, gla_gk_bias,
           gla_norm, w_out_even, w_qkv_odd, w_o_odd, router_group_w, router_group_b, router_expert_w,
           router_expert_b, expert_w1, expert_w3, expert_w2):
    b, s, d = x.shape
    n_tok = b * s
    depth = norm_mix.shape[0]
    lb_table = jnp.cumsum(jax.nn.softmax(hg_lb_logits.astype(F32), axis=0), axis=0)
    h = x.reshape(n_tok, d)
    for l in range(depth):
        if l % 2 == 0:
            e = l // 2
            w_in = w_in_even[e]
            w_low = jnp.pad(w_in[:, MAIN_IN:], ((0, 0), (0, LANES - GLA_RANK))).astype(BF16)
            proj, glow = norm_matmul(h, norm_mix[l], w_in[:, :MAIN_IN].astype(BF16), w_low, tm=512, tn=1024)
            proj = proj.reshape(b, s, MAIN_IN)
            gk2_pad = jnp.pad(gla_gk2[e], ((0, LANES - GLA_RANK), (0, 0)))
            o_hg = hgrn2_mix(proj, lb_table[l], hg_norm[e])
            o_gla = gla_mix(proj, glow.reshape(b, s, LANES), gk2_pad, gla_gk_bias[e], gla_norm[e])
            mixed = jnp.concatenate([o_hg, o_gla], axis=-1).reshape(n_tok, d)
            h = matmul_residual(mixed, w_out_even[e].astype(BF16), h, tm=512)
        else:
            o = l // 2
            qkv = norm_matmul(h, norm_mix[l], w_qkv_odd[o].astype(BF16), tm=512, tn=1024)
            attn = moba_attention(qkv.reshape(b, s, 3 * d))
            h = matmul_residual(attn.reshape(n_tok, d), w_o_odd[o].astype(BF16), h, tm=512)
        h = hierarchical_moe(h, l, norm_ffn[l], router_group_w[l], router_group_b[l], router_expert_w[l],
                             router_expert_b[l], expert_w1, expert_w3, expert_w2,
                             g_final=norm_final if l == depth - 1 else None)
    return h.reshape(b, s, d)
```

```python
import functools

import jax
import jax.numpy as jnp
from jax import lax
from jax.experimental import pallas as pl
from jax.experimental.pallas import tpu as pltpu

F32 = jnp.float32
BF16 = jnp.bfloat16
I32 = jnp.int32
HIGHEST = lax.Precision.HIGHEST
NEG_INF = float("-inf")
LOG2_E = 1.4426950408889634

EPS = 1e-6
D_MODEL = 2048

HG_HEADS, HG_DK, HG_DV = 8, 128, 128
GLA_HEADS, GLA_DK, GLA_DV = 4, 128, 256
GLA_RANK = 16
GLA_GATE_NORM = 16.0
HG_QK = HG_HEADS * HG_DK
MAIN_IN = 4 * HG_QK + 2 * GLA_HEADS * GLA_DK + 2 * GLA_HEADS * GLA_DV
CHUNK = 64
SUB = 8
REC_T = 256

MOBA_HEADS, MOBA_DH = 16, 128
MOBA_BLOCK = 256
MOBA_TOPK = 3
MOBA_HP = 2
MASK_BIAS = -1e30

N_GROUPS, EXPERTS_PER_GROUP = 4, 8
N_EXPERTS = N_GROUPS * EXPERTS_PER_GROUP
D_EXPERT = D_MODEL // 4
EXPERT_LANE0 = N_GROUPS
ROUTE_TM = 256
MOE_BM = 256
COMBINE_T = 128
ROW_CHUNKS = D_MODEL // 128
BUF_PITCH = ROW_CHUNKS + 8

LANES = 128
VMEM_LIMIT = 56 * 1024 * 1024

NT_DIMS = (((1,), (1,)), ((), ()))
TN_DIMS = (((0,), (0,)), ((), ()))


def _params(*sem):
    return pltpu.CompilerParams(dimension_semantics=sem, vmem_limit_bytes=VMEM_LIMIT)


def _rms(x, g):
    return x * lax.rsqrt(jnp.mean(x * x, axis=-1, keepdims=True) + EPS) * g


def _silu(x):
    return x * jax.nn.sigmoid(x)


def _norm_matmul_body(*refs, has_side):
    if has_side:
        x_ref, g_ref, w_ref, ws_ref, o_ref, os_ref, xn_ref = refs
    else:
        x_ref, g_ref, w_ref, o_ref, xn_ref = refs

    @pl.when(pl.program_id(1) == 0)
    def _():
        xn_ref[...] = _rms(x_ref[...], g_ref[...]).astype(BF16)
        if has_side:
            os_ref[...] = jnp.dot(xn_ref[...], ws_ref[...], preferred_element_type=F32)

    o_ref[...] = jnp.dot(xn_ref[...], w_ref[...], preferred_element_type=F32).astype(o_ref.dtype)


def norm_matmul(x, g, w, w_side=None, *, tm, tn):
    m, k = x.shape
    n = w.shape[1]
    has_side = w_side is not None
    in_specs = [pl.BlockSpec((tm, k), lambda i, j: (i, 0)),
                pl.BlockSpec((1, k), lambda i, j: (0, 0)),
                pl.BlockSpec((k, tn), lambda i, j: (0, j))]
    out_specs = [pl.BlockSpec((tm, tn), lambda i, j: (i, j))]
    out_shape = [jax.ShapeDtypeStruct((m, n), BF16)]
    args = [x, g.reshape(1, k), w]
    if has_side:
        ns = w_side.shape[1]
        in_specs.append(pl.BlockSpec((k, ns), lambda i, j: (0, 0)))
        out_specs.append(pl.BlockSpec((tm, ns), lambda i, j: (i, 0)))
        out_shape.append(jax.ShapeDtypeStruct((m, ns), F32))
        args.append(w_side)
    outs = pl.pallas_call(
        functools.partial(_norm_matmul_body, has_side=has_side),
        grid=(m // tm, n // tn),
        in_specs=in_specs, out_specs=out_specs, out_shape=out_shape,
        scratch_shapes=[pltpu.VMEM((tm, k), BF16)],
        compiler_params=_params("parallel", "arbitrary"),
        name="norm_matmul_side" if has_side else "norm_matmul",
    )(*args)
    return outs if has_side else outs[0]


def _matmul_res_body(a_ref, w_ref, r_ref, o_ref):
    o_ref[...] = r_ref[...] + jnp.dot(a_ref[...], w_ref[...], preferred_element_type=F32)


def matmul_residual(a, w, res, *, tm):
    m, k = a.shape
    n = w.shape[1]
    return pl.pallas_call(
        _matmul_res_body,
        grid=(m // tm,),
        in_specs=[pl.BlockSpec((tm, k), lambda i: (i, 0)),
                  pl.BlockSpec((k, n), lambda i: (0, 0)),
                  pl.BlockSpec((tm, n), lambda i: (i, 0))],
        out_specs=pl.BlockSpec((tm, n), lambda i: (i, 0)),
        out_shape=jax.ShapeDtypeStruct((m, n), F32),
        compiler_params=_params("parallel"),
        name="matmul_residual",
    )(a, w, res)


def _recurrence_step(q, k, v, la, st_ref):
    n_ch = REC_T // CHUNK
    n_sub = CHUNK // SUB
    dv = v.shape[1]
    r_i = lax.broadcasted_iota(I32, (CHUNK, CHUNK), 0)
    c_i = lax.broadcasted_iota(I32, (CHUNK, CHUNK), 1)
    tri = (c_i <= r_i).astype(F32)
    local = [jnp.dot(tri, la[c * CHUNK:(c + 1) * CHUNK], precision=HIGHEST, preferred_element_type=F32)
             for c in range(n_ch)]
    cums = [local[0]]
    for c in range(1, n_ch):
        cums.append(local[c] + cums[-1][CHUNK - 1:CHUNK, :])
    cum = jnp.concatenate(cums, axis=0)
    last = cum[REC_T - 1:REC_T, :]
    vb = v.astype(BF16)
    st = st_ref[...]

    o_inter = lax.dot_general((q * jnp.exp(cum)).astype(BF16), st.astype(BF16), NT_DIMS,
                              preferred_element_type=F32)
    kv = lax.dot_general(vb, (k * jnp.exp(last - cum)).astype(BF16), TN_DIMS, preferred_element_type=F32)
    rows = lax.broadcasted_iota(I32, (SUB, 1), 0)
    ones = jnp.ones((q.shape[1], LANES), BF16)
    s_cross, s_sub, diag = {}, {}, {}
    for c in range(n_ch):
        c0 = c * CHUNK
        if c > 0:
            ref_pt = cum[c0 - 1:c0, :]
            q_t = (q[c0:c0 + CHUNK] * jnp.exp(cum[c0:c0 + CHUNK] - ref_pt)).astype(BF16)
            k_t = (k[:c0] * jnp.exp(ref_pt - cum[:c0])).astype(BF16)
            s_cross[c] = lax.dot_general(q_t, k_t, NT_DIMS, preferred_element_type=F32)
        for b in range(n_sub):
            lo = c0 + b * SUB
            q_b, k_b, c_b = q[lo:lo + SUB], k[lo:lo + SUB], cum[lo:lo + SUB]
            if b > 0:
                ref_pt = cum[lo - 1:lo, :]
                q_t = (q_b * jnp.exp(c_b - ref_pt)).astype(BF16)
                k_t = (k[c0:lo] * jnp.exp(ref_pt - cum[c0:lo])).astype(BF16)
                s_sub[c, b] = lax.dot_general(q_t, k_t, NT_DIMS, preferred_element_type=F32)
            terms = []
            for j in range(SUB):
                e = jnp.exp(jnp.where(rows >= j, c_b - c_b[j:j + 1, :], NEG_INF))
                terms.append(q_b * k_b[j:j + 1, :] * e)
            diag[c, b] = jnp.dot(jnp.concatenate(terms, axis=0).astype(BF16), ones, preferred_element_type=F32)

    cross = {c: jnp.dot(s_cross[c].astype(BF16), vb[:c * CHUNK], preferred_element_type=F32) for c in s_cross}
    sub = {cb: jnp.dot(s_sub[cb].astype(BF16), v[cb[0] * CHUNK:cb[0] * CHUNK + cb[1] * SUB].astype(BF16),
                       preferred_element_type=F32) for cb in s_sub}
    st_ref[...] = st * jnp.exp(last) + kv

    outs = []
    for c in range(n_ch):
        for b in range(n_sub):
            lo = c * CHUNK + b * SUB
            acc = o_inter[lo:lo + SUB]
            if c > 0:
                acc = acc + cross[c][b * SUB:(b + 1) * SUB]
            if b > 0:
                acc = acc + sub[c, b]
            v_b = v[lo:lo + SUB]
            for j in range(SUB):
                col = diag[c, b][j * SUB:(j + 1) * SUB]
                if dv > LANES:
                    col = jnp.concatenate([col] * (dv // LANES), axis=1)
                acc = acc + col * v_b[j:j + 1, :]
            outs.append(acc)
    return jnp.concatenate(outs, axis=0)


def _log_sigmoid(z):
    return jnp.minimum(z, 0.0) - jnp.log(1.0 + jnp.exp(-jnp.abs(z)))


def _hgrn2_body(hq_ref, hf_ref, hi_ref, hg_ref, lb_ref, nw_ref, o_ref, st_ref):
    @pl.when(pl.program_id(2) == 0)
    def _():
        st_ref[...] = jnp.zeros_like(st_ref)

    lb = lb_ref[...]
    forget = lb + (1.0 - lb) * jax.nn.sigmoid(hf_ref[0].astype(F32))
    q = _silu(hq_ref[0].astype(F32)) * (HG_DK ** -0.5)
    o = _recurrence_step(q, 1.0 - forget, hi_ref[0].astype(F32), jnp.log(forget), st_ref)
    o_ref[0] = (_rms(o, nw_ref[...]) * _silu(hg_ref[0].astype(F32))).astype(o_ref.dtype)


def _gla_body(gq_ref, gk_ref, gv_ref, gg_ref, glow_ref, gk2_ref, gb_ref, nw_ref, o_ref, st_ref):
    @pl.when(pl.program_id(2) == 0)
    def _():
        st_ref[...] = jnp.zeros_like(st_ref)

    z = jnp.dot(glow_ref[0], gk2_ref[...], precision=HIGHEST, preferred_element_type=F32) + gb_ref[...]
    la = _log_sigmoid(z) / GLA_GATE_NORM
    q = gq_ref[0].astype(F32) * (GLA_DK ** -0.5)
    o = _recurrence_step(q, gk_ref[0].astype(F32), gv_ref[0].astype(F32), la, st_ref)
    o_ref[0] = (_rms(o, nw_ref[...]) * _silu(gg_ref[0].astype(F32))).astype(o_ref.dtype)


def hgrn2_mix(proj, lb, hg_norm):
    b, s, _ = proj.shape

    def col(base):
        return pl.BlockSpec((1, REC_T, HG_DK), lambda bi, h, t: (bi, t, base + h))

    return pl.pallas_call(
        _hgrn2_body,
        grid=(b, HG_HEADS, s // REC_T),
        in_specs=[col(0), col(HG_HEADS), col(2 * HG_HEADS), col(3 * HG_HEADS),
                  pl.BlockSpec((1, HG_DK), lambda bi, h, t: (0, h)),
                  pl.BlockSpec((1, HG_DV), lambda bi, h, t: (0, 0))],
        out_specs=pl.BlockSpec((1, REC_T, HG_DV), lambda bi, h, t: (bi, t, h)),
        out_shape=jax.ShapeDtypeStruct((b, s, HG_HEADS * HG_DV), BF16),
        scratch_shapes=[pltpu.VMEM((HG_DV, HG_DK), F32)],
        compiler_params=_params("parallel", "parallel", "arbitrary"),
        name="hgrn2_mix",
    )(proj, proj, proj, proj, lb.reshape(1, HG_QK), hg_norm.reshape(1, HG_DV))


def gla_mix(proj, glow, gk2_pad, gk_bias, gla_norm):
    b, s, _ = proj.shape
    q0 = 4 * HG_QK // GLA_DK
    k0 = q0 + GLA_HEADS
    v0 = (4 * HG_QK + 2 * GLA_HEADS * GLA_DK) // GLA_DV
    g0 = v0 + GLA_HEADS
    return pl.pallas_call(
        _gla_body,
        grid=(b, GLA_HEADS, s // REC_T),
        in_specs=[pl.BlockSpec((1, REC_T, GLA_DK), lambda bi, h, t: (bi, t, q0 + h)),
                  pl.BlockSpec((1, REC_T, GLA_DK), lambda bi, h, t: (bi, t, k0 + h)),
                  pl.BlockSpec((1, REC_T, GLA_DV), lambda bi, h, t: (bi, t, v0 + h)),
                  pl.BlockSpec((1, REC_T, GLA_DV), lambda bi, h, t: (bi, t, g0 + h)),
                  pl.BlockSpec((1, REC_T, LANES), lambda bi, h, t: (bi, t, 0)),
                  pl.BlockSpec((LANES, GLA_DK), lambda bi, h, t: (0, h)),
                  pl.BlockSpec((1, GLA_DK), lambda bi, h, t: (0, h)),
                  pl.BlockSpec((1, GLA_DV), lambda bi, h, t: (0, 0))],
        out_specs=pl.BlockSpec((1, REC_T, GLA_DV), lambda bi, h, t: (bi, t, h)),
        out_shape=jax.ShapeDtypeStruct((b, s, GLA_HEADS * GLA_DV), BF16),
        scratch_shapes=[pltpu.VMEM((GLA_DV, GLA_DK), F32)],
        compiler_params=_params("parallel", "parallel", "arbitrary"),
        name="gla_mix",
    )(proj, proj, proj, proj, glow, gk2_pad, gk_bias.reshape(1, -1), gla_norm.reshape(1, GLA_DV))


def _moba_body(q_ref, k_ref, v_ref, o_ref, kmean_ref, vt_ref, bias_ref, sa_ref, sb_ref, *, n_blocks):
    qi = pl.program_id(2)
    blk, dh = MOBA_BLOCK, MOBA_DH
    heads = [slice(hh * dh, (hh + 1) * dh) for hh in range(MOBA_HP)]

    @pl.when(qi == 0)
    def _():
        for n in range(n_blocks):
            rs = slice(n * blk, (n + 1) * blk)
            for hh, cols in enumerate(heads):
                kmean_ref[hh, n:n + 1, :] = jnp.mean(k_ref[0, rs, cols].astype(F32), axis=0, keepdims=True)
                vt_ref[hh, :, rs] = v_ref[0, rs, cols].astype(F32).T.astype(BF16)

    c = (MOBA_DH ** -0.5) * LOG2_E
    own = pl.ds(pl.multiple_of(qi * blk, blk), blk)
    brow = lax.broadcasted_iota(I32, (n_blocks, blk), 0)
    past = brow < qi
    k_i = lax.broadcasted_iota(I32, (blk, blk), 0)
    q_i = lax.broadcasted_iota(I32, (blk, blk), 1)

    qs, carry0 = [], []
    for hh, cols in enumerate(heads):
        q = q_ref[0, :, cols]
        qs.append(q)
        gate = lax.dot_general(kmean_ref[hh], q.astype(F32), NT_DIMS, precision=HIGHEST,
                               preferred_element_type=F32)
        gate = jnp.where(past, gate, NEG_INF)
        rank = jnp.zeros((n_blocks, blk), I32)
        for m in range(n_blocks):
            gm = gate[m:m + 1, :]
            rank = rank + jnp.where(gm > gate, 1, jnp.where(gm == gate, jnp.where(brow > m, 1, 0), 0))
        keep = jnp.where(past, jnp.where(rank < MOBA_TOPK, 1, 0), 0)
        bias_ref[hh] = jnp.where(keep > 0, 0.0, MASK_BIAS)
        s = lax.dot_general(k_ref[0, own, cols], q, NT_DIMS, preferred_element_type=F32)
        s = jnp.where(k_i <= q_i, s, NEG_INF)
        m0 = jnp.max(s, axis=0, keepdims=True)
        p = jnp.exp2((s - m0) * c)
        l0 = jnp.sum(p, axis=0, keepdims=True)
        acc0 = jnp.dot(vt_ref[hh, :, own], p.astype(BF16), preferred_element_type=F32)
        carry0 += [m0, l0, acc0]

    n_pairs = (qi + 1) // 2

    def score_pair(pair, dst_ref):
        p = jnp.minimum(pair, n_blocks // 2 - 1)
        r01 = pl.ds(pl.multiple_of(p * 2 * blk, 2 * blk), 2 * blk)
        for hh, cols in enumerate(heads):
            dst_ref[hh] = lax.dot_general(k_ref[0, r01, cols], qs[hh], NT_DIMS, preferred_element_type=F32)

    def absorb_pair(pair, src_ref, carry):
        n0 = 2 * pair
        r01 = pl.ds(pl.multiple_of(n0 * blk, 2 * blk), 2 * blk)
        out = []
        for hh, cols in enumerate(heads):
            m_run, l_run, acc = carry[3 * hh:3 * hh + 3]
            s0 = src_ref[hh, 0:blk, :] + bias_ref[hh, pl.ds(n0, 1), :]
            s1 = src_ref[hh, blk:2 * blk, :] + bias_ref[hh, pl.ds(n0 + 1, 1), :]
            m_new = jnp.maximum(m_run, jnp.maximum(jnp.max(s0, axis=0, keepdims=True),
                                                   jnp.max(s1, axis=0, keepdims=True)))
            alpha = jnp.exp2((m_run - m_new) * c)
            p0 = jnp.exp2((s0 - m_new) * c)
            p1 = jnp.exp2((s1 - m_new) * c)
            l_new = alpha * l_run + jnp.sum(p0, axis=0, keepdims=True) + jnp.sum(p1, axis=0, keepdims=True)
            p01 = jnp.concatenate([p0.astype(BF16), p1.astype(BF16)], axis=0)
            acc = alpha * acc + jnp.dot(vt_ref[hh, :, r01], p01, preferred_element_type=F32)
            out += [m_new, l_new, acc]
        return tuple(out)

    def two_pairs(t, carry):
        score_pair(2 * t + 1, sb_ref)
        carry = absorb_pair(2 * t, sa_ref, carry)

        def second(carry):
            score_pair(2 * t + 2, sa_ref)
            return absorb_pair(2 * t + 1, sb_ref, carry)

        return lax.cond(2 * t + 1 < n_pairs, second, lambda carry: carry, carry)

    score_pair(0, sa_ref)
    fin = lax.fori_loop(0, (n_pairs + 1) // 2, two_pairs, tuple(carry0))
    for hh, cols in enumerate(heads):
        o_ref[0, :, cols] = (fin[3 * hh + 2] / fin[3 * hh + 1]).T.astype(o_ref.dtype)


def moba_attention(qkv):
    b, s, _ = qkv.shape
    n_blocks = s // MOBA_BLOCK
    hw = MOBA_HP * MOBA_DH
    hsteps = MOBA_HEADS // MOBA_HP
    return pl.pallas_call(
        functools.partial(_moba_body, n_blocks=n_blocks),
        grid=(b, hsteps, n_blocks),
        in_specs=[pl.BlockSpec((1, MOBA_BLOCK, hw), lambda bi, h, t: (bi, t, h)),
                  pl.BlockSpec((1, s, hw), lambda bi, h, t: (bi, 0, hsteps + h)),
                  pl.BlockSpec((1, s, hw), lambda bi, h, t: (bi, 0, 2 * hsteps + h))],
        out_specs=pl.BlockSpec((1, MOBA_BLOCK, hw), lambda bi, h, t: (bi, t, h)),
        out_shape=jax.ShapeDtypeStruct((b, s, D_MODEL), BF16),
        scratch_shapes=[pltpu.VMEM((MOBA_HP, n_blocks, MOBA_DH), F32),
                        pltpu.VMEM((MOBA_HP, MOBA_DH, s), BF16),
                        pltpu.VMEM((MOBA_HP, n_blocks, MOBA_BLOCK), F32),
                        pltpu.VMEM((MOBA_HP, 2 * MOBA_BLOCK, MOBA_BLOCK), F32),
                        pltpu.VMEM((MOBA_HP, 2 * MOBA_BLOCK, MOBA_BLOCK), F32)],
        compiler_params=_params("parallel", "parallel", "arbitrary"),
        name="moba_attention",
    )(qkv, qkv, qkv)


def _router_body(h_ref, g_ref, w_ref, b_ref, idx_ref, wgt_ref, cnt_ref, x3_ref, run_ref):
    step = pl.program_id(0)

    @pl.when(step == 0)
    def _():
        run_ref[...] = jnp.zeros_like(run_ref)

    tm = h_ref.shape[0]
    xn = _rms(h_ref[...], g_ref[...])
    for ch in range(ROW_CHUNKS):
        x3_ref[pl.ds(ch, tm, stride=ROW_CHUNKS), :] = xn[:, ch * LANES:(ch + 1) * LANES]
    logits = jnp.dot(xn, w_ref[...], precision=HIGHEST, preferred_element_type=F32) + b_ref[...]
    lane = lax.broadcasted_iota(I32, (tm, LANES), 1)

    def first_max(vals):
        top = jnp.max(vals, axis=-1, keepdims=True)
        where = jnp.min(jnp.where(vals == top, lane, LANES), axis=-1, keepdims=True)
        return top, where

    g_logits = jnp.where(lane < N_GROUPS, logits, NEG_INF)
    g_top, grp = first_max(g_logits)
    p_grp = 1.0 / jnp.sum(jnp.exp(g_logits - g_top), axis=-1, keepdims=True)
    lo = EXPERT_LANE0 + grp * EXPERTS_PER_GROUP
    e_logits = jnp.where((lane >= lo) & (lane < lo + EXPERTS_PER_GROUP), logits, NEG_INF)
    v0, j0 = first_max(e_logits)
    v1, j1 = first_max(jnp.where(lane == j0, NEG_INF, e_logits))
    t = jnp.exp(v1 - v0)
    w0 = p_grp / (1.0 + t)
    w1 = p_grp * t / (1.0 + t)

    hit0 = lane == j0
    hit1 = lane == j1
    member = jnp.where(hit0 | hit1, 1.0, 0.0)
    r_i = lax.broadcasted_iota(I32, (tm, tm), 0)
    c_i = lax.broadcasted_iota(I32, (tm, tm), 1)
    before = jnp.where(c_i < r_i, 1.0, 0.0).astype(BF16)
    prior = jnp.dot(before, member.astype(BF16), preferred_element_type=F32) + run_ref[...]
    rank0 = jnp.sum(jnp.where(hit0, prior, 0.0), axis=-1, keepdims=True).astype(I32)
    rank1 = jnp.sum(jnp.where(hit1, prior, 0.0), axis=-1, keepdims=True).astype(I32)
    run_ref[...] = run_ref[...] + jnp.sum(member, axis=0, keepdims=True)
    cnt_ref[...] = run_ref[...]

    idx_ref[...] = jnp.where(lane == 0, j0 - EXPERT_LANE0,
                             jnp.where(lane == 1, j1 - EXPERT_LANE0,
                                       jnp.where(lane == 2, rank0, jnp.where(lane == 3, rank1, 0))))
    wgt_ref[...] = jnp.where(lane == 0, w0, jnp.where(lane == 1, w1, 0.0))


def moe_router(h2d, g, w_cat, b_cat):
    n_tok, d = h2d.shape
    return pl.pallas_call(
        _router_body,
        grid=(n_tok // ROUTE_TM,),
        in_specs=[pl.BlockSpec((ROUTE_TM, d), lambda i: (i, 0)),
                  pl.BlockSpec((1, d), lambda i: (0, 0)),
                  pl.BlockSpec((d, LANES), lambda i: (0, 0)),
                  pl.BlockSpec((1, LANES), lambda i: (0, 0))],
        out_specs=[pl.BlockSpec((ROUTE_TM, LANES), lambda i: (i, 0)),
                   pl.BlockSpec((ROUTE_TM, LANES), lambda i: (i, 0)),
                   pl.BlockSpec((1, LANES), lambda i: (0, 0)),
                   pl.BlockSpec((ROUTE_TM * ROW_CHUNKS, LANES), lambda i: (i, 0))],
        out_shape=[jax.ShapeDtypeStruct((n_tok, LANES), I32),
                   jax.ShapeDtypeStruct((n_tok, LANES), F32),
                   jax.ShapeDtypeStruct((1, LANES), F32),
                   jax.ShapeDtypeStruct((n_tok * ROW_CHUNKS, LANES), F32)],
        scratch_shapes=[pltpu.VMEM((1, LANES), F32)],
        compiler_params=_params("arbitrary"),
        name="moe_router",
    )(h2d, g.reshape(1, d), w_cat, b_cat)


INV_UNROLL = 16
DUMP_ROWS = MOE_BM


def _expert_body(be_ref, nu_ref, dest_ref, fill_hbm, x3_hbm, w1_ref, w3_ref, w2_ref, o2_hbm,
                 inv_ref, xbuf, ybuf, gsem, ssem, isem, w1_bf, w3_bf, w2_bf, *, n_tok, n_blocks):
    i = pl.program_id(0)
    n_used = nu_ref[0]
    slot = i % 2
    dummy = 2 * n_tok

    def gather_row(block, r, to_slot):
        a = inv_ref[block * MOE_BM + r]
        tok = jnp.minimum(a >> 1, n_tok - 1)
        return pltpu.make_async_copy(x3_hbm.at[pl.ds(tok * ROW_CHUNKS, ROW_CHUNKS)],
                                     xbuf.at[to_slot, pl.ds(r * BUF_PITCH, ROW_CHUNKS)], gsem.at[to_slot])

    def scatter_row(block, r, from_slot):
        a = inv_ref[block * MOE_BM + r]
        row = jnp.where(a == dummy, n_tok + r, a >> 1)
        return pltpu.make_async_copy(ybuf.at[from_slot, pl.ds(r * BUF_PITCH, ROW_CHUNKS)],
                                     o2_hbm.at[a & 1, pl.ds(row * ROW_CHUNKS, ROW_CHUNKS)], ssem.at[from_slot])

    block_rows = MOE_BM * ROW_CHUNKS

    def gather_wait(of_slot):
        pltpu.make_async_copy(x3_hbm.at[pl.ds(0, block_rows)], xbuf.at[of_slot, pl.ds(0, block_rows)],
                              gsem.at[of_slot]).wait()

    def scatter_wait(of_slot):
        pltpu.make_async_copy(ybuf.at[of_slot, pl.ds(0, block_rows)], o2_hbm.at[0, pl.ds(0, block_rows)],
                              ssem.at[of_slot]).wait()

    @pl.when(i == 0)
    def _():
        fill = pltpu.make_async_copy(fill_hbm, inv_ref, isem)
        fill.start()
        fill.wait()

        def body(c, carry):
            base = c * INV_UNROLL
            rows = [dest_ref[base + u] for u in range(INV_UNROLL)]
            for u in range(INV_UNROLL):
                inv_ref[rows[u]] = base + u
            return carry
        lax.fori_loop(0, dest_ref.shape[0] // INV_UNROLL, body, 0)
        ybuf[...] = jnp.zeros_like(ybuf)
        for plane in range(2):
            init = pltpu.make_async_copy(ybuf.at[plane, pl.ds(0, block_rows)],
                                         o2_hbm.at[plane, pl.ds(n_tok * ROW_CHUNKS, block_rows)], isem)
            init.start()
            init.wait()
        for r in range(MOE_BM):
            gather_row(0, r, 0).start()

    @pl.when(i < n_used)
    def _():
        gather_wait(slot)

        @pl.when((i == 0) | (be_ref[i] != be_ref[jnp.maximum(i - 1, 0)]))
        def _():
            w1_bf[...] = w1_ref[...].astype(BF16)
            w3_bf[...] = w3_ref[...].astype(BF16)
            w2_bf[...] = w2_ref[...].astype(BF16)

        @pl.when(i >= 1)
        def _():
            scatter_wait(slot)

        xn = jnp.concatenate([xbuf[slot, pl.ds(ch, MOE_BM, stride=BUF_PITCH), :] for ch in range(ROW_CHUNKS)],
                             axis=1).astype(BF16)
        prev = jnp.where(i == 0, n_blocks - 1, i - 1)
        for r in range(MOE_BM):
            gather_row(i + 1, r, 1 - slot).start(priority=r % 2)
            scatter_row(prev, r, 1 - slot).start(priority=(r + 1) % 2)
        h1 = jnp.dot(xn, w1_bf[...], preferred_element_type=F32)
        h3 = jnp.dot(xn, w3_bf[...], preferred_element_type=F32)
        act = (_silu(h1) * h3).astype(BF16)
        y = jnp.dot(act, w2_bf[...], preferred_element_type=F32)
        for ch in range(ROW_CHUNKS):
            ybuf[slot, pl.ds(ch, MOE_BM, stride=BUF_PITCH), :] = y[:, ch * LANES:(ch + 1) * LANES]

    @pl.when(i == n_used)
    def _():
        for r in range(MOE_BM):
            scatter_row(i - 1, r, 1 - slot).start()
        gather_wait(slot)
        scatter_wait(slot)
        scatter_wait(1 - slot)


def moe_experts(x3, dest_flat, w1, w3, w2, layer, block_e, n_used):
    n_tok = x3.shape[0] // ROW_CHUNKS
    d = D_MODEL
    n_blocks = block_e.shape[0]
    n_rows = n_blocks * MOE_BM
    f = w1.shape[-1]

    def w_map(i, be, nu, dest):
        return (layer, be[i], 0, 0)

    grid_spec = pltpu.PrefetchScalarGridSpec(
        num_scalar_prefetch=3,
        grid=(n_blocks,),
        in_specs=[pl.BlockSpec(memory_space=pl.ANY),
                  pl.BlockSpec(memory_space=pl.ANY),
                  pl.BlockSpec((None, None, d, f), w_map),
                  pl.BlockSpec((None, None, d, f), w_map),
                  pl.BlockSpec((None, None, f, d), w_map)],
        out_specs=pl.BlockSpec(memory_space=pl.ANY),
        scratch_shapes=[pltpu.SMEM((n_rows,), I32),
                        pltpu.VMEM((2, MOE_BM * BUF_PITCH, LANES), F32),
                        pltpu.VMEM((2, MOE_BM * BUF_PITCH, LANES), F32),
                        pltpu.SemaphoreType.DMA((2,)),
                        pltpu.SemaphoreType.DMA((2,)),
                        pltpu.SemaphoreType.DMA(()),
                        pltpu.VMEM((d, f), BF16), pltpu.VMEM((d, f), BF16), pltpu.VMEM((f, d), BF16)],
    )
    return pl.pallas_call(
        functools.partial(_expert_body, n_tok=n_tok, n_blocks=n_blocks),
        grid_spec=grid_spec,
        out_shape=jax.ShapeDtypeStruct((2, (n_tok + DUMP_ROWS) * ROW_CHUNKS, LANES), F32),
        compiler_params=_params("arbitrary"),
        name="moe_experts",
    )(block_e, n_used, dest_flat, jnp.full((n_rows,), 2 * n_tok, I32), x3, w1, w3, w2)


def _combine_body(*refs, final_norm):
    if final_norm:
        h_ref, wgt_ref, o2_ref, gf_ref, o_ref = refs
    else:
        h_ref, wgt_ref, o2_ref, o_ref = refs
    wgt = wgt_ref[...]
    y0, y1 = (jnp.concatenate([o2_ref[s, pl.ds(ch, COMBINE_T, stride=ROW_CHUNKS), :] for ch in range(ROW_CHUNKS)], axis=1)
              for s in range(2))
    out = h_ref[...] + wgt[:, 0:1] * y0 + wgt[:, 1:2] * y1
    if final_norm:
        out = _rms(out, gf_ref[...])
    o_ref[...] = out


def moe_combine(h2d, wgt, o2, g_final=None):
    n_tok, d = h2d.shape
    final_norm = g_final is not None
    in_specs = [pl.BlockSpec((COMBINE_T, d), lambda i: (i, 0)),
                pl.BlockSpec((COMBINE_T, LANES), lambda i: (i, 0)),
                pl.BlockSpec((2, COMBINE_T * ROW_CHUNKS, LANES), lambda i: (0, i, 0))]
    args = [h2d, wgt, o2]
    if final_norm:
        in_specs.append(pl.BlockSpec((1, d), lambda i: (0, 0)))
        args.append(g_final.reshape(1, d))
    return pl.pallas_call(
        functools.partial(_combine_body, final_norm=final_norm),
        grid=(n_tok // COMBINE_T,),
        in_specs=in_specs,
        out_specs=pl.BlockSpec((COMBINE_T, d), lambda i: (i, 0)),
        out_shape=jax.ShapeDtypeStruct((n_tok, d), F32),
        compiler_params=_params("parallel"),
        name="moe_combine_final" if final_norm else "moe_combine",
    )(*args)


def hierarchical_moe(h2d, layer, norm_g, w_group, b_group, w_router, b_router, w1, w3, w2, g_final=None):
    n_tok, d = h2d.shape
    pad_l = LANES - N_GROUPS - N_EXPERTS
    w_cat = jnp.concatenate([w_group, w_router, jnp.zeros((d, pad_l), F32)], axis=1)
    b_cat = jnp.concatenate([b_group, b_router, jnp.zeros((pad_l,), F32)]).reshape(1, LANES)
    idx, wgt, cnt, x3 = moe_router(h2d, norm_g, w_cat, b_cat)

    counts = cnt[0, EXPERT_LANE0:EXPERT_LANE0 + N_EXPERTS].astype(I32)
    padded = (counts + MOE_BM - 1) // MOE_BM * MOE_BM
    pends = jnp.cumsum(padded)
    pstarts = pends - padded
    n_blocks = (n_tok * 2) // MOE_BM + N_EXPERTS
    blk_row0 = jnp.arange(n_blocks, dtype=I32) * MOE_BM
    block_e = jnp.minimum(jnp.sum(pends[None, :] <= blk_row0[:, None], axis=1), N_EXPERTS - 1).astype(I32)
    n_used = (pends[-1:] // MOE_BM).astype(I32)
    e_iota = jnp.arange(N_EXPERTS, dtype=I32)
    row0 = jnp.sum(jnp.where(idx[:, 0:2, None] == e_iota, pstarts, 0), axis=-1)
    dest = (row0 + idx[:, 2:4]).astype(I32).reshape(-1)

    o2 = moe_experts(x3, dest, w1, w3, w2, layer, block_e, n_used)
    return moe_combine(h2d, wgt, o2, g_final)


def kernel(x, norm_mix, norm_ffn, norm_final, w_in_even, hg_lb_logits, hg_norm, gla_gk2, gla_gk_bias,
           gla_norm, w_out_even, w_qkv_odd, w_o_odd, router_group_w, router_group_b, router_expert_w,
           router_expert_b, expert_w1, expert_w3, expert_w2):
    b, s, d = x.shape
    n_tok = b * s
    depth = norm_mix.shape[0]
    lb_table = jnp.cumsum(jax.nn.softmax(hg_lb_logits.astype(F32), axis=0), axis=0)
    h = x.reshape(n_tok, d)
    for l in range(depth):
        if l % 2 == 0:
            e = l // 2
            w_in = w_in_even[e]
            w_low = jnp.pad(w_in[:, MAIN_IN:], ((0, 0), (0, LANES - GLA_RANK))).astype(BF16)
            proj, glow = norm_matmul(h, norm_mix[l], w_in[:, :MAIN_IN].astype(BF16), w_low, tm=512, tn=1024)
            proj = proj.reshape(b, s, MAIN_IN)
            gk2_pad = jnp.pad(gla_gk2[e], ((0, LANES - GLA_RANK), (0, 0)))
            o_hg = hgrn2_mix(proj, lb_table[l], hg_norm[e])
            o_gla = gla_mix(proj, glow.reshape(b, s, LANES), gk2_pad, gla_gk_bias[e], gla_norm[e])
            mixed = jnp.concatenate([o_hg, o_gla], axis=-1).reshape(n_tok, d)
            h = matmul_residual(mixed, w_out_even[e].astype(BF16), h, tm=512)
        else:
            o = l // 2
            qkv = norm_matmul(h, norm_mix[l], w_qkv_odd[o].astype(BF16), tm=512, tn=1024)
            attn = moba_attention(qkv.reshape(b, s, 3 * d))
            h = matmul_residual(attn.reshape(n_tok, d), w_o_odd[o].astype(BF16), h, tm=512)
        h = hierarchical_moe(h, l, norm_ffn[l], router_group_w[l], router_group_b[l], router_expert_w[l],
                             router_expert_b[l], expert_w1, expert_w3, expert_w2,
                             g_final=norm_final if l == depth - 1 else None)
    return h.reshape(b, s, d)
```

```python
import functools

import jax
import jax.numpy as jnp
from jax import lax
from jax.experimental import pallas as pl
from jax.experimental.pallas import tpu as pltpu

F32 = jnp.float32
BF16 = jnp.bfloat16
I32 = jnp.int32
HIGHEST = lax.Precision.HIGHEST
NEG_INF = float("-inf")
LOG2_E = 1.4426950408889634

EPS = 1e-6
D_MODEL = 2048

HG_HEADS, HG_DK, HG_DV = 8, 128, 128
GLA_HEADS, GLA_DK, GLA_DV = 4, 128, 256
GLA_RANK = 16
GLA_GATE_NORM = 16.0
HG_QK = HG_HEADS * HG_DK
MAIN_IN = 4 * HG_QK + 2 * GLA_HEADS * GLA_DK + 2 * GLA_HEADS * GLA_DV
CHUNK = 64
SUB = 8
REC_T = 256

MOBA_HEADS, MOBA_DH = 16, 128
MOBA_BLOCK = 256
MOBA_TOPK = 3
MOBA_HP = 2
MASK_BIAS = -1e30

N_GROUPS, EXPERTS_PER_GROUP = 4, 8
N_EXPERTS = N_GROUPS * EXPERTS_PER_GROUP
D_EXPERT = D_MODEL // 4
EXPERT_LANE0 = N_GROUPS
ROUTE_TM = 256
MOE_BM = 256
COMBINE_T = 128
ROW_CHUNKS = D_MODEL // 128
BUF_PITCH = ROW_CHUNKS + 8

PROJ_TM, PROJ_TN = 1024, 1024
OUT_TM = 512

LANES = 128
VMEM_LIMIT = 56 * 1024 * 1024

NT_DIMS = (((1,), (1,)), ((), ()))
TN_DIMS = (((0,), (0,)), ((), ()))


def _params(*sem):
    return pltpu.CompilerParams(dimension_semantics=sem, vmem_limit_bytes=VMEM_LIMIT)


def _rms(x, g):
    return x * lax.rsqrt(jnp.mean(x * x, axis=-1, keepdims=True) + EPS) * g


def _silu(x):
    return x * jax.nn.sigmoid(x)


def _norm_matmul_body(*refs, has_side):
    if has_side:
        x_ref, g_ref, w_ref, ws_ref, o_ref, os_ref, xn_ref = refs
    else:
        x_ref, g_ref, w_ref, o_ref, xn_ref = refs

    @pl.when(pl.program_id(1) == 0)
    def _():
        xn_ref[...] = _rms(x_ref[...], g_ref[...]).astype(BF16)
        if has_side:
            os_ref[...] = jnp.dot(xn_ref[...], ws_ref[...], preferred_element_type=F32)

    o_ref[...] = jnp.dot(xn_ref[...], w_ref[...], preferred_element_type=F32).astype(o_ref.dtype)


def norm_matmul(x, g, w, w_side=None, *, tm, tn):
    m, k = x.shape
    n = w.shape[1]
    has_side = w_side is not None
    in_specs = [pl.BlockSpec((tm, k), lambda i, j: (i, 0)),
                pl.BlockSpec((1, k), lambda i, j: (0, 0)),
                pl.BlockSpec((k, tn), lambda i, j: (0, j))]
    out_specs = [pl.BlockSpec((tm, tn), lambda i, j: (i, j))]
    out_shape = [jax.ShapeDtypeStruct((m, n), BF16)]
    args = [x, g.reshape(1, k), w]
    if has_side:
        ns = w_side.shape[1]
        in_specs.append(pl.BlockSpec((k, ns), lambda i, j: (0, 0)))
        out_specs.append(pl.BlockSpec((tm, ns), lambda i, j: (i, 0)))
        out_shape.append(jax.ShapeDtypeStruct((m, ns), F32))
        args.append(w_side)
    outs = pl.pallas_call(
        functools.partial(_norm_matmul_body, has_side=has_side),
        grid=(m // tm, n // tn),
        in_specs=in_specs, out_specs=out_specs, out_shape=out_shape,
        scratch_shapes=[pltpu.VMEM((tm, k), BF16)],
        compiler_params=_params("parallel", "arbitrary"),
        name="norm_matmul_side" if has_side else "norm_matmul",
    )(*args)
    return outs if has_side else outs[0]


def _matmul_res_body(*refs):
    *a_refs, w_ref, r_ref, o_ref = refs
    acc = r_ref[...]
    k0 = 0
    for a_ref in a_refs:
        kp = a_ref.shape[1]
        acc = acc + jnp.dot(a_ref[...], w_ref[k0:k0 + kp, :], preferred_element_type=F32)
        k0 += kp
    o_ref[...] = acc


def matmul_residual(a_pieces, w, res, *, tm):
    m = res.shape[0]
    k, n = w.shape
    return pl.pallas_call(
        _matmul_res_body,
        grid=(m // tm,),
        in_specs=[pl.BlockSpec((tm, a.shape[1]), lambda i: (i, 0)) for a in a_pieces]
        + [pl.BlockSpec((k, n), lambda i: (0, 0)),
           pl.BlockSpec((tm, n), lambda i: (i, 0))],
        out_specs=pl.BlockSpec((tm, n), lambda i: (i, 0)),
        out_shape=jax.ShapeDtypeStruct((m, n), F32),
        compiler_params=_params("parallel"),
        name="matmul_residual",
    )(*a_pieces, w, res)


def _recurrence_step(q, k, v, la, st_ref):
    n_ch = REC_T // CHUNK
    n_sub = CHUNK // SUB
    dv = v.shape[1]
    r_i = lax.broadcasted_iota(I32, (CHUNK, CHUNK), 0)
    c_i = lax.broadcasted_iota(I32, (CHUNK, CHUNK), 1)
    tri = (c_i <= r_i).astype(F32)
    local = [jnp.dot(tri, la[c * CHUNK:(c + 1) * CHUNK], precision=HIGHEST, preferred_element_type=F32)
             for c in range(n_ch)]
    cums = [local[0]]
    for c in range(1, n_ch):
        cums.append(local[c] + cums[-1][CHUNK - 1:CHUNK, :])
    cum = jnp.concatenate(cums, axis=0)
    last = cum[REC_T - 1:REC_T, :]
    vb = v.astype(BF16)
    st = st_ref[...]

    o_inter = lax.dot_general((q * jnp.exp(cum)).astype(BF16), st.astype(BF16), NT_DIMS,
                              preferred_element_type=F32)
    kv = lax.dot_general(vb, (k * jnp.exp(last - cum)).astype(BF16), TN_DIMS, preferred_element_type=F32)
    rows = lax.broadcasted_iota(I32, (SUB, 1), 0)
    ones = jnp.ones((q.shape[1], LANES), BF16)
    s_cross, s_sub, diag = {}, {}, {}
    for c in range(n_ch):
        c0 = c * CHUNK
        if c > 0:
            ref_pt = cum[c0 - 1:c0, :]
            q_t = (q[c0:c0 + CHUNK] * jnp.exp(cum[c0:c0 + CHUNK] - ref_pt)).astype(BF16)
            k_t = (k[:c0] * jnp.exp(ref_pt - cum[:c0])).astype(BF16)
            s_cross[c] = lax.dot_general(q_t, k_t, NT_DIMS, preferred_element_type=F32)
        for b in range(n_sub):
            lo = c0 + b * SUB
            q_b, k_b, c_b = q[lo:lo + SUB], k[lo:lo + SUB], cum[lo:lo + SUB]
            if b > 0:
                ref_pt = cum[lo - 1:lo, :]
                q_t = (q_b * jnp.exp(c_b - ref_pt)).astype(BF16)
                k_t = (k[c0:lo] * jnp.exp(ref_pt - cum[c0:lo])).astype(BF16)
                s_sub[c, b] = lax.dot_general(q_t, k_t, NT_DIMS, preferred_element_type=F32)
            terms = []
            for j in range(SUB):
                e = jnp.exp(jnp.where(rows >= j, c_b - c_b[j:j + 1, :], NEG_INF))
                terms.append(q_b * k_b[j:j + 1, :] * e)
            diag[c, b] = jnp.dot(jnp.concatenate(terms, axis=0).astype(BF16), ones, preferred_element_type=F32)

    cross = {c: jnp.dot(s_cross[c].astype(BF16), vb[:c * CHUNK], preferred_element_type=F32) for c in s_cross}
    sub = {cb: jnp.dot(s_sub[cb].astype(BF16), v[cb[0] * CHUNK:cb[0] * CHUNK + cb[1] * SUB].astype(BF16),
                       preferred_element_type=F32) for cb in s_sub}
    st_ref[...] = st * jnp.exp(last) + kv

    outs = []
    for c in range(n_ch):
        for b in range(n_sub):
            lo = c * CHUNK + b * SUB
            acc = o_inter[lo:lo + SUB]
            if c > 0:
                acc = acc + cross[c][b * SUB:(b + 1) * SUB]
            if b > 0:
                acc = acc + sub[c, b]
            v_b = v[lo:lo + SUB]
            for j in range(SUB):
                col = diag[c, b][j * SUB:(j + 1) * SUB]
                if dv > LANES:
                    col = jnp.concatenate([col] * (dv // LANES), axis=1)
                acc = acc + col * v_b[j:j + 1, :]
            outs.append(acc)
    return jnp.concatenate(outs, axis=0)


def _log_sigmoid(z):
    return jnp.minimum(z, 0.0) - jnp.log(1.0 + jnp.exp(-jnp.abs(z)))


def _hgrn2_body(hq_ref, hf_ref, hi_ref, hg_ref, lb_ref, nw_ref, o_ref, st_ref):
    @pl.when(pl.program_id(2) == 0)
    def _():
        st_ref[...] = jnp.zeros_like(st_ref)

    lb = lb_ref[...]
    forget = lb + (1.0 - lb) * jax.nn.sigmoid(hf_ref[0].astype(F32))
    q = _silu(hq_ref[0].astype(F32)) * (HG_DK ** -0.5)
    o = _recurrence_step(q, 1.0 - forget, hi_ref[0].astype(F32), jnp.log(forget), st_ref)
    o_ref[0] = (_rms(o, nw_ref[...]) * _silu(hg_ref[0].astype(F32))).astype(o_ref.dtype)


def _gla_body(gq_ref, gk_ref, gv_ref, gg_ref, glow_ref, gk2_ref, gb_ref, nw_ref, o_ref, st_ref):
    @pl.when(pl.program_id(2) == 0)
    def _():
        st_ref[...] = jnp.zeros_like(st_ref)

    z = jnp.dot(glow_ref[0], gk2_ref[...], precision=HIGHEST, preferred_element_type=F32) + gb_ref[...]
    la = _log_sigmoid(z) / GLA_GATE_NORM
    q = gq_ref[0].astype(F32) * (GLA_DK ** -0.5)
    o = _recurrence_step(q, gk_ref[0].astype(F32), gv_ref[0].astype(F32), la, st_ref)
    o_ref[0] = (_rms(o, nw_ref[...]) * _silu(gg_ref[0].astype(F32))).astype(o_ref.dtype)


def hgrn2_mix(proj, lb, hg_norm):
    b, s, _ = proj.shape

    def col(base):
        return pl.BlockSpec((1, REC_T, HG_DK), lambda bi, h, t: (bi, t, base + h))

    return pl.pallas_call(
        _hgrn2_body,
        grid=(b, HG_HEADS, s // REC_T),
        in_specs=[col(0), col(HG_HEADS), col(2 * HG_HEADS), col(3 * HG_HEADS),
                  pl.BlockSpec((1, HG_DK), lambda bi, h, t: (0, h)),
                  pl.BlockSpec((1, HG_DV), lambda bi, h, t: (0, 0))],
        out_specs=pl.BlockSpec((1, REC_T, HG_DV), lambda bi, h, t: (bi, t, h)),
        out_shape=jax.ShapeDtypeStruct((b, s, HG_HEADS * HG_DV), BF16),
        scratch_shapes=[pltpu.VMEM((HG_DV, HG_DK), F32)],
        compiler_params=_params("parallel", "parallel", "arbitrary"),
        name="hgrn2_mix",
    )(proj, proj, proj, proj, lb.reshape(1, HG_QK), hg_norm.reshape(1, HG_DV))


def gla_mix(proj, glow, gk2_pad, gk_bias, gla_norm):
    b, s, _ = proj.shape
    q0 = 4 * HG_QK // GLA_DK
    k0 = q0 + GLA_HEADS
    v0 = (4 * HG_QK + 2 * GLA_HEADS * GLA_DK) // GLA_DV
    g0 = v0 + GLA_HEADS
    return pl.pallas_call(
        _gla_body,
        grid=(b, GLA_HEADS, s // REC_T),
        in_specs=[pl.BlockSpec((1, REC_T, GLA_DK), lambda bi, h, t: (bi, t, q0 + h)),
                  pl.BlockSpec((1, REC_T, GLA_DK), lambda bi, h, t: (bi, t, k0 + h)),
                  pl.BlockSpec((1, REC_T, GLA_DV), lambda bi, h, t: (bi, t, v0 + h)),
                  pl.BlockSpec((1, REC_T, GLA_DV), lambda bi, h, t: (bi, t, g0 + h)),
                  pl.BlockSpec((1, REC_T, LANES), lambda bi, h, t: (bi, t, 0)),
                  pl.BlockSpec((LANES, GLA_DK), lambda bi, h, t: (0, h)),
                  pl.BlockSpec((1, GLA_DK), lambda bi, h, t: (0, h)),
                  pl.BlockSpec((1, GLA_DV), lambda bi, h, t: (0, 0))],
        out_specs=pl.BlockSpec((1, REC_T, GLA_DV), lambda bi, h, t: (bi, t, h)),
        out_shape=jax.ShapeDtypeStruct((b, s, GLA_HEADS * GLA_DV), BF16),
        scratch_shapes=[pltpu.VMEM((GLA_DV, GLA_DK), F32)],
        compiler_params=_params("parallel", "parallel", "arbitrary"),
        name="gla_mix",
    )(proj, proj, proj, proj, glow, gk2_pad, gk_bias.reshape(1, -1), gla_norm.reshape(1, GLA_DV))


def _moba_body(q_ref, k_ref, v_ref, o_ref, kmean_ref, vt_ref, bias_ref, sa_ref, sb_ref, *, n_blocks):
    qi = pl.program_id(2)
    blk, dh = MOBA_BLOCK, MOBA_DH
    heads = [slice(hh * dh, (hh + 1) * dh) for hh in range(MOBA_HP)]

    @pl.when(qi == 0)
    def _():
        for n in range(n_blocks):
            rs = slice(n * blk, (n + 1) * blk)
            for hh, cols in enumerate(heads):
                kmean_ref[hh, n:n + 1, :] = jnp.mean(k_ref[0, rs, cols].astype(F32), axis=0, keepdims=True)
                vt_ref[hh, :, rs] = v_ref[0, rs, cols].astype(F32).T.astype(BF16)

    c = (MOBA_DH ** -0.5) * LOG2_E
    own = pl.ds(pl.multiple_of(qi * blk, blk), blk)
    brow = lax.broadcasted_iota(I32, (n_blocks, blk), 0)
    past = brow < qi
    k_i = lax.broadcasted_iota(I32, (blk, blk), 0)
    q_i = lax.broadcasted_iota(I32, (blk, blk), 1)

    qs, carry0 = [], []
    for hh, cols in enumerate(heads):
        q = q_ref[0, :, cols]
        qs.append(q)
        gate = lax.dot_general(kmean_ref[hh], q.astype(F32), NT_DIMS, precision=HIGHEST,
                               preferred_element_type=F32)
        gate = jnp.where(past, gate, NEG_INF)
        rank = jnp.zeros((n_blocks, blk), I32)
        for m in range(n_blocks):
            gm = gate[m:m + 1, :]
            rank = rank + jnp.where(gm > gate, 1, jnp.where(gm == gate, jnp.where(brow > m, 1, 0), 0))
        keep = jnp.where(past, jnp.where(rank < MOBA_TOPK, 1, 0), 0)
        bias_ref[hh] = jnp.where(keep > 0, 0.0, MASK_BIAS)
        s = lax.dot_general(k_ref[0, own, cols], q, NT_DIMS, preferred_element_type=F32)
        s = jnp.where(k_i <= q_i, s, NEG_INF)
        m0 = jnp.max(s, axis=0, keepdims=True)
        p = jnp.exp2((s - m0) * c)
        l0 = jnp.sum(p, axis=0, keepdims=True)
        acc0 = jnp.dot(vt_ref[hh, :, own], p.astype(BF16), preferred_element_type=F32)
        carry0 += [m0, l0, acc0]

    n_pairs = (qi + 1) // 2

    def score_pair(pair, dst_ref):
        p = jnp.minimum(pair, n_blocks // 2 - 1)
        r01 = pl.ds(pl.multiple_of(p * 2 * blk, 2 * blk), 2 * blk)
        for hh, cols in enumerate(heads):
            dst_ref[hh] = lax.dot_general(k_ref[0, r01, cols], qs[hh], NT_DIMS, preferred_element_type=F32)

    def absorb_pair(pair, src_ref, carry):
        n0 = 2 * pair
        r01 = pl.ds(pl.multiple_of(n0 * blk, 2 * blk), 2 * blk)
        out = []
        for hh, cols in enumerate(heads):
            m_run, l_run, acc = carry[3 * hh:3 * hh + 3]
            s0 = src_ref[hh, 0:blk, :] + bias_ref[hh, pl.ds(n0, 1), :]
            s1 = src_ref[hh, blk:2 * blk, :] + bias_ref[hh, pl.ds(n0 + 1, 1), :]
            m_new = jnp.maximum(m_run, jnp.maximum(jnp.max(s0, axis=0, keepdims=True),
                                                   jnp.max(s1, axis=0, keepdims=True)))
            alpha = jnp.exp2((m_run - m_new) * c)
            p0 = jnp.exp2((s0 - m_new) * c)
            p1 = jnp.exp2((s1 - m_new) * c)
            l_new = alpha * l_run + jnp.sum(p0, axis=0, keepdims=True) + jnp.sum(p1, axis=0, keepdims=True)
            p01 = jnp.concatenate([p0.astype(BF16), p1.astype(BF16)], axis=0)
            acc = alpha * acc + jnp.dot(vt_ref[hh, :, r01], p01, preferred_element_type=F32)
            out += [m_new, l_new, acc]
        return tuple(out)

    def two_pairs(t, carry):
        score_pair(2 * t + 1, sb_ref)
        carry = absorb_pair(2 * t, sa_ref, carry)

        def second(carry):
            score_pair(2 * t + 2, sa_ref)
            return absorb_pair(2 * t + 1, sb_ref, carry)

        return lax.cond(2 * t + 1 < n_pairs, second, lambda carry: carry, carry)

    score_pair(0, sa_ref)
    fin = lax.fori_loop(0, (n_pairs + 1) // 2, two_pairs, tuple(carry0))
    for hh, cols in enumerate(heads):
        o_ref[0, :, cols] = (fin[3 * hh + 2] / fin[3 * hh + 1]).T.astype(o_ref.dtype)


def moba_attention(qkv):
    b, s, _ = qkv.shape
    n_blocks = s // MOBA_BLOCK
    hw = MOBA_HP * MOBA_DH
    hsteps = MOBA_HEADS // MOBA_HP
    return pl.pallas_call(
        functools.partial(_moba_body, n_blocks=n_blocks),
        grid=(b, hsteps, n_blocks),
        in_specs=[pl.BlockSpec((1, MOBA_BLOCK, hw), lambda bi, h, t: (bi, t, h)),
                  pl.BlockSpec((1, s, hw), lambda bi, h, t: (bi, 0, hsteps + h)),
                  pl.BlockSpec((1, s, hw), lambda bi, h, t: (bi, 0, 2 * hsteps + h))],
        out_specs=pl.BlockSpec((1, MOBA_BLOCK, hw), lambda bi, h, t: (bi, t, h)),
        out_shape=jax.ShapeDtypeStruct((b, s, D_MODEL), BF16),
        scratch_shapes=[pltpu.VMEM((MOBA_HP, n_blocks, MOBA_DH), F32),
                        pltpu.VMEM((MOBA_HP, MOBA_DH, s), BF16),
                        pltpu.VMEM((MOBA_HP, n_blocks, MOBA_BLOCK), F32),
                        pltpu.VMEM((MOBA_HP, 2 * MOBA_BLOCK, MOBA_BLOCK), F32),
                        pltpu.VMEM((MOBA_HP, 2 * MOBA_BLOCK, MOBA_BLOCK), F32)],
        compiler_params=_params("parallel", "parallel", "arbitrary"),
        name="moba_attention",
    )(qkv, qkv, qkv)


def _router_body(h_ref, g_ref, wh_ref, wl_ref, b_ref, idx_ref, wgt_ref, cnt_ref, x3_ref, run_ref):
    step = pl.program_id(0)

    @pl.when(step == 0)
    def _():
        run_ref[...] = jnp.zeros_like(run_ref)

    tm = h_ref.shape[0]
    xn = _rms(h_ref[...], g_ref[...])
    for ch in range(ROW_CHUNKS):
        x3_ref[pl.ds(ch, tm, stride=ROW_CHUNKS), :] = xn[:, ch * LANES:(ch + 1) * LANES]
    x_hi = xn.astype(BF16)
    x_lo = (xn - x_hi.astype(F32)).astype(BF16)
    logits = (jnp.dot(x_hi, wh_ref[...], preferred_element_type=F32)
              + (jnp.dot(x_hi, wl_ref[...], preferred_element_type=F32)
                 + jnp.dot(x_lo, wh_ref[...], preferred_element_type=F32))) + b_ref[...]
    lane = lax.broadcasted_iota(I32, (tm, LANES), 1)

    def first_max(vals):
        top = jnp.max(vals, axis=-1, keepdims=True)
        where = jnp.min(jnp.where(vals == top, lane, LANES), axis=-1, keepdims=True)
        return top, where

    g_logits = jnp.where(lane < N_GROUPS, logits, NEG_INF)
    g_top, grp = first_max(g_logits)
    p_grp = 1.0 / jnp.sum(jnp.exp(g_logits - g_top), axis=-1, keepdims=True)
    lo = EXPERT_LANE0 + grp * EXPERTS_PER_GROUP
    e_logits = jnp.where((lane >= lo) & (lane < lo + EXPERTS_PER_GROUP), logits, NEG_INF)
    v0, j0 = first_max(e_logits)
    v1, j1 = first_max(jnp.where(lane == j0, NEG_INF, e_logits))
    t = jnp.exp(v1 - v0)
    w0 = p_grp / (1.0 + t)
    w1 = p_grp * t / (1.0 + t)

    hit0 = lane == j0
    hit1 = lane == j1
    member = jnp.where(hit0 | hit1, 1.0, 0.0)
    r_i = lax.broadcasted_iota(I32, (tm, tm), 0)
    c_i = lax.broadcasted_iota(I32, (tm, tm), 1)
    before = jnp.where(c_i < r_i, 1.0, 0.0).astype(BF16)
    prior = jnp.dot(before, member.astype(BF16), preferred_element_type=F32) + run_ref[...]
    rank0 = jnp.sum(jnp.where(hit0, prior, 0.0), axis=-1, keepdims=True).astype(I32)
    rank1 = jnp.sum(jnp.where(hit1, prior, 0.0), axis=-1, keepdims=True).astype(I32)
    run_ref[...] = run_ref[...] + jnp.sum(member, axis=0, keepdims=True)
    cnt_ref[...] = run_ref[...]

    idx_ref[...] = jnp.where(lane == 0, j0 - EXPERT_LANE0,
                             jnp.where(lane == 1, j1 - EXPERT_LANE0,
                                       jnp.where(lane == 2, rank0, jnp.where(lane == 3, rank1, 0))))
    wgt_ref[...] = jnp.where(lane == 0, w0, jnp.where(lane == 1, w1, 0.0))


def moe_router(h2d, g, w_cat, b_cat):
    n_tok, d = h2d.shape
    w_hi = w_cat.astype(BF16)
    return pl.pallas_call(
        _router_body,
        grid=(n_tok // ROUTE_TM,),
        in_specs=[pl.BlockSpec((ROUTE_TM, d), lambda i: (i, 0)),
                  pl.BlockSpec((1, d), lambda i: (0, 0)),
                  pl.BlockSpec((d, LANES), lambda i: (0, 0)),
                  pl.BlockSpec((d, LANES), lambda i: (0, 0)),
                  pl.BlockSpec((1, LANES), lambda i: (0, 0))],
        out_specs=[pl.BlockSpec((ROUTE_TM, LANES), lambda i: (i, 0)),
                   pl.BlockSpec((ROUTE_TM, LANES), lambda i: (i, 0)),
                   pl.BlockSpec((1, LANES), lambda i: (0, 0)),
                   pl.BlockSpec((ROUTE_TM * ROW_CHUNKS, LANES), lambda i: (i, 0))],
        out_shape=[jax.ShapeDtypeStruct((n_tok, LANES), I32),
                   jax.ShapeDtypeStruct((n_tok, LANES), F32),
                   jax.ShapeDtypeStruct((1, LANES), F32),
                   jax.ShapeDtypeStruct((n_tok * ROW_CHUNKS, LANES), F32)],
        scratch_shapes=[pltpu.VMEM((1, LANES), F32)],
        compiler_params=_params("arbitrary"),
        name="moe_router",
    )(h2d, g.reshape(1, d), w_hi, (w_cat - w_hi.astype(F32)).astype(BF16), b_cat)


INV_UNROLL = 16
DUMP_ROWS = MOE_BM


def _expert_body(be_ref, nu_ref, dest_ref, fill_hbm, x3_hbm, w1_ref, w3_ref, w2_ref, o2_hbm,
                 inv_ref, xbuf, ybuf, gsem, ssem, isem, w1_bf, w3_bf, w2_bf, *, n_tok, n_blocks):
    i = pl.program_id(0)
    n_used = nu_ref[0]
    slot = i % 2
    dummy = 2 * n_tok

    def gather_row(block, r, to_slot):
        a = inv_ref[block * MOE_BM + r]
        tok = jnp.minimum(a >> 1, n_tok - 1)
        return pltpu.make_async_copy(x3_hbm.at[pl.ds(tok * ROW_CHUNKS, ROW_CHUNKS)],
                                     xbuf.at[to_slot, pl.ds(r * BUF_PITCH, ROW_CHUNKS)], gsem.at[to_slot])

    def scatter_row(block, r, from_slot):
        a = inv_ref[block * MOE_BM + r]
        row = jnp.where(a == dummy, n_tok + r, a >> 1)
        return pltpu.make_async_copy(ybuf.at[from_slot, pl.ds(r * BUF_PITCH, ROW_CHUNKS)],
                                     o2_hbm.at[a & 1, pl.ds(row * ROW_CHUNKS, ROW_CHUNKS)], ssem.at[from_slot])

    block_rows = MOE_BM * ROW_CHUNKS

    def gather_wait(of_slot):
        pltpu.make_async_copy(x3_hbm.at[pl.ds(0, block_rows)], xbuf.at[of_slot, pl.ds(0, block_rows)],
                              gsem.at[of_slot]).wait()

    def scatter_wait(of_slot):
        pltpu.make_async_copy(ybuf.at[of_slot, pl.ds(0, block_rows)], o2_hbm.at[0, pl.ds(0, block_rows)],
                              ssem.at[of_slot]).wait()

    @pl.when(i == 0)
    def _():
        fill = pltpu.make_async_copy(fill_hbm, inv_ref, isem)
        fill.start()
        fill.wait()

        def body(c, carry):
            base = c * INV_UNROLL
            rows = [dest_ref[base + u] for u in range(INV_UNROLL)]
            for u in range(INV_UNROLL):
                inv_ref[rows[u]] = base + u
            return carry
        lax.fori_loop(0, dest_ref.shape[0] // INV_UNROLL, body, 0)
        ybuf[...] = jnp.zeros_like(ybuf)
        for plane in range(2):
            init = pltpu.make_async_copy(ybuf.at[plane, pl.ds(0, block_rows)],
                                         o2_hbm.at[plane, pl.ds(n_tok * ROW_CHUNKS, block_rows)], isem)
            init.start()
            init.wait()
        for r in range(MOE_BM):
            gather_row(0, r, 0).start()

    @pl.when(i < n_used)
    def _():
        gather_wait(slot)

        @pl.when((i == 0) | (be_ref[i] != be_ref[jnp.maximum(i - 1, 0)]))
        def _():
            w1_bf[...] = w1_ref[...].astype(BF16)
            w3_bf[...] = w3_ref[...].astype(BF16)
            w2_bf[...] = w2_ref[...].astype(BF16)

        @pl.when(i >= 1)
        def _():
            scatter_wait(slot)

        xn = jnp.concatenate([xbuf[slot, pl.ds(ch, MOE_BM, stride=BUF_PITCH), :] for ch in range(ROW_CHUNKS)],
                             axis=1).astype(BF16)
        prev = jnp.where(i == 0, n_blocks - 1, i - 1)
        for r in range(MOE_BM):
            gather_row(i + 1, r, 1 - slot).start()
            scatter_row(prev, r, 1 - slot).start()
        h1 = jnp.dot(xn, w1_bf[...], preferred_element_type=F32)
        h3 = jnp.dot(xn, w3_bf[...], preferred_element_type=F32)
        act = (_silu(h1) * h3).astype(BF16)
        y = jnp.dot(act, w2_bf[...], preferred_element_type=F32)
        for ch in range(ROW_CHUNKS):
            ybuf[slot, pl.ds(ch, MOE_BM, stride=BUF_PITCH), :] = y[:, ch * LANES:(ch + 1) * LANES]

    @pl.when(i == n_used)
    def _():
        for r in range(MOE_BM):
            scatter_row(i - 1, r, 1 - slot).start()
        gather_wait(slot)
        scatter_wait(slot)
        scatter_wait(1 - slot)


def moe_experts(x3, dest_flat, w1, w3, w2, layer, block_e, n_used):
    n_tok = x3.shape[0] // ROW_CHUNKS
    d = D_MODEL
    n_blocks = block_e.shape[0]
    n_rows = n_blocks * MOE_BM
    f = w1.shape[-1]

    def w_map(i, be, nu, dest):
        return (layer, be[i], 0, 0)

    grid_spec = pltpu.PrefetchScalarGridSpec(
        num_scalar_prefetch=3,
        grid=(n_blocks,),
        in_specs=[pl.BlockSpec(memory_space=pl.ANY),
                  pl.BlockSpec(memory_space=pl.ANY),
                  pl.BlockSpec((None, None, d, f), w_map),
                  pl.BlockSpec((None, None, d, f), w_map),
                  pl.BlockSpec((None, None, f, d), w_map)],
        out_specs=pl.BlockSpec(memory_space=pl.ANY),
        scratch_shapes=[pltpu.SMEM((n_rows,), I32),
                        pltpu.VMEM((2, MOE_BM * BUF_PITCH, LANES), F32),
                        pltpu.VMEM((2, MOE_BM * BUF_PITCH, LANES), F32),
                        pltpu.SemaphoreType.DMA((2,)),
                        pltpu.SemaphoreType.DMA((2,)),
                        pltpu.SemaphoreType.DMA(()),
                        pltpu.VMEM((d, f), BF16), pltpu.VMEM((d, f), BF16), pltpu.VMEM((f, d), BF16)],
    )
    return pl.pallas_call(
        functools.partial(_expert_body, n_tok=n_tok, n_blocks=n_blocks),
        grid_spec=grid_spec,
        out_shape=jax.ShapeDtypeStruct((2, (n_tok + DUMP_ROWS) * ROW_CHUNKS, LANES), F32),
        compiler_params=_params("arbitrary"),
        name="moe_experts",
    )(block_e, n_used, dest_flat, jnp.full((n_rows,), 2 * n_tok, I32), x3, w1, w3, w2)


def _combine_body(*refs, final_norm):
    if final_norm:
        h_ref, wgt_ref, o2_ref, gf_ref, o_ref = refs
    else:
        h_ref, wgt_ref, o2_ref, o_ref = refs
    wgt = wgt_ref[...]
    y0, y1 = (jnp.concatenate([o2_ref[s, pl.ds(ch, COMBINE_T, stride=ROW_CHUNKS), :] for ch in range(ROW_CHUNKS)], axis=1)
              for s in range(2))
    out = h_ref[...] + wgt[:, 0:1] * y0 + wgt[:, 1:2] * y1
    if final_norm:
        out = _rms(out, gf_ref[...])
    o_ref[...] = out


def moe_combine(h2d, wgt, o2, g_final=None):
    n_tok, d = h2d.shape
    final_norm = g_final is not None
    in_specs = [pl.BlockSpec((COMBINE_T, d), lambda i: (i, 0)),
                pl.BlockSpec((COMBINE_T, LANES), lambda i: (i, 0)),
                pl.BlockSpec((2, COMBINE_T * ROW_CHUNKS, LANES), lambda i: (0, i, 0))]
    args = [h2d, wgt, o2]
    if final_norm:
        in_specs.append(pl.BlockSpec((1, d), lambda i: (0, 0)))
        args.append(g_final.reshape(1, d))
    return pl.pallas_call(
        functools.partial(_combine_body, final_norm=final_norm),
        grid=(n_tok // COMBINE_T,),
        in_specs=in_specs,
        out_specs=pl.BlockSpec((COMBINE_T, d), lambda i: (i, 0)),
        out_shape=jax.ShapeDtypeStruct((n_tok, d), F32),
        compiler_params=_params("parallel"),
        name="moe_combine_final" if final_norm else "moe_combine",
    )(*args)


def hierarchical_moe(h2d, layer, norm_g, w_group, b_group, w_router, b_router, w1, w3, w2, g_final=None):
    n_tok, d = h2d.shape
    pad_l = LANES - N_GROUPS - N_EXPERTS
    w_cat = jnp.concatenate([w_group, w_router, jnp.zeros((d, pad_l), F32)], axis=1)
    b_cat = jnp.concatenate([b_group, b_router, jnp.zeros((pad_l,), F32)]).reshape(1, LANES)
    idx, wgt, cnt, x3 = moe_router(h2d, norm_g, w_cat, b_cat)

    counts = cnt[0, EXPERT_LANE0:EXPERT_LANE0 + N_EXPERTS].astype(I32)
    padded = (counts + MOE_BM - 1) // MOE_BM * MOE_BM
    pends = jnp.cumsum(padded)
    pstarts = pends - padded
    n_blocks = (n_tok * 2) // MOE_BM + N_EXPERTS
    blk_row0 = jnp.arange(n_blocks, dtype=I32) * MOE_BM
    block_e = jnp.minimum(jnp.sum(pends[None, :] <= blk_row0[:, None], axis=1), N_EXPERTS - 1).astype(I32)
    n_used = (pends[-1:] // MOE_BM).astype(I32)
    e_iota = jnp.arange(N_EXPERTS, dtype=I32)
    row0 = jnp.sum(jnp.where(idx[:, 0:2, None] == e_iota, pstarts, 0), axis=-1)
    dest = (row0 + idx[:, 2:4]).astype(I32).reshape(-1)

    o2 = moe_experts(x3, dest, w1, w3, w2, layer, block_e, n_used)
    return moe_combine(h2d, wgt, o2, g_final)


def kernel(x, norm_mix, norm_ffn, norm_final, w_in_even, hg_lb_logits, hg_norm, gla_gk2, gla_gk_bias,
           gla_norm, w_out_even, w_qkv_odd, w_o_odd, router_group_w, router_group_b, router_expert_w,
           router_expert_b, expert_w1, expert_w3, expert_w2):
    b, s, d = x.shape
    n_tok = b * s
    depth = norm_mix.shape[0]
    lb_table = jnp.cumsum(jax.nn.softmax(hg_lb_logits.astype(F32), axis=0), axis=0)
    h = x.reshape(n_tok, d)
    for l in range(depth):
        if l % 2 == 0:
            e = l // 2
            w_in = w_in_even[e]
            w_low = jnp.pad(w_in[:, MAIN_IN:], ((0, 0), (0, LANES - GLA_RANK))).astype(BF16)
            proj, glow = norm_matmul(h, norm_mix[l], w_in[:, :MAIN_IN].astype(BF16), w_low, tm=PROJ_TM, tn=PROJ_TN)
            proj = proj.reshape(b, s, MAIN_IN)
            gk2_pad = jnp.pad(gla_gk2[e], ((0, LANES - GLA_RANK), (0, 0)))
            o_hg = hgrn2_mix(proj, lb_table[l], hg_norm[e])
            o_gla = gla_mix(proj, glow.reshape(b, s, LANES), gk2_pad, gla_gk_bias[e], gla_norm[e])
            h = matmul_residual([o_hg.reshape(n_tok, -1), o_gla.reshape(n_tok, -1)], w_out_even[e].astype(BF16), h,
                                tm=OUT_TM)
        else:
            o = l // 2
            qkv = norm_matmul(h, norm_mix[l], w_qkv_odd[o].astype(BF16), tm=PROJ_TM, tn=PROJ_TN)
            attn = moba_attention(qkv.reshape(b, s, 3 * d))
            h = matmul_residual([attn.reshape(n_tok, d)], w_o_odd[o].astype(BF16), h, tm=OUT_TM)
        h = hierarchical_moe(h, l, norm_ffn[l], router_group_w[l], router_group_b[l], router_expert_w[l],
                             router_expert_b[l], expert_w1, expert_w3, expert_w2,
                             g_final=norm_final if l == depth - 1 else None)
    return h.reshape(b, s, d)
```

```python
import functools

import jax
import jax.numpy as jnp
from jax import lax
from jax.experimental import pallas as pl
from jax.experimental.pallas import tpu as pltpu

F32 = jnp.float32
BF16 = jnp.bfloat16
I32 = jnp.int32
HIGHEST = lax.Precision.HIGHEST
NEG_INF = float("-inf")
LOG2_E = 1.4426950408889634

EPS = 1e-6
D_MODEL = 2048

HG_HEADS, HG_DK, HG_DV = 8, 128, 128
GLA_HEADS, GLA_DK, GLA_DV = 4, 128, 256
GLA_RANK = 16
GLA_GATE_NORM = 16.0
HG_QK = HG_HEADS * HG_DK
MAIN_IN = 4 * HG_QK + 2 * GLA_HEADS * GLA_DK + 2 * GLA_HEADS * GLA_DV
CHUNK = 64
SUB = 8
REC_T = 256

MOBA_HEADS, MOBA_DH = 16, 128
MOBA_BLOCK = 256
MOBA_TOPK = 3
MOBA_HP = 2
MASK_BIAS = -1e30

N_GROUPS, EXPERTS_PER_GROUP = 4, 8
N_EXPERTS = N_GROUPS * EXPERTS_PER_GROUP
D_EXPERT = D_MODEL // 4
EXPERT_LANE0 = N_GROUPS
ROUTE_TM = 256
MOE_BM = 256
COMBINE_T = 128
ROW_CHUNKS = D_MODEL // 128
BUF_PITCH = ROW_CHUNKS + 8

PROJ_TM, PROJ_TN = 1024, 1024
OUT_TM = 512

LANES = 128
VMEM_LIMIT = 56 * 1024 * 1024

NT_DIMS = (((1,), (1,)), ((), ()))
TN_DIMS = (((0,), (0,)), ((), ()))


def _params(*sem):
    return pltpu.CompilerParams(dimension_semantics=sem, vmem_limit_bytes=VMEM_LIMIT)


def _rms(x, g):
    return x * lax.rsqrt(jnp.mean(x * x, axis=-1, keepdims=True) + EPS) * g


def _silu(x):
    return x * jax.nn.sigmoid(x)


def _norm_matmul_body(*refs, has_side):
    if has_side:
        x_ref, g_ref, w_ref, ws_ref, o_ref, os_ref, xn_ref = refs
    else:
        x_ref, g_ref, w_ref, o_ref, xn_ref = refs

    @pl.when(pl.program_id(1) == 0)
    def _():
        xn_ref[...] = _rms(x_ref[...], g_ref[...]).astype(BF16)
        if has_side:
            os_ref[...] = jnp.dot(xn_ref[...], ws_ref[...], preferred_element_type=F32)

    o_ref[...] = jnp.dot(xn_ref[...], w_ref[...], preferred_element_type=F32).astype(o_ref.dtype)


def norm_matmul(x, g, w, w_side=None, *, tm, tn):
    m, k = x.shape
    n = w.shape[1]
    has_side = w_side is not None
    in_specs = [pl.BlockSpec((tm, k), lambda i, j: (i, 0)),
                pl.BlockSpec((1, k), lambda i, j: (0, 0)),
                pl.BlockSpec((k, tn), lambda i, j: (0, j))]
    out_specs = [pl.BlockSpec((tm, tn), lambda i, j: (i, j))]
    out_shape = [jax.ShapeDtypeStruct((m, n), BF16)]
    args = [x, g.reshape(1, k), w]
    if has_side:
        ns = w_side.shape[1]
        in_specs.append(pl.BlockSpec((k, ns), lambda i, j: (0, 0)))
        out_specs.append(pl.BlockSpec((tm, ns), lambda i, j: (i, 0)))
        out_shape.append(jax.ShapeDtypeStruct((m, ns), F32))
        args.append(w_side)
    outs = pl.pallas_call(
        functools.partial(_norm_matmul_body, has_side=has_side),
        grid=(m // tm, n // tn),
        in_specs=in_specs, out_specs=out_specs, out_shape=out_shape,
        scratch_shapes=[pltpu.VMEM((tm, k), BF16)],
        compiler_params=_params("parallel", "arbitrary"),
        name="norm_matmul_side" if has_side else "norm_matmul",
    )(*args)
    return outs if has_side else outs[0]


def _matmul_res_body(*refs):
    *a_refs, w_ref, r_ref, o_ref = refs
    acc = r_ref[...]
    k0 = 0
    for a_ref in a_refs:
        kp = a_ref.shape[1]
        acc = acc + jnp.dot(a_ref[...], w_ref[k0:k0 + kp, :], preferred_element_type=F32)
        k0 += kp
    o_ref[...] = acc


def matmul_residual(a_pieces, w, res, *, tm):
    m = res.shape[0]
    k, n = w.shape
    return pl.pallas_call(
        _matmul_res_body,
        grid=(m // tm,),
        in_specs=[pl.BlockSpec((tm, a.shape[1]), lambda i: (i, 0)) for a in a_pieces]
        + [pl.BlockSpec((k, n), lambda i: (0, 0)),
           pl.BlockSpec((tm, n), lambda i: (i, 0))],
        out_specs=pl.BlockSpec((tm, n), lambda i: (i, 0)),
        out_shape=jax.ShapeDtypeStruct((m, n), F32),
        compiler_params=_params("parallel"),
        name="matmul_residual",
    )(*a_pieces, w, res)


def _recurrence_step(q, k, v, la, st_ref):
    n_ch = REC_T // CHUNK
    n_sub = CHUNK // SUB
    dv = v.shape[1]
    r_i = lax.broadcasted_iota(I32, (CHUNK, CHUNK), 0)
    c_i = lax.broadcasted_iota(I32, (CHUNK, CHUNK), 1)
    tri = (c_i <= r_i).astype(F32)
    local = [jnp.dot(tri, la[c * CHUNK:(c + 1) * CHUNK], precision=HIGHEST, preferred_element_type=F32)
             for c in range(n_ch)]
    cums = [local[0]]
    for c in range(1, n_ch):
        cums.append(local[c] + cums[-1][CHUNK - 1:CHUNK, :])
    cum = jnp.concatenate(cums, axis=0)
    last = cum[REC_T - 1:REC_T, :]
    vb = v.astype(BF16)
    st = st_ref[...]

    o_inter = lax.dot_general((q * jnp.exp(cum)).astype(BF16), st.astype(BF16), NT_DIMS,
                              preferred_element_type=F32)
    kv = lax.dot_general(vb, (k * jnp.exp(last - cum)).astype(BF16), TN_DIMS, preferred_element_type=F32)
    rows = lax.broadcasted_iota(I32, (SUB, 1), 0)
    ones = jnp.ones((q.shape[1], LANES), BF16)
    s_cross, s_sub, diag = {}, {}, {}
    for c in range(n_ch):
        c0 = c * CHUNK
        if c > 0:
            ref_pt = cum[c0 - 1:c0, :]
            q_t = (q[c0:c0 + CHUNK] * jnp.exp(cum[c0:c0 + CHUNK] - ref_pt)).astype(BF16)
            k_t = (k[:c0] * jnp.exp(ref_pt - cum[:c0])).astype(BF16)
            s_cross[c] = lax.dot_general(q_t, k_t, NT_DIMS, preferred_element_type=F32)
        for b in range(n_sub):
            lo = c0 + b * SUB
            q_b, k_b, c_b = q[lo:lo + SUB], k[lo:lo + SUB], cum[lo:lo + SUB]
            if b > 0:
                ref_pt = cum[lo - 1:lo, :]
                q_t = (q_b * jnp.exp(c_b - ref_pt)).astype(BF16)
                k_t = (k[c0:lo] * jnp.exp(ref_pt - cum[c0:lo])).astype(BF16)
                s_sub[c, b] = lax.dot_general(q_t, k_t, NT_DIMS, preferred_element_type=F32)
            terms = []
            for j in range(SUB):
                e = jnp.exp(jnp.where(rows >= j, c_b - c_b[j:j + 1, :], NEG_INF))
                terms.append(q_b * k_b[j:j + 1, :] * e)
            diag[c, b] = jnp.dot(jnp.concatenate(terms, axis=0).astype(BF16), ones, preferred_element_type=F32)

    cross = {c: jnp.dot(s_cross[c].astype(BF16), vb[:c * CHUNK], preferred_element_type=F32) for c in s_cross}
    sub = {cb: jnp.dot(s_sub[cb].astype(BF16), v[cb[0] * CHUNK:cb[0] * CHUNK + cb[1] * SUB].astype(BF16),
                       preferred_element_type=F32) for cb in s_sub}
    st_ref[...] = st * jnp.exp(last) + kv

    outs = []
    for c in range(n_ch):
        for b in range(n_sub):
            lo = c * CHUNK + b * SUB
            acc = o_inter[lo:lo + SUB]
            if c > 0:
                acc = acc + cross[c][b * SUB:(b + 1) * SUB]
            if b > 0:
                acc = acc + sub[c, b]
            v_b = v[lo:lo + SUB]
            for j in range(SUB):
                col = diag[c, b][j * SUB:(j + 1) * SUB]
                if dv > LANES:
                    col = jnp.concatenate([col] * (dv // LANES), axis=1)
                acc = acc + col * v_b[j:j + 1, :]
            outs.append(acc)
    return jnp.concatenate(outs, axis=0)


def _log_sigmoid(z):
    return jnp.minimum(z, 0.0) - jnp.log(1.0 + jnp.exp(-jnp.abs(z)))


def _hgrn2_body(hq_ref, hf_ref, hi_ref, hg_ref, lb_ref, nw_ref, o_ref, st_ref):
    @pl.when(pl.program_id(2) == 0)
    def _():
        st_ref[...] = jnp.zeros_like(st_ref)

    lb = lb_ref[...]
    forget = lb + (1.0 - lb) * jax.nn.sigmoid(hf_ref[0].astype(F32))
    q = _silu(hq_ref[0].astype(F32)) * (HG_DK ** -0.5)
    o = _recurrence_step(q, 1.0 - forget, hi_ref[0].astype(F32), jnp.log(forget), st_ref)
    o_ref[0] = (_rms(o, nw_ref[...]) * _silu(hg_ref[0].astype(F32))).astype(o_ref.dtype)


def _gla_body(gq_ref, gk_ref, gv_ref, gg_ref, glow_ref, gk2_ref, gb_ref, nw_ref, o_ref, st_ref):
    @pl.when(pl.program_id(2) == 0)
    def _():
        st_ref[...] = jnp.zeros_like(st_ref)

    z = jnp.dot(glow_ref[0], gk2_ref[...], precision=HIGHEST, preferred_element_type=F32) + gb_ref[...]
    la = _log_sigmoid(z) / GLA_GATE_NORM
    q = gq_ref[0].astype(F32) * (GLA_DK ** -0.5)
    o = _recurrence_step(q, gk_ref[0].astype(F32), gv_ref[0].astype(F32), la, st_ref)
    o_ref[0] = (_rms(o, nw_ref[...]) * _silu(gg_ref[0].astype(F32))).astype(o_ref.dtype)


def hgrn2_mix(proj, lb, hg_norm):
    b, s, _ = proj.shape

    def col(base):
        return pl.BlockSpec((1, REC_T, HG_DK), lambda bi, h, t: (bi, t, base + h))

    return pl.pallas_call(
        _hgrn2_body,
        grid=(b, HG_HEADS, s // REC_T),
        in_specs=[col(0), col(HG_HEADS), col(2 * HG_HEADS), col(3 * HG_HEADS),
                  pl.BlockSpec((1, HG_DK), lambda bi, h, t: (0, h)),
                  pl.BlockSpec((1, HG_DV), lambda bi, h, t: (0, 0))],
        out_specs=pl.BlockSpec((1, REC_T, HG_DV), lambda bi, h, t: (bi, t, h)),
        out_shape=jax.ShapeDtypeStruct((b, s, HG_HEADS * HG_DV), BF16),
        scratch_shapes=[pltpu.VMEM((HG_DV, HG_DK), F32)],
        compiler_params=_params("parallel", "parallel", "arbitrary"),
        name="hgrn2_mix",
    )(proj, proj, proj, proj, lb.reshape(1, HG_QK), hg_norm.reshape(1, HG_DV))


def gla_mix(proj, glow, gk2_pad, gk_bias, gla_norm):
    b, s, _ = proj.shape
    q0 = 4 * HG_QK // GLA_DK
    k0 = q0 + GLA_HEADS
    v0 = (4 * HG_QK + 2 * GLA_HEADS * GLA_DK) // GLA_DV
    g0 = v0 + GLA_HEADS
    return pl.pallas_call(
        _gla_body,
        grid=(b, GLA_HEADS, s // REC_T),
        in_specs=[pl.BlockSpec((1, REC_T, GLA_DK), lambda bi, h, t: (bi, t, q0 + h)),
                  pl.BlockSpec((1, REC_T, GLA_DK), lambda bi, h, t: (bi, t, k0 + h)),
                  pl.BlockSpec((1, REC_T, GLA_DV), lambda bi, h, t: (bi, t, v0 + h)),
                  pl.BlockSpec((1, REC_T, GLA_DV), lambda bi, h, t: (bi, t, g0 + h)),
                  pl.BlockSpec((1, REC_T, LANES), lambda bi, h, t: (bi, t, 0)),
                  pl.BlockSpec((LANES, GLA_DK), lambda bi, h, t: (0, h)),
                  pl.BlockSpec((1, GLA_DK), lambda bi, h, t: (0, h)),
                  pl.BlockSpec((1, GLA_DV), lambda bi, h, t: (0, 0))],
        out_specs=pl.BlockSpec((1, REC_T, GLA_DV), lambda bi, h, t: (bi, t, h)),
        out_shape=jax.ShapeDtypeStruct((b, s, GLA_HEADS * GLA_DV), BF16),
        scratch_shapes=[pltpu.VMEM((GLA_DV, GLA_DK), F32)],
        compiler_params=_params("parallel", "parallel", "arbitrary"),
        name="gla_mix",
    )(proj, proj, proj, proj, glow, gk2_pad, gk_bias.reshape(1, -1), gla_norm.reshape(1, GLA_DV))


def _moba_body(q_ref, k_ref, v_ref, o_ref, kmean_ref, vt_ref, bias_ref, sa_ref, sb_ref, *, n_blocks):
    qi = pl.program_id(2)
    blk, dh = MOBA_BLOCK, MOBA_DH
    heads = [slice(hh * dh, (hh + 1) * dh) for hh in range(MOBA_HP)]

    @pl.when(qi == 0)
    def _():
        for n in range(n_blocks):
            rs = slice(n * blk, (n + 1) * blk)
            for hh, cols in enumerate(heads):
                kmean_ref[hh, n:n + 1, :] = jnp.mean(k_ref[0, rs, cols].astype(F32), axis=0, keepdims=True)
                vt_ref[hh, :, rs] = v_ref[0, rs, cols].astype(F32).T.astype(BF16)

    c = (MOBA_DH ** -0.5) * LOG2_E
    own = pl.ds(pl.multiple_of(qi * blk, blk), blk)
    brow = lax.broadcasted_iota(I32, (n_blocks, blk), 0)
    past = brow < qi
    k_i = lax.broadcasted_iota(I32, (blk, blk), 0)
    q_i = lax.broadcasted_iota(I32, (blk, blk), 1)

    qs, carry0 = [], []
    for hh, cols in enumerate(heads):
        q = q_ref[0, :, cols]
        qs.append(q)
        gate = lax.dot_general(kmean_ref[hh], q.astype(F32), NT_DIMS, precision=HIGHEST,
                               preferred_element_type=F32)
        gate = jnp.where(past, gate, NEG_INF)
        rank = jnp.zeros((n_blocks, blk), I32)
        for m in range(n_blocks):
            gm = gate[m:m + 1, :]
            rank = rank + jnp.where(gm > gate, 1, jnp.where(gm == gate, jnp.where(brow > m, 1, 0), 0))
        keep = jnp.where(past, jnp.where(rank < MOBA_TOPK, 1, 0), 0)
        bias_ref[hh] = jnp.where(keep > 0, 0.0, MASK_BIAS)
        s = lax.dot_general(k_ref[0, own, cols], q, NT_DIMS, preferred_element_type=F32)
        s = jnp.where(k_i <= q_i, s, NEG_INF)
        m0 = jnp.max(s, axis=0, keepdims=True)
        p = jnp.exp2((s - m0) * c)
        l0 = jnp.sum(p, axis=0, keepdims=True)
        acc0 = jnp.dot(vt_ref[hh, :, own], p.astype(BF16), preferred_element_type=F32)
        carry0 += [m0, l0, acc0]

    n_pairs = (qi + 1) // 2

    def score_pair(pair, dst_ref):
        p = jnp.minimum(pair, n_blocks // 2 - 1)
        r01 = pl.ds(pl.multiple_of(p * 2 * blk, 2 * blk), 2 * blk)
        for hh, cols in enumerate(heads):
            dst_ref[hh] = lax.dot_general(k_ref[0, r01, cols], qs[hh], NT_DIMS, preferred_element_type=F32)

    def absorb_pair(pair, src_ref, carry):
        n0 = 2 * pair
        r01 = pl.ds(pl.multiple_of(n0 * blk, 2 * blk), 2 * blk)
        out = []
        for hh, cols in enumerate(heads):
            m_run, l_run, acc = carry[3 * hh:3 * hh + 3]
            s0 = src_ref[hh, 0:blk, :] + bias_ref[hh, pl.ds(n0, 1), :]
            s1 = src_ref[hh, blk:2 * blk, :] + bias_ref[hh, pl.ds(n0 + 1, 1), :]
            m_new = jnp.maximum(m_run, jnp.maximum(jnp.max(s0, axis=0, keepdims=True),
                                                   jnp.max(s1, axis=0, keepdims=True)))
            alpha = jnp.exp2((m_run - m_new) * c)
            p0 = jnp.exp2((s0 - m_new) * c)
            p1 = jnp.exp2((s1 - m_new) * c)
            l_new = alpha * l_run + jnp.sum(p0, axis=0, keepdims=True) + jnp.sum(p1, axis=0, keepdims=True)
            p01 = jnp.concatenate([p0.astype(BF16), p1.astype(BF16)], axis=0)
            acc = alpha * acc + jnp.dot(vt_ref[hh, :, r01], p01, preferred_element_type=F32)
            out += [m_new, l_new, acc]
        return tuple(out)

    def two_pairs(t, carry):
        score_pair(2 * t + 1, sb_ref)
        carry = absorb_pair(2 * t, sa_ref, carry)

        def second(carry):
            score_pair(2 * t + 2, sa_ref)
            return absorb_pair(2 * t + 1, sb_ref, carry)

        return lax.cond(2 * t + 1 < n_pairs, second, lambda carry: carry, carry)

    score_pair(0, sa_ref)
    fin = lax.fori_loop(0, (n_pairs + 1) // 2, two_pairs, tuple(carry0))
    for hh, cols in enumerate(heads):
        o_ref[0, :, cols] = (fin[3 * hh + 2] / fin[3 * hh + 1]).T.astype(o_ref.dtype)


def moba_attention(qkv):
    b, s, _ = qkv.shape
    n_blocks = s // MOBA_BLOCK
    hw = MOBA_HP * MOBA_DH
    hsteps = MOBA_HEADS // MOBA_HP
    return pl.pallas_call(
        functools.partial(_moba_body, n_blocks=n_blocks),
        grid=(b, hsteps, n_blocks),
        in_specs=[pl.BlockSpec((1, MOBA_BLOCK, hw), lambda bi, h, t: (bi, t, h)),
                  pl.BlockSpec((1, s, hw), lambda bi, h, t: (bi, 0, hsteps + h)),
                  pl.BlockSpec((1, s, hw), lambda bi, h, t: (bi, 0, 2 * hsteps + h))],
        out_specs=pl.BlockSpec((1, MOBA_BLOCK, hw), lambda bi, h, t: (bi, t, h)),
        out_shape=jax.ShapeDtypeStruct((b, s, D_MODEL), BF16),
        scratch_shapes=[pltpu.VMEM((MOBA_HP, n_blocks, MOBA_DH), F32),
                        pltpu.VMEM((MOBA_HP, MOBA_DH, s), BF16),
                        pltpu.VMEM((MOBA_HP, n_blocks, MOBA_BLOCK), F32),
                        pltpu.VMEM((MOBA_HP, 2 * MOBA_BLOCK, MOBA_BLOCK), F32),
                        pltpu.VMEM((MOBA_HP, 2 * MOBA_BLOCK, MOBA_BLOCK), F32)],
        compiler_params=_params("parallel", "parallel", "arbitrary"),
        name="moba_attention",
    )(qkv, qkv, qkv)


def _router_body(h_ref, g_ref, wh_ref, wl_ref, b_ref, idx_ref, wgt_ref, cnt_ref, x3_ref, run_ref):
    step = pl.program_id(0)

    @pl.when(step == 0)
    def _():
        run_ref[...] = jnp.zeros_like(run_ref)

    tm = h_ref.shape[0]
    xn = _rms(h_ref[...], g_ref[...])
    for ch in range(ROW_CHUNKS):
        x3_ref[pl.ds(ch, tm, stride=ROW_CHUNKS), :] = xn[:, ch * LANES:(ch + 1) * LANES]
    x_hi = xn.astype(BF16)
    x_lo = (xn - x_hi.astype(F32)).astype(BF16)
    logits = (jnp.dot(x_hi, wh_ref[...], preferred_element_type=F32)
              + (jnp.dot(x_hi, wl_ref[...], preferred_element_type=F32)
                 + jnp.dot(x_lo, wh_ref[...], preferred_element_type=F32))) + b_ref[...]
    lane = lax.broadcasted_iota(I32, (tm, LANES), 1)

    def first_max(vals):
        top = jnp.max(vals, axis=-1, keepdims=True)
        where = jnp.min(jnp.where(vals == top, lane, LANES), axis=-1, keepdims=True)
        return top, where

    g_logits = jnp.where(lane < N_GROUPS, logits, NEG_INF)
    g_top, grp = first_max(g_logits)
    p_grp = 1.0 / jnp.sum(jnp.exp(g_logits - g_top), axis=-1, keepdims=True)
    lo = EXPERT_LANE0 + grp * EXPERTS_PER_GROUP
    e_logits = jnp.where((lane >= lo) & (lane < lo + EXPERTS_PER_GROUP), logits, NEG_INF)
    v0, j0 = first_max(e_logits)
    v1, j1 = first_max(jnp.where(lane == j0, NEG_INF, e_logits))
    t = jnp.exp(v1 - v0)
    w0 = p_grp / (1.0 + t)
    w1 = p_grp * t / (1.0 + t)

    hit0 = lane == j0
    hit1 = lane == j1
    member = jnp.where(hit0 | hit1, 1.0, 0.0)
    r_i = lax.broadcasted_iota(I32, (tm, tm), 0)
    c_i = lax.broadcasted_iota(I32, (tm, tm), 1)
    before = jnp.where(c_i < r_i, 1.0, 0.0).astype(BF16)
    prior = jnp.dot(before, member.astype(BF16), preferred_element_type=F32) + run_ref[...]
    rank0 = jnp.sum(jnp.where(hit0, prior, 0.0), axis=-1, keepdims=True).astype(I32)
    rank1 = jnp.sum(jnp.where(hit1, prior, 0.0), axis=-1, keepdims=True).astype(I32)
    run_ref[...] = run_ref[...] + jnp.sum(member, axis=0, keepdims=True)
    cnt_ref[...] = run_ref[...]

    idx_ref[...] = jnp.where(lane == 0, j0 - EXPERT_LANE0,
                             jnp.where(lane == 1, j1 - EXPERT_LANE0,
                                       jnp.where(lane == 2, rank0, jnp.where(lane == 3, rank1, 0))))
    wgt_ref[...] = jnp.where(lane == 0, w0, jnp.where(lane == 1, w1, 0.0))


def moe_router(h2d, g, w_cat, b_cat):
    n_tok, d = h2d.shape
    w_hi = w_cat.astype(BF16)
    return pl.pallas_call(
        _router_body,
        grid=(n_tok // ROUTE_TM,),
        in_specs=[pl.BlockSpec((ROUTE_TM, d), lambda i: (i, 0)),
                  pl.BlockSpec((1, d), lambda i: (0, 0)),
                  pl.BlockSpec((d, LANES), lambda i: (0, 0)),
                  pl.BlockSpec((d, LANES), lambda i: (0, 0)),
                  pl.BlockSpec((1, LANES), lambda i: (0, 0))],
        out_specs=[pl.BlockSpec((ROUTE_TM, LANES), lambda i: (i, 0)),
                   pl.BlockSpec((ROUTE_TM, LANES), lambda i: (i, 0)),
                   pl.BlockSpec((1, LANES), lambda i: (0, 0)),
                   pl.BlockSpec((ROUTE_TM * ROW_CHUNKS, LANES), lambda i: (i, 0))],
        out_shape=[jax.ShapeDtypeStruct((n_tok, LANES), I32),
                   jax.ShapeDtypeStruct((n_tok, LANES), F32),
                   jax.ShapeDtypeStruct((1, LANES), F32),
                   jax.ShapeDtypeStruct((n_tok * ROW_CHUNKS, LANES), F32)],
        scratch_shapes=[pltpu.VMEM((1, LANES), F32)],
        compiler_params=_params("arbitrary"),
        name="moe_router",
    )(h2d, g.reshape(1, d), w_hi, (w_cat - w_hi.astype(F32)).astype(BF16), b_cat)


INV_UNROLL = 16
DUMP_ROWS = MOE_BM


def _expert_body(be_ref, nu_ref, dest_ref, gfill_hbm, sfill_hbm, x3_hbm, w1_ref, w3_ref, w2_ref, o2_hbm,
                 gsrc_ref, sdst_ref, xbuf, ybuf, gsem, ssem, isem, w1_bf, w3_bf, w2_bf, *, n_tok, n_blocks):
    i = pl.program_id(0)
    n_used = nu_ref[0]
    slot = i % 2
    plane_rows = (n_tok + DUMP_ROWS) * ROW_CHUNKS

    def gather_row(block, r, to_slot):
        src = pl.multiple_of(gsrc_ref[block * MOE_BM + r], ROW_CHUNKS)
        return pltpu.make_async_copy(x3_hbm.at[pl.ds(src, ROW_CHUNKS)],
                                     xbuf.at[to_slot, pl.ds(r * BUF_PITCH, ROW_CHUNKS)], gsem.at[to_slot])

    def scatter_row(block, r, from_slot):
        dst = pl.multiple_of(sdst_ref[block * MOE_BM + r], ROW_CHUNKS)
        return pltpu.make_async_copy(ybuf.at[from_slot, pl.ds(r * BUF_PITCH, ROW_CHUNKS)],
                                     o2_hbm.at[pl.ds(dst, ROW_CHUNKS)], ssem.at[from_slot])

    block_rows = MOE_BM * ROW_CHUNKS

    def gather_wait(of_slot):
        pltpu.make_async_copy(x3_hbm.at[pl.ds(0, block_rows)], xbuf.at[of_slot, pl.ds(0, block_rows)],
                              gsem.at[of_slot]).wait()

    def scatter_wait(of_slot):
        pltpu.make_async_copy(ybuf.at[of_slot, pl.ds(0, block_rows)], o2_hbm.at[pl.ds(0, block_rows)],
                              ssem.at[of_slot]).wait()

    @pl.when(i == 0)
    def _():
        for fill_hbm, table in ((gfill_hbm, gsrc_ref), (sfill_hbm, sdst_ref)):
            fill = pltpu.make_async_copy(fill_hbm, table, isem)
            fill.start()
            fill.wait()

        def body(c, carry):
            base = c * INV_UNROLL
            rows = [dest_ref[base + u] for u in range(INV_UNROLL)]
            tok_row0 = c * (INV_UNROLL // 2 * ROW_CHUNKS)
            for u in range(INV_UNROLL):
                gsrc_ref[rows[u]] = tok_row0 + (u // 2) * ROW_CHUNKS
                sdst_ref[rows[u]] = tok_row0 + ((u % 2) * plane_rows + (u // 2) * ROW_CHUNKS)
            return carry
        lax.fori_loop(0, dest_ref.shape[0] // INV_UNROLL, body, 0)
        ybuf[...] = jnp.zeros_like(ybuf)
        for plane in range(2):
            init = pltpu.make_async_copy(ybuf.at[plane, pl.ds(0, block_rows)],
                                         o2_hbm.at[pl.ds(plane * plane_rows + n_tok * ROW_CHUNKS, block_rows)], isem)
            init.start()
            init.wait()
        for r in range(MOE_BM):
            gather_row(0, r, 0).start()

    @pl.when(i < n_used)
    def _():
        gather_wait(slot)

        @pl.when((i == 0) | (be_ref[i] != be_ref[jnp.maximum(i - 1, 0)]))
        def _():
            w1_bf[...] = w1_ref[...].astype(BF16)
            w3_bf[...] = w3_ref[...].astype(BF16)
            w2_bf[...] = w2_ref[...].astype(BF16)

        @pl.when(i >= 1)
        def _():
            scatter_wait(slot)

        prev = jnp.where(i == 0, n_blocks - 1, i - 1)
        for r in range(MOE_BM):
            scatter_row(prev, r, 1 - slot).start()
        xn = jnp.concatenate([xbuf[slot, pl.ds(ch, MOE_BM, stride=BUF_PITCH), :] for ch in range(ROW_CHUNKS)],
                             axis=1).astype(BF16)
        for r in range(MOE_BM):
            gather_row(i + 1, r, 1 - slot).start()
        h1 = jnp.dot(xn, w1_bf[...], preferred_element_type=F32)
        h3 = jnp.dot(xn, w3_bf[...], preferred_element_type=F32)
        act = (_silu(h1) * h3).astype(BF16)
        y = jnp.dot(act, w2_bf[...], preferred_element_type=F32)
        for ch in range(ROW_CHUNKS):
            ybuf[slot, pl.ds(ch, MOE_BM, stride=BUF_PITCH), :] = y[:, ch * LANES:(ch + 1) * LANES]

    @pl.when(i == n_used)
    def _():
        for r in range(MOE_BM):
            scatter_row(i - 1, r, 1 - slot).start()
        gather_wait(slot)
        scatter_wait(slot)
        scatter_wait(1 - slot)


def moe_experts(x3, dest_flat, w1, w3, w2, layer, block_e, n_used):
    plane_rows = (x3.shape[0] // ROW_CHUNKS + DUMP_ROWS) * ROW_CHUNKS
    n_tok = x3.shape[0] // ROW_CHUNKS
    d = D_MODEL
    n_blocks = block_e.shape[0]
    n_rows = n_blocks * MOE_BM
    f = w1.shape[-1]

    def w_map(i, be, nu, dest):
        return (layer, be[i], 0, 0)

    grid_spec = pltpu.PrefetchScalarGridSpec(
        num_scalar_prefetch=3,
        grid=(n_blocks,),
        in_specs=[pl.BlockSpec(memory_space=pl.ANY),
                  pl.BlockSpec(memory_space=pl.ANY),
                  pl.BlockSpec(memory_space=pl.ANY),
                  pl.BlockSpec((None, None, d, f), w_map),
                  pl.BlockSpec((None, None, d, f), w_map),
                  pl.BlockSpec((None, None, f, d), w_map)],
        out_specs=pl.BlockSpec(memory_space=pl.ANY),
        scratch_shapes=[pltpu.SMEM((n_rows,), I32),
                        pltpu.SMEM((n_rows,), I32),
                        pltpu.VMEM((2, MOE_BM * BUF_PITCH, LANES), F32),
                        pltpu.VMEM((2, MOE_BM * BUF_PITCH, LANES), F32),
                        pltpu.SemaphoreType.DMA((2,)),
                        pltpu.SemaphoreType.DMA((2,)),
                        pltpu.SemaphoreType.DMA(()),
                        pltpu.VMEM((d, f), BF16), pltpu.VMEM((d, f), BF16), pltpu.VMEM((f, d), BF16)],
    )
    pad_row = jnp.arange(n_rows, dtype=I32) % MOE_BM
    o2 = pl.pallas_call(
        functools.partial(_expert_body, n_tok=n_tok, n_blocks=n_blocks),
        grid_spec=grid_spec,
        out_shape=jax.ShapeDtypeStruct((2 * plane_rows, LANES), F32),
        compiler_params=_params("arbitrary"),
        name="moe_experts",
    )(block_e, n_used, dest_flat, jnp.zeros((n_rows,), I32), (n_tok + pad_row) * ROW_CHUNKS, x3, w1, w3, w2)
    return o2.reshape(2, plane_rows, LANES)


def _combine_body(*refs, final_norm):
    if final_norm:
        h_ref, wgt_ref, o2_ref, gf_ref, o_ref = refs
    else:
        h_ref, wgt_ref, o2_ref, o_ref = refs
    wgt = wgt_ref[...]
    y0, y1 = (jnp.concatenate([o2_ref[s, pl.ds(ch, COMBINE_T, stride=ROW_CHUNKS), :] for ch in range(ROW_CHUNKS)], axis=1)
              for s in range(2))
    out = h_ref[...] + wgt[:, 0:1] * y0 + wgt[:, 1:2] * y1
    if final_norm:
        out = _rms(out, gf_ref[...])
    o_ref[...] = out


def moe_combine(h2d, wgt, o2, g_final=None):
    n_tok, d = h2d.shape
    final_norm = g_final is not None
    in_specs = [pl.BlockSpec((COMBINE_T, d), lambda i: (i, 0)),
                pl.BlockSpec((COMBINE_T, LANES), lambda i: (i, 0)),
                pl.BlockSpec((2, COMBINE_T * ROW_CHUNKS, LANES), lambda i: (0, i, 0))]
    args = [h2d, wgt, o2]
    if final_norm:
        in_specs.append(pl.BlockSpec((1, d), lambda i: (0, 0)))
        args.append(g_final.reshape(1, d))
    return pl.pallas_call(
        functools.partial(_combine_body, final_norm=final_norm),
        grid=(n_tok // COMBINE_T,),
        in_specs=in_specs,
        out_specs=pl.BlockSpec((COMBINE_T, d), lambda i: (i, 0)),
        out_shape=jax.ShapeDtypeStruct((n_tok, d), F32),
        compiler_params=_params("parallel"),
        name="moe_combine_final" if final_norm else "moe_combine",
    )(*args)


def hierarchical_moe(h2d, layer, norm_g, w_group, b_group, w_router, b_router, w1, w3, w2, g_final=None):
    n_tok, d = h2d.shape
    pad_l = LANES - N_GROUPS - N_EXPERTS
    w_cat = jnp.concatenate([w_group, w_router, jnp.zeros((d, pad_l), F32)], axis=1)
    b_cat = jnp.concatenate([b_group, b_router, jnp.zeros((pad_l,), F32)]).reshape(1, LANES)
    idx, wgt, cnt, x3 = moe_router(h2d, norm_g, w_cat, b_cat)

    counts = cnt[0, EXPERT_LANE0:EXPERT_LANE0 + N_EXPERTS].astype(I32)
    padded = (counts + MOE_BM - 1) // MOE_BM * MOE_BM
    pends = jnp.cumsum(padded)
    pstarts = pends - padded
    n_blocks = (n_tok * 2) // MOE_BM + N_EXPERTS
    blk_row0 = jnp.arange(n_blocks, dtype=I32) * MOE_BM
    block_e = jnp.minimum(jnp.sum(pends[None, :] <= blk_row0[:, None], axis=1), N_EXPERTS - 1).astype(I32)
    n_used = (pends[-1:] // MOE_BM).astype(I32)
    e_iota = jnp.arange(N_EXPERTS, dtype=I32)
    row0 = jnp.sum(jnp.where(idx[:, 0:2, None] == e_iota, pstarts, 0), axis=-1)
    dest = (row0 + idx[:, 2:4]).astype(I32).reshape(-1)

    o2 = moe_experts(x3, dest, w1, w3, w2, layer, block_e, n_used)
    return moe_combine(h2d, wgt, o2, g_final)


def kernel(x, norm_mix, norm_ffn, norm_final, w_in_even, hg_lb_logits, hg_norm, gla_gk2, gla_gk_bias,
           gla_norm, w_out_even, w_qkv_odd, w_o_odd, router_group_w, router_group_b, router_expert_w,
           router_expert_b, expert_w1, expert_w3, expert_w2):
    b, s, d = x.shape
    n_tok = b * s
    depth = norm_mix.shape[0]
    lb_table = jnp.cumsum(jax.nn.softmax(hg_lb_logits.astype(F32), axis=0), axis=0)
    h = x.reshape(n_tok, d)
    for l in range(depth):
        if l % 2 == 0:
            e = l // 2
            w_in = w_in_even[e]
            w_low = jnp.pad(w_in[:, MAIN_IN:], ((0, 0), (0, LANES - GLA_RANK))).astype(BF16)
            proj, glow = norm_matmul(h, norm_mix[l], w_in[:, :MAIN_IN].astype(BF16), w_low, tm=PROJ_TM, tn=PROJ_TN)
            proj = proj.reshape(b, s, MAIN_IN)
            gk2_pad = jnp.pad(gla_gk2[e], ((0, LANES - GLA_RANK), (0, 0)))
            o_hg = hgrn2_mix(proj, lb_table[l], hg_norm[e])
            o_gla = gla_mix(proj, glow.reshape(b, s, LANES), gk2_pad, gla_gk_bias[e], gla_norm[e])
            h = matmul_residual([o_hg.reshape(n_tok, -1), o_gla.reshape(n_tok, -1)], w_out_even[e].astype(BF16), h,
                                tm=OUT_TM)
        else:
            o = l // 2
            qkv = norm_matmul(h, norm_mix[l], w_qkv_odd[o].astype(BF16), tm=PROJ_TM, tn=PROJ_TN)
            attn = moba_attention(qkv.reshape(b, s, 3 * d))
            h = matmul_residual([attn.reshape(n_tok, d)], w_o_odd[o].astype(BF16), h, tm=OUT_TM)
        h = hierarchical_moe(h, l, norm_ffn[l], router_group_w[l], router_group_b[l], router_expert_w[l],
                             router_expert_b[l], expert_w1, expert_w3, expert_w2,
                             g_final=norm_final if l == depth - 1 else None)
    return h.reshape(b, s, d)
```

```python
import functools

import jax
import jax.numpy as jnp
from jax import lax
from jax.experimental import pallas as pl
from jax.experimental.pallas import tpu as pltpu

F32 = jnp.float32
BF16 = jnp.bfloat16
I32 = jnp.int32
HIGHEST = lax.Precision.HIGHEST
NEG_INF = float("-inf")
LOG2_E = 1.4426950408889634

EPS = 1e-6
D_MODEL = 2048

HG_HEADS, HG_DK, HG_DV = 8, 128, 128
GLA_HEADS, GLA_DK, GLA_DV = 4, 128, 256
GLA_RANK = 16
GLA_GATE_NORM = 16.0
HG_QK = HG_HEADS * HG_DK
MAIN_IN = 4 * HG_QK + 2 * GLA_HEADS * GLA_DK + 2 * GLA_HEADS * GLA_DV
CHUNK = 64
SUB = 8
REC_T = 256
REC_HP = 4

MOBA_HEADS, MOBA_DH = 16, 128
MOBA_BLOCK = 256
MOBA_TOPK = 3
MOBA_HP = 2
MASK_BIAS = -1e30

N_GROUPS, EXPERTS_PER_GROUP = 4, 8
N_EXPERTS = N_GROUPS * EXPERTS_PER_GROUP
D_EXPERT = D_MODEL // 4
EXPERT_LANE0 = N_GROUPS
ROUTE_TM = 256
MOE_BM = 256
COMBINE_T = 128
ROW_CHUNKS = D_MODEL // 128
BUF_PITCH = ROW_CHUNKS + 8

PROJ_TM, PROJ_TN = 1024, 1024
OUT_TM = 512

LANES = 128
VMEM_LIMIT = 56 * 1024 * 1024

NT_DIMS = (((1,), (1,)), ((), ()))
TN_DIMS = (((0,), (0,)), ((), ()))


def _params(*sem):
    return pltpu.CompilerParams(dimension_semantics=sem, vmem_limit_bytes=VMEM_LIMIT)


def _rms(x, g):
    return x * lax.rsqrt(jnp.mean(x * x, axis=-1, keepdims=True) + EPS) * g


def _silu(x):
    return x * jax.nn.sigmoid(x)


def _norm_matmul_body(*refs, has_side):
    if has_side:
        x_ref, g_ref, w_ref, ws_ref, o_ref, os_ref, xn_ref = refs
    else:
        x_ref, g_ref, w_ref, o_ref, xn_ref = refs

    @pl.when(pl.program_id(1) == 0)
    def _():
        xn_ref[...] = _rms(x_ref[...], g_ref[...]).astype(BF16)
        if has_side:
            os_ref[...] = jnp.dot(xn_ref[...], ws_ref[...], preferred_element_type=F32)

    o_ref[...] = jnp.dot(xn_ref[...], w_ref[...], preferred_element_type=F32).astype(o_ref.dtype)


def norm_matmul(x, g, w, w_side=None, *, tm, tn):
    m, k = x.shape
    n = w.shape[1]
    has_side = w_side is not None
    in_specs = [pl.BlockSpec((tm, k), lambda i, j: (i, 0)),
                pl.BlockSpec((1, k), lambda i, j: (0, 0)),
                pl.BlockSpec((k, tn), lambda i, j: (0, j))]
    out_specs = [pl.BlockSpec((tm, tn), lambda i, j: (i, j))]
    out_shape = [jax.ShapeDtypeStruct((m, n), BF16)]
    args = [x, g.reshape(1, k), w]
    if has_side:
        ns = w_side.shape[1]
        in_specs.append(pl.BlockSpec((k, ns), lambda i, j: (0, 0)))
        out_specs.append(pl.BlockSpec((tm, ns), lambda i, j: (i, 0)))
        out_shape.append(jax.ShapeDtypeStruct((m, ns), F32))
        args.append(w_side)
    outs = pl.pallas_call(
        functools.partial(_norm_matmul_body, has_side=has_side),
        grid=(m // tm, n // tn),
        in_specs=in_specs, out_specs=out_specs, out_shape=out_shape,
        scratch_shapes=[pltpu.VMEM((tm, k), BF16)],
        compiler_params=_params("parallel", "arbitrary"),
        name="norm_matmul_side" if has_side else "norm_matmul",
    )(*args)
    return outs if has_side else outs[0]


def _matmul_res_body(*refs):
    *a_refs, w_ref, r_ref, o_ref = refs
    acc = r_ref[...]
    k0 = 0
    for a_ref in a_refs:
        kp = a_ref.shape[1]
        acc = acc + jnp.dot(a_ref[...], w_ref[k0:k0 + kp, :], preferred_element_type=F32)
        k0 += kp
    o_ref[...] = acc


def matmul_residual(a_pieces, w, res, *, tm):
    m = res.shape[0]
    k, n = w.shape
    return pl.pallas_call(
        _matmul_res_body,
        grid=(m // tm,),
        in_specs=[pl.BlockSpec((tm, a.shape[1]), lambda i: (i, 0)) for a in a_pieces]
        + [pl.BlockSpec((k, n), lambda i: (0, 0)),
           pl.BlockSpec((tm, n), lambda i: (i, 0))],
        out_specs=pl.BlockSpec((tm, n), lambda i: (i, 0)),
        out_shape=jax.ShapeDtypeStruct((m, n), F32),
        compiler_params=_params("parallel"),
        name="matmul_residual",
    )(*a_pieces, w, res)


def _recurrence_levels(q, k, v, la, st_ref):
    n_ch = REC_T // CHUNK
    n_sub = CHUNK // SUB
    dv = v.shape[1]
    r_i = lax.broadcasted_iota(I32, (CHUNK, CHUNK), 0)
    c_i = lax.broadcasted_iota(I32, (CHUNK, CHUNK), 1)
    tri = (c_i <= r_i).astype(F32)
    local = [jnp.dot(tri, la[c * CHUNK:(c + 1) * CHUNK], precision=HIGHEST, preferred_element_type=F32)
             for c in range(n_ch)]
    cums = [local[0]]
    for c in range(1, n_ch):
        cums.append(local[c] + cums[-1][CHUNK - 1:CHUNK, :])
    cum = jnp.concatenate(cums, axis=0)
    last = cum[REC_T - 1:REC_T, :]
    vb = v.astype(BF16)
    st = st_ref[...]
    yield

    o_inter = lax.dot_general((q * jnp.exp(cum)).astype(BF16), st.astype(BF16), NT_DIMS,
                              preferred_element_type=F32)
    kv = lax.dot_general(vb, (k * jnp.exp(last - cum)).astype(BF16), TN_DIMS, preferred_element_type=F32)
    rows = lax.broadcasted_iota(I32, (SUB, 1), 0)
    ones = jnp.ones((q.shape[1], LANES), BF16)
    s_cross, s_sub, diag = {}, {}, {}
    for c in range(n_ch):
        c0 = c * CHUNK
        if c > 0:
            ref_pt = cum[c0 - 1:c0, :]
            q_t = (q[c0:c0 + CHUNK] * jnp.exp(cum[c0:c0 + CHUNK] - ref_pt)).astype(BF16)
            k_t = (k[:c0] * jnp.exp(ref_pt - cum[:c0])).astype(BF16)
            s_cross[c] = lax.dot_general(q_t, k_t, NT_DIMS, preferred_element_type=F32)
        for b in range(n_sub):
            lo = c0 + b * SUB
            q_b, k_b, c_b = q[lo:lo + SUB], k[lo:lo + SUB], cum[lo:lo + SUB]
            if b > 0:
                ref_pt = cum[lo - 1:lo, :]
                q_t = (q_b * jnp.exp(c_b - ref_pt)).astype(BF16)
                k_t = (k[c0:lo] * jnp.exp(ref_pt - cum[c0:lo])).astype(BF16)
                s_sub[c, b] = lax.dot_general(q_t, k_t, NT_DIMS, preferred_element_type=F32)
            terms = []
            for j in range(SUB):
                e = jnp.exp(jnp.where(rows >= j, c_b - c_b[j:j + 1, :], NEG_INF))
                terms.append(q_b * k_b[j:j + 1, :] * e)
            diag[c, b] = jnp.dot(jnp.concatenate(terms, axis=0).astype(BF16), ones, preferred_element_type=F32)

    yield
    cross = {c: jnp.dot(s_cross[c].astype(BF16), vb[:c * CHUNK], preferred_element_type=F32) for c in s_cross}
    sub = {cb: jnp.dot(s_sub[cb].astype(BF16), v[cb[0] * CHUNK:cb[0] * CHUNK + cb[1] * SUB].astype(BF16),
                       preferred_element_type=F32) for cb in s_sub}
    st_ref[...] = st * jnp.exp(last) + kv
    yield

    outs = []
    for c in range(n_ch):
        for b in range(n_sub):
            lo = c * CHUNK + b * SUB
            acc = o_inter[lo:lo + SUB]
            if c > 0:
                acc = acc + cross[c][b * SUB:(b + 1) * SUB]
            if b > 0:
                acc = acc + sub[c, b]
            v_b = v[lo:lo + SUB]
            for j in range(SUB):
                col = diag[c, b][j * SUB:(j + 1) * SUB]
                if dv > LANES:
                    col = jnp.concatenate([col] * (dv // LANES), axis=1)
                acc = acc + col * v_b[j:j + 1, :]
            outs.append(acc)
    return jnp.concatenate(outs, axis=0)


def _run_heads(steps):
    results = [None] * len(steps)
    live = list(range(len(steps)))
    while live:
        for n in list(live):
            try:
                next(steps[n])
            except StopIteration as done:
                results[n] = done.value
                live.remove(n)
    return results


def _log_sigmoid(z):
    return jnp.minimum(z, 0.0) - jnp.log(1.0 + jnp.exp(-jnp.abs(z)))


def _hgrn2_body(hq_ref, hf_ref, hi_ref, hg_ref, lb_ref, nw_ref, o_ref, st_ref):
    @pl.when(pl.program_id(2) == 0)
    def _():
        st_ref[...] = jnp.zeros_like(st_ref)

    steps = []
    for hh in range(REC_HP):
        cols = slice(hh * HG_DK, (hh + 1) * HG_DK)
        lb = lb_ref[:, cols]
        forget = lb + (1.0 - lb) * jax.nn.sigmoid(hf_ref[0, :, cols].astype(F32))
        q = _silu(hq_ref[0, :, cols].astype(F32)) * (HG_DK ** -0.5)
        steps.append(_recurrence_levels(q, 1.0 - forget, hi_ref[0, :, cols].astype(F32), jnp.log(forget),
                                        st_ref.at[hh]))
    for hh, o in enumerate(_run_heads(steps)):
        cols = slice(hh * HG_DV, (hh + 1) * HG_DV)
        o_ref[0, :, cols] = (_rms(o, nw_ref[...]) * _silu(hg_ref[0, :, cols].astype(F32))).astype(o_ref.dtype)


def _gla_body(gq_ref, gk_ref, gv_ref, gg_ref, glow_ref, gk2_ref, gb_ref, nw_ref, o_ref, st_ref):
    @pl.when(pl.program_id(2) == 0)
    def _():
        st_ref[...] = jnp.zeros_like(st_ref)

    z = jnp.dot(glow_ref[0], gk2_ref[...], precision=HIGHEST, preferred_element_type=F32) + gb_ref[...]
    la = _log_sigmoid(z) / GLA_GATE_NORM
    steps = []
    for hh in range(REC_HP):
        kc = slice(hh * GLA_DK, (hh + 1) * GLA_DK)
        vc = slice(hh * GLA_DV, (hh + 1) * GLA_DV)
        q = gq_ref[0, :, kc].astype(F32) * (GLA_DK ** -0.5)
        steps.append(_recurrence_levels(q, gk_ref[0, :, kc].astype(F32), gv_ref[0, :, vc].astype(F32), la[:, kc],
                                        st_ref.at[hh]))
    for hh, o in enumerate(_run_heads(steps)):
        vc = slice(hh * GLA_DV, (hh + 1) * GLA_DV)
        o_ref[0, :, vc] = (_rms(o, nw_ref[...]) * _silu(gg_ref[0, :, vc].astype(F32))).astype(o_ref.dtype)


def hgrn2_mix(proj, lb, hg_norm):
    b, s, _ = proj.shape

    hsteps = HG_HEADS // REC_HP

    def col(base):
        return pl.BlockSpec((1, REC_T, REC_HP * HG_DK), lambda bi, h, t: (bi, t, base + h))

    return pl.pallas_call(
        _hgrn2_body,
        grid=(b, hsteps, s // REC_T),
        in_specs=[col(0), col(hsteps), col(2 * hsteps), col(3 * hsteps),
                  pl.BlockSpec((1, REC_HP * HG_DK), lambda bi, h, t: (0, h)),
                  pl.BlockSpec((1, HG_DV), lambda bi, h, t: (0, 0))],
        out_specs=pl.BlockSpec((1, REC_T, REC_HP * HG_DV), lambda bi, h, t: (bi, t, h)),
        out_shape=jax.ShapeDtypeStruct((b, s, HG_HEADS * HG_DV), BF16),
        scratch_shapes=[pltpu.VMEM((REC_HP, HG_DV, HG_DK), F32)],
        compiler_params=_params("parallel", "parallel", "arbitrary"),
        name="hgrn2_mix",
    )(proj, proj, proj, proj, lb.reshape(1, HG_QK), hg_norm.reshape(1, HG_DV))


def gla_mix(proj, glow, gk2_pad, gk_bias, gla_norm):
    b, s, _ = proj.shape
    hsteps = GLA_HEADS // REC_HP
    kw, vw = REC_HP * GLA_DK, REC_HP * GLA_DV
    q0 = 4 * HG_QK // kw
    k0 = q0 + hsteps
    v0 = (4 * HG_QK + 2 * GLA_HEADS * GLA_DK) // vw
    g0 = v0 + hsteps
    return pl.pallas_call(
        _gla_body,
        grid=(b, hsteps, s // REC_T),
        in_specs=[pl.BlockSpec((1, REC_T, kw), lambda bi, h, t: (bi, t, q0 + h)),
                  pl.BlockSpec((1, REC_T, kw), lambda bi, h, t: (bi, t, k0 + h)),
                  pl.BlockSpec((1, REC_T, vw), lambda bi, h, t: (bi, t, v0 + h)),
                  pl.BlockSpec((1, REC_T, vw), lambda bi, h, t: (bi, t, g0 + h)),
                  pl.BlockSpec((1, REC_T, LANES), lambda bi, h, t: (bi, t, 0)),
                  pl.BlockSpec((LANES, kw), lambda bi, h, t: (0, h)),
                  pl.BlockSpec((1, kw), lambda bi, h, t: (0, h)),
                  pl.BlockSpec((1, GLA_DV), lambda bi, h, t: (0, 0))],
        out_specs=pl.BlockSpec((1, REC_T, vw), lambda bi, h, t: (bi, t, h)),
        out_shape=jax.ShapeDtypeStruct((b, s, GLA_HEADS * GLA_DV), BF16),
        scratch_shapes=[pltpu.VMEM((REC_HP, GLA_DV, GLA_DK), F32)],
        compiler_params=_params("parallel", "parallel", "arbitrary"),
        name="gla_mix",
    )(proj, proj, proj, proj, glow, gk2_pad, gk_bias.reshape(1, -1), gla_norm.reshape(1, GLA_DV))


def _moba_body(q_ref, k_ref, v_ref, o_ref, kmean_ref, vt_ref, bias_ref, sa_ref, sb_ref, *, n_blocks):
    qi = pl.program_id(2)
    blk, dh = MOBA_BLOCK, MOBA_DH
    heads = [slice(hh * dh, (hh + 1) * dh) for hh in range(MOBA_HP)]

    @pl.when(qi == 0)
    def _():
        for n in range(n_blocks):
            rs = slice(n * blk, (n + 1) * blk)
            for hh, cols in enumerate(heads):
                kmean_ref[hh, n:n + 1, :] = jnp.mean(k_ref[0, rs, cols].astype(F32), axis=0, keepdims=True)
                vt_ref[hh, :, rs] = v_ref[0, rs, cols].astype(F32).T.astype(BF16)

    c = (MOBA_DH ** -0.5) * LOG2_E
    own = pl.ds(pl.multiple_of(qi * blk, blk), blk)
    brow = lax.broadcasted_iota(I32, (n_blocks, blk), 0)
    past = brow < qi
    k_i = lax.broadcasted_iota(I32, (blk, blk), 0)
    q_i = lax.broadcasted_iota(I32, (blk, blk), 1)

    qs, carry0 = [], []
    for hh, cols in enumerate(heads):
        q = q_ref[0, :, cols]
        qs.append(q)
        gate = lax.dot_general(kmean_ref[hh], q.astype(F32), NT_DIMS, precision=HIGHEST,
                               preferred_element_type=F32)
        gate = jnp.where(past, gate, NEG_INF)
        rank = jnp.zeros((n_blocks, blk), I32)
        for m in range(n_blocks):
            gm = gate[m:m + 1, :]
            rank = rank + jnp.where(gm > gate, 1, jnp.where(gm == gate, jnp.where(brow > m, 1, 0), 0))
        keep = jnp.where(past, jnp.where(rank < MOBA_TOPK, 1, 0), 0)
        bias_ref[hh] = jnp.where(keep > 0, 0.0, MASK_BIAS)
        s = lax.dot_general(k_ref[0, own, cols], q, NT_DIMS, preferred_element_type=F32)
        s = jnp.where(k_i <= q_i, s, NEG_INF)
        m0 = jnp.max(s, axis=0, keepdims=True)
        p = jnp.exp2((s - m0) * c)
        l0 = jnp.sum(p, axis=0, keepdims=True)
        acc0 = jnp.dot(vt_ref[hh, :, own], p.astype(BF16), preferred_element_type=F32)
        carry0 += [m0, l0, acc0]

    n_pairs = (qi + 1) // 2

    def score_pair(pair, dst_ref):
        p = jnp.minimum(pair, n_blocks // 2 - 1)
        r01 = pl.ds(pl.multiple_of(p * 2 * blk, 2 * blk), 2 * blk)
        for hh, cols in enumerate(heads):
            dst_ref[hh] = lax.dot_general(k_ref[0, r01, cols], qs[hh], NT_DIMS, preferred_element_type=F32)

    def absorb_pair(pair, src_ref, carry):
        n0 = 2 * pair
        r01 = pl.ds(pl.multiple_of(n0 * blk, 2 * blk), 2 * blk)
        out = []
        for hh, cols in enumerate(heads):
            m_run, l_run, acc = carry[3 * hh:3 * hh + 3]
            s0 = src_ref[hh, 0:blk, :] + bias_ref[hh, pl.ds(n0, 1), :]
            s1 = src_ref[hh, blk:2 * blk, :] + bias_ref[hh, pl.ds(n0 + 1, 1), :]
            m_new = jnp.maximum(m_run, jnp.maximum(jnp.max(s0, axis=0, keepdims=True),
                                                   jnp.max(s1, axis=0, keepdims=True)))
            alpha = jnp.exp2((m_run - m_new) * c)
            p0 = jnp.exp2((s0 - m_new) * c)
            p1 = jnp.exp2((s1 - m_new) * c)
            l_new = alpha * l_run + jnp.sum(p0, axis=0, keepdims=True) + jnp.sum(p1, axis=0, keepdims=True)
            p01 = jnp.concatenate([p0.astype(BF16), p1.astype(BF16)], axis=0)
            acc = alpha * acc + jnp.dot(vt_ref[hh, :, r01], p01, preferred_element_type=F32)
            out += [m_new, l_new, acc]
        return tuple(out)

    def two_pairs(t, carry):
        score_pair(2 * t + 1, sb_ref)
        carry = absorb_pair(2 * t, sa_ref, carry)

        def second(carry):
            score_pair(2 * t + 2, sa_ref)
            return absorb_pair(2 * t + 1, sb_ref, carry)

        return lax.cond(2 * t + 1 < n_pairs, second, lambda carry: carry, carry)

    score_pair(0, sa_ref)
    fin = lax.fori_loop(0, (n_pairs + 1) // 2, two_pairs, tuple(carry0))
    for hh, cols in enumerate(heads):
        o_ref[0, :, cols] = (fin[3 * hh + 2] / fin[3 * hh + 1]).T.astype(o_ref.dtype)


def moba_attention(qkv):
    b, s, _ = qkv.shape
    n_blocks = s // MOBA_BLOCK
    hw = MOBA_HP * MOBA_DH
    hsteps = MOBA_HEADS // MOBA_HP
    return pl.pallas_call(
        functools.partial(_moba_body, n_blocks=n_blocks),
        grid=(b, hsteps, n_blocks),
        in_specs=[pl.BlockSpec((1, MOBA_BLOCK, hw), lambda bi, h, t: (bi, t, h)),
                  pl.BlockSpec((1, s, hw), lambda bi, h, t: (bi, 0, hsteps + h)),
                  pl.BlockSpec((1, s, hw), lambda bi, h, t: (bi, 0, 2 * hsteps + h))],
        out_specs=pl.BlockSpec((1, MOBA_BLOCK, hw), lambda bi, h, t: (bi, t, h)),
        out_shape=jax.ShapeDtypeStruct((b, s, D_MODEL), BF16),
        scratch_shapes=[pltpu.VMEM((MOBA_HP, n_blocks, MOBA_DH), F32),
                        pltpu.VMEM((MOBA_HP, MOBA_DH, s), BF16),
                        pltpu.VMEM((MOBA_HP, n_blocks, MOBA_BLOCK), F32),
                        pltpu.VMEM((MOBA_HP, 2 * MOBA_BLOCK, MOBA_BLOCK), F32),
                        pltpu.VMEM((MOBA_HP, 2 * MOBA_BLOCK, MOBA_BLOCK), F32)],
        compiler_params=_params("parallel", "parallel", "arbitrary"),
        name="moba_attention",
    )(qkv, qkv, qkv)


def _router_body(h_ref, g_ref, wh_ref, wl_ref, b_ref, idx_ref, wgt_ref, cnt_ref, x3_ref, run_ref):
    step = pl.program_id(0)

    @pl.when(step == 0)
    def _():
        run_ref[...] = jnp.zeros_like(run_ref)

    tm = h_ref.shape[0]
    xn = _rms(h_ref[...], g_ref[...])
    for ch in range(ROW_CHUNKS):
        x3_ref[pl.ds(ch, tm, stride=ROW_CHUNKS), :] = xn[:, ch * LANES:(ch + 1) * LANES]
    x_hi = xn.astype(BF16)
    x_lo = (xn - x_hi.astype(F32)).astype(BF16)
    logits = (jnp.dot(x_hi, wh_ref[...], preferred_element_type=F32)
              + (jnp.dot(x_hi, wl_ref[...], preferred_element_type=F32)
                 + jnp.dot(x_lo, wh_ref[...], preferred_element_type=F32))) + b_ref[...]
    lane = lax.broadcasted_iota(I32, (tm, LANES), 1)

    def first_max(vals):
        top = jnp.max(vals, axis=-1, keepdims=True)
        where = jnp.min(jnp.where(vals == top, lane, LANES), axis=-1, keepdims=True)
        return top, where

    g_logits = jnp.where(lane < N_GROUPS, logits, NEG_INF)
    g_top, grp = first_max(g_logits)
    p_grp = 1.0 / jnp.sum(jnp.exp(g_logits - g_top), axis=-1, keepdims=True)
    lo = EXPERT_LANE0 + grp * EXPERTS_PER_GROUP
    e_logits = jnp.where((lane >= lo) & (lane < lo + EXPERTS_PER_GROUP), logits, NEG_INF)
    v0, j0 = first_max(e_logits)
    v1, j1 = first_max(jnp.where(lane == j0, NEG_INF, e_logits))
    t = jnp.exp(v1 - v0)
    w0 = p_grp / (1.0 + t)
    w1 = p_grp * t / (1.0 + t)

    hit0 = lane == j0
    hit1 = lane == j1
    member = jnp.where(hit0 | hit1, 1.0, 0.0)
    r_i = lax.broadcasted_iota(I32, (tm, tm), 0)
    c_i = lax.broadcasted_iota(I32, (tm, tm), 1)
    before = jnp.where(c_i < r_i, 1.0, 0.0).astype(BF16)
    prior = jnp.dot(before, member.astype(BF16), preferred_element_type=F32) + run_ref[...]
    rank0 = jnp.sum(jnp.where(hit0, prior, 0.0), axis=-1, keepdims=True).astype(I32)
    rank1 = jnp.sum(jnp.where(hit1, prior, 0.0), axis=-1, keepdims=True).astype(I32)
    run_ref[...] = run_ref[...] + jnp.sum(member, axis=0, keepdims=True)
    cnt_ref[...] = run_ref[...]

    idx_ref[...] = jnp.where(lane == 0, j0 - EXPERT_LANE0,
                             jnp.where(lane == 1, j1 - EXPERT_LANE0,
                                       jnp.where(lane == 2, rank0, jnp.where(lane == 3, rank1, 0))))
    wgt_ref[...] = jnp.where(lane == 0, w0, jnp.where(lane == 1, w1, 0.0))


def moe_router(h2d, g, w_cat, b_cat):
    n_tok, d = h2d.shape
    w_hi = w_cat.astype(BF16)
    return pl.pallas_call(
        _router_body,
        grid=(n_tok // ROUTE_TM,),
        in_specs=[pl.BlockSpec((ROUTE_TM, d), lambda i: (i, 0)),
                  pl.BlockSpec((1, d), lambda i: (0, 0)),
                  pl.BlockSpec((d, LANES), lambda i: (0, 0)),
                  pl.BlockSpec((d, LANES), lambda i: (0, 0)),
                  pl.BlockSpec((1, LANES), lambda i: (0, 0))],
        out_specs=[pl.BlockSpec((ROUTE_TM, LANES), lambda i: (i, 0)),
                   pl.BlockSpec((ROUTE_TM, LANES), lambda i: (i, 0)),
                   pl.BlockSpec((1, LANES), lambda i: (0, 0)),
                   pl.BlockSpec((ROUTE_TM * ROW_CHUNKS, LANES), lambda i: (i, 0))],
        out_shape=[jax.ShapeDtypeStruct((n_tok, LANES), I32),
                   jax.ShapeDtypeStruct((n_tok, LANES), F32),
                   jax.ShapeDtypeStruct((1, LANES), F32),
                   jax.ShapeDtypeStruct((n_tok * ROW_CHUNKS, LANES), F32)],
        scratch_shapes=[pltpu.VMEM((1, LANES), F32)],
        compiler_params=_params("arbitrary"),
        name="moe_router",
    )(h2d, g.reshape(1, d), w_hi, (w_cat - w_hi.astype(F32)).astype(BF16), b_cat)


INV_UNROLL = 16
DUMP_ROWS = MOE_BM


def _expert_body(be_ref, nu_ref, dest_ref, gfill_hbm, sfill_hbm, x3_hbm, w1_ref, w3_ref, w2_ref, o2_hbm,
                 gsrc_ref, sdst_ref, xbuf, ybuf, gsem, ssem, isem, w1_bf, w3_bf, w2_bf, *, n_tok, n_blocks):
    i = pl.program_id(0)
    n_used = nu_ref[0]
    slot = i % 2
    plane_rows = (n_tok + DUMP_ROWS) * ROW_CHUNKS

    def gather_row(block, r, to_slot):
        src = pl.multiple_of(gsrc_ref[block * MOE_BM + r], ROW_CHUNKS)
        return pltpu.make_async_copy(x3_hbm.at[pl.ds(src, ROW_CHUNKS)],
                                     xbuf.at[to_slot, pl.ds(r * BUF_PITCH, ROW_CHUNKS)], gsem.at[to_slot])

    def scatter_row(block, r, from_slot):
        dst = pl.multiple_of(sdst_ref[block * MOE_BM + r], ROW_CHUNKS)
        return pltpu.make_async_copy(ybuf.at[from_slot, pl.ds(r * BUF_PITCH, ROW_CHUNKS)],
                                     o2_hbm.at[pl.ds(dst, ROW_CHUNKS)], ssem.at[from_slot])

    block_rows = MOE_BM * ROW_CHUNKS

    def gather_wait(of_slot):
        pltpu.make_async_copy(x3_hbm.at[pl.ds(0, block_rows)], xbuf.at[of_slot, pl.ds(0, block_rows)],
                              gsem.at[of_slot]).wait()

    def scatter_wait(of_slot):
        pltpu.make_async_copy(ybuf.at[of_slot, pl.ds(0, block_rows)], o2_hbm.at[pl.ds(0, block_rows)],
                              ssem.at[of_slot]).wait()

    @pl.when(i == 0)
    def _():
        for fill_hbm, table in ((gfill_hbm, gsrc_ref), (sfill_hbm, sdst_ref)):
            fill = pltpu.make_async_copy(fill_hbm, table, isem)
            fill.start()
            fill.wait()

        def body(c, carry):
            base = c * INV_UNROLL
            rows = [dest_ref[base + u] for u in range(INV_UNROLL)]
            tok_row0 = c * (INV_UNROLL // 2 * ROW_CHUNKS)
            for u in range(INV_UNROLL):
                gsrc_ref[rows[u]] = tok_row0 + (u // 2) * ROW_CHUNKS
                sdst_ref[rows[u]] = tok_row0 + ((u % 2) * plane_rows + (u // 2) * ROW_CHUNKS)
            return carry
        lax.fori_loop(0, dest_ref.shape[0] // INV_UNROLL, body, 0)
        ybuf[...] = jnp.zeros_like(ybuf)
        for plane in range(2):
            init = pltpu.make_async_copy(ybuf.at[plane, pl.ds(0, block_rows)],
                                         o2_hbm.at[pl.ds(plane * plane_rows + n_tok * ROW_CHUNKS, block_rows)], isem)
            init.start()
            init.wait()
        for r in range(MOE_BM):
            gather_row(0, r, 0).start()

    @pl.when(i < n_used)
    def _():
        gather_wait(slot)

        @pl.when((i == 0) | (be_ref[i] != be_ref[jnp.maximum(i - 1, 0)]))
        def _():
            w1_bf[...] = w1_ref[...].astype(BF16)
            w3_bf[...] = w3_ref[...].astype(BF16)
            w2_bf[...] = w2_ref[...].astype(BF16)

        @pl.when(i >= 1)
        def _():
            scatter_wait(slot)

        prev = jnp.where(i == 0, n_blocks - 1, i - 1)
        for r in range(MOE_BM):
            scatter_row(prev, r, 1 - slot).start()
        xn = jnp.concatenate([xbuf[slot, pl.ds(ch, MOE_BM, stride=BUF_PITCH), :] for ch in range(ROW_CHUNKS)],
                             axis=1).astype(BF16)
        for r in range(MOE_BM):
            gather_row(i + 1, r, 1 - slot).start()
        h1 = jnp.dot(xn, w1_bf[...], preferred_element_type=F32)
        h3 = jnp.dot(xn, w3_bf[...], preferred_element_type=F32)
        act = (_silu(h1) * h3).astype(BF16)
        y = jnp.dot(act, w2_bf[...], preferred_element_type=F32)
        for ch in range(ROW_CHUNKS):
            ybuf[slot, pl.ds(ch, MOE_BM, stride=BUF_PITCH), :] = y[:, ch * LANES:(ch + 1) * LANES]

    @pl.when(i == n_used)
    def _():
        for r in range(MOE_BM):
            scatter_row(i - 1, r, 1 - slot).start()
        gather_wait(slot)
        scatter_wait(slot)
        scatter_wait(1 - slot)


def moe_experts(x3, dest_flat, w1, w3, w2, layer, block_e, n_used):
    plane_rows = (x3.shape[0] // ROW_CHUNKS + DUMP_ROWS) * ROW_CHUNKS
    n_tok = x3.shape[0] // ROW_CHUNKS
    d = D_MODEL
    n_blocks = block_e.shape[0]
    n_rows = n_blocks * MOE_BM
    f = w1.shape[-1]

    def w_map(i, be, nu, dest):
        return (layer, be[i], 0, 0)

    grid_spec = pltpu.PrefetchScalarGridSpec(
        num_scalar_prefetch=3,
        grid=(n_blocks,),
        in_specs=[pl.BlockSpec(memory_space=pl.ANY),
                  pl.BlockSpec(memory_space=pl.ANY),
                  pl.BlockSpec(memory_space=pl.ANY),
                  pl.BlockSpec((None, None, d, f), w_map),
                  pl.BlockSpec((None, None, d, f), w_map),
                  pl.BlockSpec((None, None, f, d), w_map)],
        out_specs=pl.BlockSpec(memory_space=pl.ANY),
        scratch_shapes=[pltpu.SMEM((n_rows,), I32),
                        pltpu.SMEM((n_rows,), I32),
                        pltpu.VMEM((2, MOE_BM * BUF_PITCH, LANES), F32),
                        pltpu.VMEM((2, MOE_BM * BUF_PITCH, LANES), F32),
                        pltpu.SemaphoreType.DMA((2,)),
                        pltpu.SemaphoreType.DMA((2,)),
                        pltpu.SemaphoreType.DMA(()),
                        pltpu.VMEM((d, f), BF16), pltpu.VMEM((d, f), BF16), pltpu.VMEM((f, d), BF16)],
    )
    pad_row = jnp.arange(n_rows, dtype=I32) % MOE_BM
    o2 = pl.pallas_call(
        functools.partial(_expert_body, n_tok=n_tok, n_blocks=n_blocks),
        grid_spec=grid_spec,
        out_shape=jax.ShapeDtypeStruct((2 * plane_rows, LANES), F32),
        compiler_params=_params("arbitrary"),
        name="moe_experts",
    )(block_e, n_used, dest_flat, jnp.zeros((n_rows,), I32), (n_tok + pad_row) * ROW_CHUNKS, x3, w1, w3, w2)
    return o2.reshape(2, plane_rows, LANES)


def _combine_body(*refs, final_norm):
    if final_norm:
        h_ref, wgt_ref, o2_ref, gf_ref, o_ref = refs
    else:
        h_ref, wgt_ref, o2_ref, o_ref = refs
    wgt = wgt_ref[...]
    y0, y1 = (jnp.concatenate([o2_ref[s, pl.ds(ch, COMBINE_T, stride=ROW_CHUNKS), :] for ch in range(ROW_CHUNKS)], axis=1)
              for s in range(2))
    out = h_ref[...] + wgt[:, 0:1] * y0 + wgt[:, 1:2] * y1
    if final_norm:
        out = _rms(out, gf_ref[...])
    o_ref[...] = out


def moe_combine(h2d, wgt, o2, g_final=None):
    n_tok, d = h2d.shape
    final_norm = g_final is not None
    in_specs = [pl.BlockSpec((COMBINE_T, d), lambda i: (i, 0)),
                pl.BlockSpec((COMBINE_T, LANES), lambda i: (i, 0)),
                pl.BlockSpec((2, COMBINE_T * ROW_CHUNKS, LANES), lambda i: (0, i, 0))]
    args = [h2d, wgt, o2]
    if final_norm:
        in_specs.append(pl.BlockSpec((1, d), lambda i: (0, 0)))
        args.append(g_final.reshape(1, d))
    return pl.pallas_call(
        functools.partial(_combine_body, final_norm=final_norm),
        grid=(n_tok // COMBINE_T,),
        in_specs=in_specs,
        out_specs=pl.BlockSpec((COMBINE_T, d), lambda i: (i, 0)),
        out_shape=jax.ShapeDtypeStruct((n_tok, d), F32),
        compiler_params=_params("parallel"),
        name="moe_combine_final" if final_norm else "moe_combine",
    )(*args)


def hierarchical_moe(h2d, layer, norm_g, w_group, b_group, w_router, b_router, w1, w3, w2, g_final=None):
    n_tok, d = h2d.shape
    pad_l = LANES - N_GROUPS - N_EXPERTS
    w_cat = jnp.concatenate([w_group, w_router, jnp.zeros((d, pad_l), F32)], axis=1)
    b_cat = jnp.concatenate([b_group, b_router, jnp.zeros((pad_l,), F32)]).reshape(1, LANES)
    idx, wgt, cnt, x3 = moe_router(h2d, norm_g, w_cat, b_cat)

    counts = cnt[0, EXPERT_LANE0:EXPERT_LANE0 + N_EXPERTS].astype(I32)
    padded = (counts + MOE_BM - 1) // MOE_BM * MOE_BM
    pends = jnp.cumsum(padded)
    pstarts = pends - padded
    n_blocks = (n_tok * 2) // MOE_BM + N_EXPERTS
    blk_row0 = jnp.arange(n_blocks, dtype=I32) * MOE_BM
    block_e = jnp.minimum(jnp.sum(pends[None, :] <= blk_row0[:, None], axis=1), N_EXPERTS - 1).astype(I32)
    n_used = (pends[-1:] // MOE_BM).astype(I32)
    e_iota = jnp.arange(N_EXPERTS, dtype=I32)
    row0 = jnp.sum(jnp.where(idx[:, 0:2, None] == e_iota, pstarts, 0), axis=-1)
    dest = (row0 + idx[:, 2:4]).astype(I32).reshape(-1)

    o2 = moe_experts(x3, dest, w1, w3, w2, layer, block_e, n_used)
    return moe_combine(h2d, wgt, o2, g_final)


def kernel(x, norm_mix, norm_ffn, norm_final, w_in_even, hg_lb_logits, hg_norm, gla_gk2, gla_gk_bias,
           gla_norm, w_out_even, w_qkv_odd, w_o_odd, router_group_w, router_group_b, router_expert_w,
           router_expert_b, expert_w1, expert_w3, expert_w2):
    b, s, d = x.shape
    n_tok = b * s
    depth = norm_mix.shape[0]
    lb_table = jnp.cumsum(jax.nn.softmax(hg_lb_logits.astype(F32), axis=0), axis=0)
    h = x.reshape(n_tok, d)
    for l in range(depth):
        if l % 2 == 0:
            e = l // 2
            w_in = w_in_even[e]
            w_low = jnp.pad(w_in[:, MAIN_IN:], ((0, 0), (0, LANES - GLA_RANK))).astype(BF16)
            proj, glow = norm_matmul(h, norm_mix[l], w_in[:, :MAIN_IN].astype(BF16), w_low, tm=PROJ_TM, tn=PROJ_TN)
            proj = proj.reshape(b, s, MAIN_IN)
            gk2_pad = jnp.pad(gla_gk2[e], ((0, LANES - GLA_RANK), (0, 0)))
            o_hg = hgrn2_mix(proj, lb_table[l], hg_norm[e])
            o_gla = gla_mix(proj, glow.reshape(b, s, LANES), gk2_pad, gla_gk_bias[e], gla_norm[e])
            h = matmul_residual([o_hg.reshape(n_tok, -1), o_gla.reshape(n_tok, -1)], w_out_even[e].astype(BF16), h,
                                tm=OUT_TM)
        else:
            o = l // 2
            qkv = norm_matmul(h, norm_mix[l], w_qkv_odd[o].astype(BF16), tm=PROJ_TM, tn=PROJ_TN)
            attn = moba_attention(qkv.reshape(b, s, 3 * d))
            h = matmul_residual([attn.reshape(n_tok, d)], w_o_odd[o].astype(BF16), h, tm=OUT_TM)
        h = hierarchical_moe(h, l, norm_ffn[l], router_group_w[l], router_group_b[l], router_expert_w[l],
                             router_expert_b[l], expert_w1, expert_w3, expert_w2,
                             g_final=norm_final if l == depth - 1 else None)
    return h.reshape(b, s, d)
```

```python
import functools

import jax
import jax.numpy as jnp
from jax import lax
from jax.experimental import pallas as pl
from jax.experimental.pallas import tpu as pltpu

F32 = jnp.float32
BF16 = jnp.bfloat16
I32 = jnp.int32
HIGHEST = lax.Precision.HIGHEST
NEG_INF = float("-inf")
LOG2_E = 1.4426950408889634

EPS = 1e-6
D_MODEL = 2048

HG_HEADS, HG_DK, HG_DV = 8, 128, 128
GLA_HEADS, GLA_DK, GLA_DV = 4, 128, 256
GLA_RANK = 16
GLA_GATE_NORM = 16.0
HG_QK = HG_HEADS * HG_DK
MAIN_IN = 4 * HG_QK + 2 * GLA_HEADS * GLA_DK + 2 * GLA_HEADS * GLA_DV
CHUNK = 64
SUB = 8
REC_T = 256
REC_HP = 4

MOBA_HEADS, MOBA_DH = 16, 128
MOBA_BLOCK = 256
MOBA_TOPK = 3
MOBA_HP = 4
MASK_BIAS = -1e30

N_GROUPS, EXPERTS_PER_GROUP = 4, 8
N_EXPERTS = N_GROUPS * EXPERTS_PER_GROUP
D_EXPERT = D_MODEL // 4
EXPERT_LANE0 = N_GROUPS
ROUTE_TM = 256
MOE_BM = 256
COMBINE_T = 256
ROW_CHUNKS = D_MODEL // 128
BUF_PITCH = ROW_CHUNKS + 8

PROJ_TM, PROJ_TN = 1024, 1024
OUT_TM = 512

LANES = 128
VMEM_LIMIT = 56 * 1024 * 1024

NT_DIMS = (((1,), (1,)), ((), ()))
TN_DIMS = (((0,), (0,)), ((), ()))


def _params(*sem):
    return pltpu.CompilerParams(dimension_semantics=sem, vmem_limit_bytes=VMEM_LIMIT)


def _rms(x, g):
    return x * lax.rsqrt(jnp.mean(x * x, axis=-1, keepdims=True) + EPS) * g


def _silu(x):
    return x * jax.nn.sigmoid(x)


def _norm_matmul_body(*refs, has_side):
    if has_side:
        x_ref, g_ref, w_ref, ws_ref, o_ref, os_ref, xn_ref = refs
    else:
        x_ref, g_ref, w_ref, o_ref, xn_ref = refs

    @pl.when(pl.program_id(1) == 0)
    def _():
        xn_ref[...] = _rms(x_ref[...], g_ref[...]).astype(BF16)
        if has_side:
            os_ref[...] = jnp.dot(xn_ref[...], ws_ref[...], preferred_element_type=F32)

    o_ref[...] = jnp.dot(xn_ref[...], w_ref[...], preferred_element_type=F32).astype(o_ref.dtype)


def norm_matmul(x, g, w, w_side=None, *, tm, tn):
    m, k = x.shape
    n = w.shape[1]
    has_side = w_side is not None
    in_specs = [pl.BlockSpec((tm, k), lambda i, j: (i, 0)),
                pl.BlockSpec((1, k), lambda i, j: (0, 0)),
                pl.BlockSpec((k, tn), lambda i, j: (0, j))]
    out_specs = [pl.BlockSpec((tm, tn), lambda i, j: (i, j))]
    out_shape = [jax.ShapeDtypeStruct((m, n), BF16)]
    args = [x, g.reshape(1, k), w]
    if has_side:
        ns = w_side.shape[1]
        in_specs.append(pl.BlockSpec((k, ns), lambda i, j: (0, 0)))
        out_specs.append(pl.BlockSpec((tm, ns), lambda i, j: (i, 0)))
        out_shape.append(jax.ShapeDtypeStruct((m, ns), F32))
        args.append(w_side)
    outs = pl.pallas_call(
        functools.partial(_norm_matmul_body, has_side=has_side),
        grid=(m // tm, n // tn),
        in_specs=in_specs, out_specs=out_specs, out_shape=out_shape,
        scratch_shapes=[pltpu.VMEM((tm, k), BF16)],
        compiler_params=_params("parallel", "arbitrary"),
        name="norm_matmul_side" if has_side else "norm_matmul",
    )(*args)
    return outs if has_side else outs[0]


def _matmul_res_body(*refs):
    *a_refs, w_ref, r_ref, o_ref = refs
    acc = r_ref[...]
    k0 = 0
    for a_ref in a_refs:
        kp = a_ref.shape[1]
        acc = acc + jnp.dot(a_ref[...], w_ref[k0:k0 + kp, :], preferred_element_type=F32)
        k0 += kp
    o_ref[...] = acc


def matmul_residual(a_pieces, w, res, *, tm):
    m = res.shape[0]
    k, n = w.shape
    return pl.pallas_call(
        _matmul_res_body,
        grid=(m // tm,),
        in_specs=[pl.BlockSpec((tm, a.shape[1]), lambda i: (i, 0)) for a in a_pieces]
        + [pl.BlockSpec((k, n), lambda i: (0, 0)),
           pl.BlockSpec((tm, n), lambda i: (i, 0))],
        out_specs=pl.BlockSpec((tm, n), lambda i: (i, 0)),
        out_shape=jax.ShapeDtypeStruct((m, n), F32),
        compiler_params=_params("parallel"),
        name="matmul_residual",
    )(*a_pieces, w, res)


def _recurrence_levels(q, k, v, la, st_ref):
    n_ch = REC_T // CHUNK
    n_sub = CHUNK // SUB
    dv = v.shape[1]
    r_i = lax.broadcasted_iota(I32, (CHUNK, CHUNK), 0)
    c_i = lax.broadcasted_iota(I32, (CHUNK, CHUNK), 1)
    tri = (c_i <= r_i).astype(F32)
    local = [jnp.dot(tri, la[c * CHUNK:(c + 1) * CHUNK], precision=HIGHEST, preferred_element_type=F32)
             for c in range(n_ch)]
    cums = [local[0]]
    for c in range(1, n_ch):
        cums.append(local[c] + cums[-1][CHUNK - 1:CHUNK, :])
    cum = jnp.concatenate(cums, axis=0)
    last = cum[REC_T - 1:REC_T, :]
    vb = v.astype(BF16)
    st = st_ref[...]
    yield

    o_inter = lax.dot_general((q * jnp.exp(cum)).astype(BF16), st.astype(BF16), NT_DIMS,
                              preferred_element_type=F32)
    kv = lax.dot_general(vb, (k * jnp.exp(last - cum)).astype(BF16), TN_DIMS, preferred_element_type=F32)
    rows = lax.broadcasted_iota(I32, (SUB, 1), 0)
    ones = jnp.ones((q.shape[1], LANES), BF16)
    s_cross, s_sub, diag = {}, {}, {}
    for c in range(n_ch):
        c0 = c * CHUNK
        if c > 0:
            ref_pt = cum[c0 - 1:c0, :]
            q_t = (q[c0:c0 + CHUNK] * jnp.exp(cum[c0:c0 + CHUNK] - ref_pt)).astype(BF16)
            k_t = (k[:c0] * jnp.exp(ref_pt - cum[:c0])).astype(BF16)
            s_cross[c] = lax.dot_general(q_t, k_t, NT_DIMS, preferred_element_type=F32)
        for b in range(n_sub):
            lo = c0 + b * SUB
            q_b, k_b, c_b = q[lo:lo + SUB], k[lo:lo + SUB], cum[lo:lo + SUB]
            if b > 0:
                ref_pt = cum[lo - 1:lo, :]
                q_t = (q_b * jnp.exp(c_b - ref_pt)).astype(BF16)
                k_t = (k[c0:lo] * jnp.exp(ref_pt - cum[c0:lo])).astype(BF16)
                s_sub[c, b] = lax.dot_general(q_t, k_t, NT_DIMS, preferred_element_type=F32)
            terms = []
            for j in range(SUB):
                e = jnp.exp(jnp.where(rows >= j, c_b - c_b[j:j + 1, :], NEG_INF))
                terms.append(q_b * k_b[j:j + 1, :] * e)
            diag[c, b] = jnp.dot(jnp.concatenate(terms, axis=0).astype(BF16), ones, preferred_element_type=F32)

    yield
    cross = {c: jnp.dot(s_cross[c].astype(BF16), vb[:c * CHUNK], preferred_element_type=F32) for c in s_cross}
    sub = {cb: jnp.dot(s_sub[cb].astype(BF16), v[cb[0] * CHUNK:cb[0] * CHUNK + cb[1] * SUB].astype(BF16),
                       preferred_element_type=F32) for cb in s_sub}
    st_ref[...] = st * jnp.exp(last) + kv
    yield

    outs = []
    for c in range(n_ch):
        for b in range(n_sub):
            lo = c * CHUNK + b * SUB
            acc = o_inter[lo:lo + SUB]
            if c > 0:
                acc = acc + cross[c][b * SUB:(b + 1) * SUB]
            if b > 0:
                acc = acc + sub[c, b]
            v_b = v[lo:lo + SUB]
            for j in range(SUB):
                col = diag[c, b][j * SUB:(j + 1) * SUB]
                if dv > LANES:
                    col = jnp.concatenate([col] * (dv // LANES), axis=1)
                acc = acc + col * v_b[j:j + 1, :]
            outs.append(acc)
    return jnp.concatenate(outs, axis=0)


def _run_heads(steps):
    results = [None] * len(steps)
    live = list(range(len(steps)))
    while live:
        for n in list(live):
            try:
                next(steps[n])
            except StopIteration as done:
                results[n] = done.value
                live.remove(n)
    return results


def _log_sigmoid(z):
    return jnp.minimum(z, 0.0) - jnp.log(1.0 + jnp.exp(-jnp.abs(z)))


def _hgrn2_body(hq_ref, hf_ref, hi_ref, hg_ref, lb_ref, nw_ref, o_ref, st_ref):
    @pl.when(pl.program_id(2) == 0)
    def _():
        st_ref[...] = jnp.zeros_like(st_ref)

    steps = []
    for hh in range(REC_HP):
        cols = slice(hh * HG_DK, (hh + 1) * HG_DK)
        lb = lb_ref[:, cols]
        forget = lb + (1.0 - lb) * jax.nn.sigmoid(hf_ref[0, :, cols].astype(F32))
        q = _silu(hq_ref[0, :, cols].astype(F32)) * (HG_DK ** -0.5)
        steps.append(_recurrence_levels(q, 1.0 - forget, hi_ref[0, :, cols].astype(F32), jnp.log(forget),
                                        st_ref.at[hh]))
    for hh, o in enumerate(_run_heads(steps)):
        cols = slice(hh * HG_DV, (hh + 1) * HG_DV)
        o_ref[0, :, cols] = (_rms(o, nw_ref[...]) * _silu(hg_ref[0, :, cols].astype(F32))).astype(o_ref.dtype)


def _gla_body(gq_ref, gk_ref, gv_ref, gg_ref, glow_ref, gk2_ref, gb_ref, nw_ref, o_ref, st_ref):
    @pl.when(pl.program_id(2) == 0)
    def _():
        st_ref[...] = jnp.zeros_like(st_ref)

    z = jnp.dot(glow_ref[0], gk2_ref[...], precision=HIGHEST, preferred_element_type=F32) + gb_ref[...]
    la = _log_sigmoid(z) / GLA_GATE_NORM
    steps = []
    for hh in range(REC_HP):
        kc = slice(hh * GLA_DK, (hh + 1) * GLA_DK)
        vc = slice(hh * GLA_DV, (hh + 1) * GLA_DV)
        q = gq_ref[0, :, kc].astype(F32) * (GLA_DK ** -0.5)
        steps.append(_recurrence_levels(q, gk_ref[0, :, kc].astype(F32), gv_ref[0, :, vc].astype(F32), la[:, kc],
                                        st_ref.at[hh]))
    for hh, o in enumerate(_run_heads(steps)):
        vc = slice(hh * GLA_DV, (hh + 1) * GLA_DV)
        o_ref[0, :, vc] = (_rms(o, nw_ref[...]) * _silu(gg_ref[0, :, vc].astype(F32))).astype(o_ref.dtype)


def hgrn2_mix(proj, lb, hg_norm):
    b, s, _ = proj.shape

    hsteps = HG_HEADS // REC_HP

    def col(base):
        return pl.BlockSpec((1, REC_T, REC_HP * HG_DK), lambda bi, h, t: (bi, t, base + h))

    return pl.pallas_call(
        _hgrn2_body,
        grid=(b, hsteps, s // REC_T),
        in_specs=[col(0), col(hsteps), col(2 * hsteps), col(3 * hsteps),
                  pl.BlockSpec((1, REC_HP * HG_DK), lambda bi, h, t: (0, h)),
                  pl.BlockSpec((1, HG_DV), lambda bi, h, t: (0, 0))],
        out_specs=pl.BlockSpec((1, REC_T, REC_HP * HG_DV), lambda bi, h, t: (bi, t, h)),
        out_shape=jax.ShapeDtypeStruct((b, s, HG_HEADS * HG_DV), BF16),
        scratch_shapes=[pltpu.VMEM((REC_HP, HG_DV, HG_DK), F32)],
        compiler_params=_params("parallel", "parallel", "arbitrary"),
        name="hgrn2_mix",
    )(proj, proj, proj, proj, lb.reshape(1, HG_QK), hg_norm.reshape(1, HG_DV))


def gla_mix(proj, glow, gk2_pad, gk_bias, gla_norm):
    b, s, _ = proj.shape
    hsteps = GLA_HEADS // REC_HP
    kw, vw = REC_HP * GLA_DK, REC_HP * GLA_DV
    q0 = 4 * HG_QK // kw
    k0 = q0 + hsteps
    v0 = (4 * HG_QK + 2 * GLA_HEADS * GLA_DK) // vw
    g0 = v0 + hsteps
    return pl.pallas_call(
        _gla_body,
        grid=(b, hsteps, s // REC_T),
        in_specs=[pl.BlockSpec((1, REC_T, kw), lambda bi, h, t: (bi, t, q0 + h)),
                  pl.BlockSpec((1, REC_T, kw), lambda bi, h, t: (bi, t, k0 + h)),
                  pl.BlockSpec((1, REC_T, vw), lambda bi, h, t: (bi, t, v0 + h)),
                  pl.BlockSpec((1, REC_T, vw), lambda bi, h, t: (bi, t, g0 + h)),
                  pl.BlockSpec((1, REC_T, LANES), lambda bi, h, t: (bi, t, 0)),
                  pl.BlockSpec((LANES, kw), lambda bi, h, t: (0, h)),
                  pl.BlockSpec((1, kw), lambda bi, h, t: (0, h)),
                  pl.BlockSpec((1, GLA_DV), lambda bi, h, t: (0, 0))],
        out_specs=pl.BlockSpec((1, REC_T, vw), lambda bi, h, t: (bi, t, h)),
        out_shape=jax.ShapeDtypeStruct((b, s, GLA_HEADS * GLA_DV), BF16),
        scratch_shapes=[pltpu.VMEM((REC_HP, GLA_DV, GLA_DK), F32)],
        compiler_params=_params("parallel", "parallel", "arbitrary"),
        name="gla_mix",
    )(proj, proj, proj, proj, glow, gk2_pad, gk_bias.reshape(1, -1), gla_norm.reshape(1, GLA_DV))


def _moba_body(q_ref, k_ref, v_ref, o_ref, kmean_ref, vt_ref, bias_ref, sa_ref, sb_ref, *, n_blocks):
    qi = pl.program_id(2)
    blk, dh = MOBA_BLOCK, MOBA_DH
    heads = [slice(hh * dh, (hh + 1) * dh) for hh in range(MOBA_HP)]

    @pl.when(qi == 0)
    def _():
        for n in range(n_blocks):
            rs = slice(n * blk, (n + 1) * blk)
            for hh, cols in enumerate(heads):
                kmean_ref[hh, n:n + 1, :] = jnp.mean(k_ref[0, rs, cols].astype(F32), axis=0, keepdims=True)
                vt_ref[hh, :, rs] = v_ref[0, rs, cols].astype(F32).T.astype(BF16)

    c = (MOBA_DH ** -0.5) * LOG2_E
    own = pl.ds(pl.multiple_of(qi * blk, blk), blk)
    brow = lax.broadcasted_iota(I32, (n_blocks, blk), 0)
    past = brow < qi
    k_i = lax.broadcasted_iota(I32, (blk, blk), 0)
    q_i = lax.broadcasted_iota(I32, (blk, blk), 1)

    qs = [q_ref[0, :, cols] for cols in heads]
    gates = [lax.dot_general(kmean_ref[hh], qs[hh].astype(F32), NT_DIMS, precision=HIGHEST,
                             preferred_element_type=F32) for hh in range(MOBA_HP)]
    own_s = [lax.dot_general(k_ref[0, own, cols], qs[hh], NT_DIMS, preferred_element_type=F32)
             for hh, cols in enumerate(heads)]
    r01_first = pl.ds(0, 2 * blk)
    for hh, cols in enumerate(heads):
        sa_ref[hh] = lax.dot_general(k_ref[0, r01_first, cols], qs[hh], NT_DIMS, preferred_element_type=F32)
    carry0 = []
    for hh in range(MOBA_HP):
        gate = jnp.where(past, gates[hh], NEG_INF)
        rank = jnp.zeros((n_blocks, blk), I32)
        for m in range(n_blocks):
            gm = gate[m:m + 1, :]
            rank = rank + jnp.where(gm > gate, 1, jnp.where(gm == gate, jnp.where(brow > m, 1, 0), 0))
        keep = jnp.where(past, jnp.where(rank < MOBA_TOPK, 1, 0), 0)
        bias_ref[hh] = jnp.where(keep > 0, 0.0, MASK_BIAS)
    for hh in range(MOBA_HP):
        s = jnp.where(k_i <= q_i, own_s[hh], NEG_INF)
        m0 = jnp.max(s, axis=0, keepdims=True)
        p = jnp.exp2((s - m0) * c)
        l0 = jnp.sum(p, axis=0, keepdims=True)
        acc0 = jnp.dot(vt_ref[hh, :, own], p.astype(BF16), preferred_element_type=F32)
        carry0 += [m0, l0, acc0]

    n_pairs = (qi + 1) // 2

    def score_pair(pair, dst_ref):
        p = jnp.minimum(pair, n_blocks // 2 - 1)
        r01 = pl.ds(pl.multiple_of(p * 2 * blk, 2 * blk), 2 * blk)
        for hh, cols in enumerate(heads):
            dst_ref[hh] = lax.dot_general(k_ref[0, r01, cols], qs[hh], NT_DIMS, preferred_element_type=F32)

    def absorb_pair(pair, src_ref, carry):
        n0 = 2 * pair
        r01 = pl.ds(pl.multiple_of(n0 * blk, 2 * blk), 2 * blk)
        out = []
        for hh, cols in enumerate(heads):
            m_run, l_run, acc = carry[3 * hh:3 * hh + 3]
            s0 = src_ref[hh, 0:blk, :] + bias_ref[hh, pl.ds(n0, 1), :]
            s1 = src_ref[hh, blk:2 * blk, :] + bias_ref[hh, pl.ds(n0 + 1, 1), :]
            m_new = jnp.maximum(m_run, jnp.maximum(jnp.max(s0, axis=0, keepdims=True),
                                                   jnp.max(s1, axis=0, keepdims=True)))
            alpha = jnp.exp2((m_run - m_new) * c)
            p0 = jnp.exp2((s0 - m_new) * c)
            p1 = jnp.exp2((s1 - m_new) * c)
            l_new = alpha * l_run + jnp.sum(p0, axis=0, keepdims=True) + jnp.sum(p1, axis=0, keepdims=True)
            p01 = jnp.concatenate([p0.astype(BF16), p1.astype(BF16)], axis=0)
            acc = alpha * acc + jnp.dot(vt_ref[hh, :, r01], p01, preferred_element_type=F32)
            out += [m_new, l_new, acc]
        return tuple(out)

    def two_pairs(t, carry):
        score_pair(2 * t + 1, sb_ref)
        carry = absorb_pair(2 * t, sa_ref, carry)

        def second(carry):
            score_pair(2 * t + 2, sa_ref)
            return absorb_pair(2 * t + 1, sb_ref, carry)

        return lax.cond(2 * t + 1 < n_pairs, second, lambda carry: carry, carry)

    fin = lax.fori_loop(0, (n_pairs + 1) // 2, two_pairs, tuple(carry0))
    for hh, cols in enumerate(heads):
        o_ref[0, :, cols] = (fin[3 * hh + 2] / fin[3 * hh + 1]).T.astype(o_ref.dtype)


def moba_attention(qkv):
    b, s, _ = qkv.shape
    n_blocks = s // MOBA_BLOCK
    hw = MOBA_HP * MOBA_DH
    hsteps = MOBA_HEADS // MOBA_HP
    return pl.pallas_call(
        functools.partial(_moba_body, n_blocks=n_blocks),
        grid=(b, hsteps, n_blocks),
        in_specs=[pl.BlockSpec((1, MOBA_BLOCK, hw), lambda bi, h, t: (bi, t, h)),
                  pl.BlockSpec((1, s, hw), lambda bi, h, t: (bi, 0, hsteps + h)),
                  pl.BlockSpec((1, s, hw), lambda bi, h, t: (bi, 0, 2 * hsteps + h))],
        out_specs=pl.BlockSpec((1, MOBA_BLOCK, hw), lambda bi, h, t: (bi, t, h)),
        out_shape=jax.ShapeDtypeStruct((b, s, D_MODEL), BF16),
        scratch_shapes=[pltpu.VMEM((MOBA_HP, n_blocks, MOBA_DH), F32),
                        pltpu.VMEM((MOBA_HP, MOBA_DH, s), BF16),
                        pltpu.VMEM((MOBA_HP, n_blocks, MOBA_BLOCK), F32),
                        pltpu.VMEM((MOBA_HP, 2 * MOBA_BLOCK, MOBA_BLOCK), F32),
                        pltpu.VMEM((MOBA_HP, 2 * MOBA_BLOCK, MOBA_BLOCK), F32)],
        compiler_params=_params("parallel", "parallel", "arbitrary"),
        name="moba_attention",
    )(qkv, qkv, qkv)


def _router_body(h_ref, g_ref, wh_ref, wl_ref, b_ref, idx_ref, wgt_ref, cnt_ref, x3_ref, run_ref):
    step = pl.program_id(0)

    @pl.when(step == 0)
    def _():
        run_ref[...] = jnp.zeros_like(run_ref)

    tm = h_ref.shape[0]
    xn = _rms(h_ref[...], g_ref[...])
    for ch in range(ROW_CHUNKS):
        x3_ref[pl.ds(ch, tm, stride=ROW_CHUNKS), :] = xn[:, ch * LANES:(ch + 1) * LANES]
    x_hi = xn.astype(BF16)
    x_lo = (xn - x_hi.astype(F32)).astype(BF16)
    logits = (jnp.dot(x_hi, wh_ref[...], preferred_element_type=F32)
              + (jnp.dot(x_hi, wl_ref[...], preferred_element_type=F32)
                 + jnp.dot(x_lo, wh_ref[...], preferred_element_type=F32))) + b_ref[...]
    lane = lax.broadcasted_iota(I32, (tm, LANES), 1)

    def first_max(vals):
        top = jnp.max(vals, axis=-1, keepdims=True)
        where = jnp.min(jnp.where(vals == top, lane, LANES), axis=-1, keepdims=True)
        return top, where

    g_logits = jnp.where(lane < N_GROUPS, logits, NEG_INF)
    g_top, grp = first_max(g_logits)
    p_grp = 1.0 / jnp.sum(jnp.exp(g_logits - g_top), axis=-1, keepdims=True)
    lo = EXPERT_LANE0 + grp * EXPERTS_PER_GROUP
    e_logits = jnp.where((lane >= lo) & (lane < lo + EXPERTS_PER_GROUP), logits, NEG_INF)
    v0, j0 = first_max(e_logits)
    v1, j1 = first_max(jnp.where(lane == j0, NEG_INF, e_logits))
    t = jnp.exp(v1 - v0)
    w0 = p_grp / (1.0 + t)
    w1 = p_grp * t / (1.0 + t)

    hit0 = lane == j0
    hit1 = lane == j1
    member = jnp.where(hit0 | hit1, 1.0, 0.0)
    r_i = lax.broadcasted_iota(I32, (tm, tm), 0)
    c_i = lax.broadcasted_iota(I32, (tm, tm), 1)
    before = jnp.where(c_i < r_i, 1.0, 0.0).astype(BF16)
    prior = jnp.dot(before, member.astype(BF16), preferred_element_type=F32) + run_ref[...]
    rank0 = jnp.sum(jnp.where(hit0, prior, 0.0), axis=-1, keepdims=True).astype(I32)
    rank1 = jnp.sum(jnp.where(hit1, prior, 0.0), axis=-1, keepdims=True).astype(I32)
    run_ref[...] = run_ref[...] + jnp.sum(member, axis=0, keepdims=True)
    cnt_ref[...] = run_ref[...]

    idx_ref[...] = jnp.where(lane == 0, j0 - EXPERT_LANE0,
                             jnp.where(lane == 1, j1 - EXPERT_LANE0,
                                       jnp.where(lane == 2, rank0, jnp.where(lane == 3, rank1, 0))))
    wgt_ref[...] = jnp.where(lane == 0, w0, jnp.where(lane == 1, w1, 0.0))


def moe_router(h2d, g, w_cat, b_cat):
    n_tok, d = h2d.shape
    w_hi = w_cat.astype(BF16)
    return pl.pallas_call(
        _router_body,
        grid=(n_tok // ROUTE_TM,),
        in_specs=[pl.BlockSpec((ROUTE_TM, d), lambda i: (i, 0)),
                  pl.BlockSpec((1, d), lambda i: (0, 0)),
                  pl.BlockSpec((d, LANES), lambda i: (0, 0)),
                  pl.BlockSpec((d, LANES), lambda i: (0, 0)),
                  pl.BlockSpec((1, LANES), lambda i: (0, 0))],
        out_specs=[pl.BlockSpec((ROUTE_TM, LANES), lambda i: (i, 0)),
                   pl.BlockSpec((ROUTE_TM, LANES), lambda i: (i, 0)),
                   pl.BlockSpec((1, LANES), lambda i: (0, 0)),
                   pl.BlockSpec((ROUTE_TM * ROW_CHUNKS, LANES), lambda i: (i, 0))],
        out_shape=[jax.ShapeDtypeStruct((n_tok, LANES), I32),
                   jax.ShapeDtypeStruct((n_tok, LANES), F32),
                   jax.ShapeDtypeStruct((1, LANES), F32),
                   jax.ShapeDtypeStruct((n_tok * ROW_CHUNKS, LANES), F32)],
        scratch_shapes=[pltpu.VMEM((1, LANES), F32)],
        compiler_params=_params("arbitrary"),
        name="moe_router",
    )(h2d, g.reshape(1, d), w_hi, (w_cat - w_hi.astype(F32)).astype(BF16), b_cat)


INV_UNROLL = 16
DUMP_ROWS = MOE_BM


def _expert_body(be_ref, nu_ref, dest_ref, gfill_hbm, sfill_hbm, x3_hbm, w1_ref, w3_ref, w2_ref, o2_hbm,
                 gsrc_ref, sdst_ref, xbuf, ybuf, gsem, ssem, isem, w1_bf, w3_bf, w2_bf, *, n_tok, n_blocks):
    i = pl.program_id(0)
    n_used = nu_ref[0]
    slot = i % 2
    plane_rows = (n_tok + DUMP_ROWS) * ROW_CHUNKS

    def gather_row(block, r, to_slot):
        src = pl.multiple_of(gsrc_ref[block * MOE_BM + r], ROW_CHUNKS)
        return pltpu.make_async_copy(x3_hbm.at[pl.ds(src, ROW_CHUNKS)],
                                     xbuf.at[to_slot, pl.ds(r * BUF_PITCH, ROW_CHUNKS)], gsem.at[to_slot])

    def scatter_row(block, r, from_slot):
        dst = pl.multiple_of(sdst_ref[block * MOE_BM + r], ROW_CHUNKS)
        return pltpu.make_async_copy(ybuf.at[from_slot, pl.ds(r * BUF_PITCH, ROW_CHUNKS)],
                                     o2_hbm.at[pl.ds(dst, ROW_CHUNKS)], ssem.at[from_slot])

    block_rows = MOE_BM * ROW_CHUNKS

    def gather_wait(of_slot):
        pltpu.make_async_copy(x3_hbm.at[pl.ds(0, block_rows)], xbuf.at[of_slot, pl.ds(0, block_rows)],
                              gsem.at[of_slot]).wait()

    def scatter_wait(of_slot):
        pltpu.make_async_copy(ybuf.at[of_slot, pl.ds(0, block_rows)], o2_hbm.at[pl.ds(0, block_rows)],
                              ssem.at[of_slot]).wait()

    @pl.when(i == 0)
    def _():
        for fill_hbm, table in ((gfill_hbm, gsrc_ref), (sfill_hbm, sdst_ref)):
            fill = pltpu.make_async_copy(fill_hbm, table, isem)
            fill.start()
            fill.wait()

        def body(c, carry):
            base = c * INV_UNROLL
            rows = [dest_ref[base + u] for u in range(INV_UNROLL)]
            tok_row0 = c * (INV_UNROLL // 2 * ROW_CHUNKS)
            for u in range(INV_UNROLL):
                gsrc_ref[rows[u]] = tok_row0 + (u // 2) * ROW_CHUNKS
                sdst_ref[rows[u]] = tok_row0 + ((u % 2) * plane_rows + (u // 2) * ROW_CHUNKS)
            return carry
        lax.fori_loop(0, dest_ref.shape[0] // INV_UNROLL, body, 0)
        ybuf[...] = jnp.zeros_like(ybuf)
        for plane in range(2):
            init = pltpu.make_async_copy(ybuf.at[plane, pl.ds(0, block_rows)],
                                         o2_hbm.at[pl.ds(plane * plane_rows + n_tok * ROW_CHUNKS, block_rows)], isem)
            init.start()
            init.wait()
        for r in range(MOE_BM):
            gather_row(0, r, 0).start()

    @pl.when(i < n_used)
    def _():
        gather_wait(slot)

        @pl.when((i == 0) | (be_ref[i] != be_ref[jnp.maximum(i - 1, 0)]))
        def _():
            w1_bf[...] = w1_ref[...].astype(BF16)
            w3_bf[...] = w3_ref[...].astype(BF16)
            w2_bf[...] = w2_ref[...].astype(BF16)

        @pl.when(i >= 1)
        def _():
            scatter_wait(slot)

        prev = jnp.where(i == 0, n_blocks - 1, i - 1)
        for r in range(MOE_BM):
            scatter_row(prev, r, 1 - slot).start()
        xn = jnp.concatenate([xbuf[slot, pl.ds(ch, MOE_BM, stride=BUF_PITCH), :] for ch in range(ROW_CHUNKS)],
                             axis=1).astype(BF16)
        for r in range(MOE_BM):
            gather_row(i + 1, r, 1 - slot).start()
        h1 = jnp.dot(xn, w1_bf[...], preferred_element_type=F32)
        h3 = jnp.dot(xn, w3_bf[...], preferred_element_type=F32)
        act = (_silu(h1) * h3).astype(BF16)
        y = jnp.dot(act, w2_bf[...], preferred_element_type=F32)
        for ch in range(ROW_CHUNKS):
            ybuf[slot, pl.ds(ch, MOE_BM, stride=BUF_PITCH), :] = y[:, ch * LANES:(ch + 1) * LANES]

    @pl.when(i == n_used)
    def _():
        for r in range(MOE_BM):
            scatter_row(i - 1, r, 1 - slot).start()
        gather_wait(slot)
        scatter_wait(slot)
        scatter_wait(1 - slot)


def moe_experts(x3, dest_flat, w1, w3, w2, layer, block_e, n_used):
    plane_rows = (x3.shape[0] // ROW_CHUNKS + DUMP_ROWS) * ROW_CHUNKS
    n_tok = x3.shape[0] // ROW_CHUNKS
    d = D_MODEL
    n_blocks = block_e.shape[0]
    n_rows = n_blocks * MOE_BM
    f = w1.shape[-1]

    def w_map(i, be, nu, dest):
        return (layer, be[i], 0, 0)

    grid_spec = pltpu.PrefetchScalarGridSpec(
        num_scalar_prefetch=3,
        grid=(n_blocks,),
        in_specs=[pl.BlockSpec(memory_space=pl.ANY),
                  pl.BlockSpec(memory_space=pl.ANY),
                  pl.BlockSpec(memory_space=pl.ANY),
                  pl.BlockSpec((None, None, d, f), w_map),
                  pl.BlockSpec((None, None, d, f), w_map),
                  pl.BlockSpec((None, None, f, d), w_map)],
        out_specs=pl.BlockSpec(memory_space=pl.ANY),
        scratch_shapes=[pltpu.SMEM((n_rows,), I32),
                        pltpu.SMEM((n_rows,), I32),
                        pltpu.VMEM((2, MOE_BM * BUF_PITCH, LANES), F32),
                        pltpu.VMEM((2, MOE_BM * BUF_PITCH, LANES), F32),
                        pltpu.SemaphoreType.DMA((2,)),
                        pltpu.SemaphoreType.DMA((2,)),
                        pltpu.SemaphoreType.DMA(()),
                        pltpu.VMEM((d, f), BF16), pltpu.VMEM((d, f), BF16), pltpu.VMEM((f, d), BF16)],
    )
    pad_row = jnp.arange(n_rows, dtype=I32) % MOE_BM
    o2 = pl.pallas_call(
        functools.partial(_expert_body, n_tok=n_tok, n_blocks=n_blocks),
        grid_spec=grid_spec,
        out_shape=jax.ShapeDtypeStruct((2 * plane_rows, LANES), F32),
        compiler_params=_params("arbitrary"),
        name="moe_experts",
    )(block_e, n_used, dest_flat, jnp.zeros((n_rows,), I32), (n_tok + pad_row) * ROW_CHUNKS, x3, w1, w3, w2)
    return o2.reshape(2, plane_rows, LANES)


def _combine_body(*refs, final_norm):
    if final_norm:
        h_ref, wgt_ref, o2_ref, gf_ref, o_ref = refs
    else:
        h_ref, wgt_ref, o2_ref, o_ref = refs
    wgt = wgt_ref[...]
    y0, y1 = (jnp.concatenate([o2_ref[s, pl.ds(ch, COMBINE_T, stride=ROW_CHUNKS), :] for ch in range(ROW_CHUNKS)], axis=1)
              for s in range(2))
    out = h_ref[...] + wgt[:, 0:1] * y0 + wgt[:, 1:2] * y1
    if final_norm:
        out = _rms(out, gf_ref[...])
    o_ref[...] = out


def moe_combine(h2d, wgt, o2, g_final=None):
    n_tok, d = h2d.shape
    final_norm = g_final is not None
    in_specs = [pl.BlockSpec((COMBINE_T, d), lambda i: (i, 0)),
                pl.BlockSpec((COMBINE_T, LANES), lambda i: (i, 0)),
                pl.BlockSpec((2, COMBINE_T * ROW_CHUNKS, LANES), lambda i: (0, i, 0))]
    args = [h2d, wgt, o2]
    if final_norm:
        in_specs.append(pl.BlockSpec((1, d), lambda i: (0, 0)))
        args.append(g_final.reshape(1, d))
    return pl.pallas_call(
        functools.partial(_combine_body, final_norm=final_norm),
        grid=(n_tok // COMBINE_T,),
        in_specs=in_specs,
        out_specs=pl.BlockSpec((COMBINE_T, d), lambda i: (i, 0)),
        out_shape=jax.ShapeDtypeStruct((n_tok, d), F32),
        compiler_params=_params("parallel"),
        name="moe_combine_final" if final_norm else "moe_combine",
    )(*args)


def hierarchical_moe(h2d, layer, norm_g, w_group, b_group, w_router, b_router, w1, w3, w2, g_final=None):
    n_tok, d = h2d.shape
    pad_l = LANES - N_GROUPS - N_EXPERTS
    w_cat = jnp.concatenate([w_group, w_router, jnp.zeros((d, pad_l), F32)], axis=1)
    b_cat = jnp.concatenate([b_group, b_router, jnp.zeros((pad_l,), F32)]).reshape(1, LANES)
    idx, wgt, cnt, x3 = moe_router(h2d, norm_g, w_cat, b_cat)

    counts = cnt[0, EXPERT_LANE0:EXPERT_LANE0 + N_EXPERTS].astype(I32)
    padded = (counts + MOE_BM - 1) // MOE_BM * MOE_BM
    pends = jnp.cumsum(padded)
    pstarts = pends - padded
    n_blocks = (n_tok * 2) // MOE_BM + N_EXPERTS
    blk_row0 = jnp.arange(n_blocks, dtype=I32) * MOE_BM
    block_e = jnp.minimum(jnp.sum(pends[None, :] <= blk_row0[:, None], axis=1), N_EXPERTS - 1).astype(I32)
    n_used = (pends[-1:] // MOE_BM).astype(I32)
    e_iota = jnp.arange(N_EXPERTS, dtype=I32)
    row0 = jnp.sum(jnp.where(idx[:, 0:2, None] == e_iota, pstarts, 0), axis=-1)
    dest = (row0 + idx[:, 2:4]).astype(I32).reshape(-1)

    o2 = moe_experts(x3, dest, w1, w3, w2, layer, block_e, n_used)
    return moe_combine(h2d, wgt, o2, g_final)


def kernel(x, norm_mix, norm_ffn, norm_final, w_in_even, hg_lb_logits, hg_norm, gla_gk2, gla_gk_bias,
           gla_norm, w_out_even, w_qkv_odd, w_o_odd, router_group_w, router_group_b, router_expert_w,
           router_expert_b, expert_w1, expert_w3, expert_w2):
    b, s, d = x.shape
    n_tok = b * s
    depth = norm_mix.shape[0]
    lb_table = jnp.cumsum(jax.nn.softmax(hg_lb_logits.astype(F32), axis=0), axis=0)
    h = x.reshape(n_tok, d)
    for l in range(depth):
        if l % 2 == 0:
            e = l // 2
            w_in = w_in_even[e]
            w_low = jnp.pad(w_in[:, MAIN_IN:], ((0, 0), (0, LANES - GLA_RANK))).astype(BF16)
            proj, glow = norm_matmul(h, norm_mix[l], w_in[:, :MAIN_IN].astype(BF16), w_low, tm=PROJ_TM, tn=PROJ_TN)
            proj = proj.reshape(b, s, MAIN_IN)
            gk2_pad = jnp.pad(gla_gk2[e], ((0, LANES - GLA_RANK), (0, 0)))
            o_hg = hgrn2_mix(proj, lb_table[l], hg_norm[e])
            o_gla = gla_mix(proj, glow.reshape(b, s, LANES), gk2_pad, gla_gk_bias[e], gla_norm[e])
            h = matmul_residual([o_hg.reshape(n_tok, -1), o_gla.reshape(n_tok, -1)], w_out_even[e].astype(BF16), h,
                                tm=OUT_TM)
        else:
            o = l // 2
            qkv = norm_matmul(h, norm_mix[l], w_qkv_odd[o].astype(BF16), tm=PROJ_TM, tn=PROJ_TN)
            attn = moba_attention(qkv.reshape(b, s, 3 * d))
            h = matmul_residual([attn.reshape(n_tok, d)], w_o_odd[o].astype(BF16), h, tm=OUT_TM)
        h = hierarchical_moe(h, l, norm_ffn[l], router_group_w[l], router_group_b[l], router_expert_w[l],
                             router_expert_b[l], expert_w1, expert_w3, expert_w2,
                             g_final=norm_final if l == depth - 1 else None)
    return h.reshape(b, s, d)
```

```python
import functools

import jax
import jax.numpy as jnp
from jax import lax
from jax.experimental import pallas as pl
from jax.experimental.pallas import tpu as pltpu

F32 = jnp.float32
BF16 = jnp.bfloat16
I32 = jnp.int32
HIGHEST = lax.Precision.HIGHEST
NEG_INF = float("-inf")
LOG2_E = 1.4426950408889634

EPS = 1e-6
D_MODEL = 2048

HG_HEADS, HG_DK, HG_DV = 8, 128, 128
GLA_HEADS, GLA_DK, GLA_DV = 4, 128, 256
GLA_RANK = 16
GLA_GATE_NORM = 16.0
HG_QK = HG_HEADS * HG_DK
MAIN_IN = 4 * HG_QK + 2 * GLA_HEADS * GLA_DK + 2 * GLA_HEADS * GLA_DV
CHUNK = 64
SUB = 8
REC_T = 256
REC_HP = 4

MOBA_HEADS, MOBA_DH = 16, 128
MOBA_BLOCK = 256
MOBA_TOPK = 3
MOBA_HP = 4
ONES_ROWS = 16
MASK_BIAS = -1e30

N_GROUPS, EXPERTS_PER_GROUP = 4, 8
N_EXPERTS = N_GROUPS * EXPERTS_PER_GROUP
D_EXPERT = D_MODEL // 4
EXPERT_LANE0 = N_GROUPS
ROUTE_TM = 256
MOE_BM = 256
COMBINE_T = 256
ROW_CHUNKS = D_MODEL // 128
BUF_PITCH = ROW_CHUNKS + 8

PROJ_TM, PROJ_TN = 1024, 1024
OUT_TM = 512

LANES = 128
VMEM_LIMIT = 56 * 1024 * 1024

NT_DIMS = (((1,), (1,)), ((), ()))
TN_DIMS = (((0,), (0,)), ((), ()))


def _params(*sem):
    return pltpu.CompilerParams(dimension_semantics=sem, vmem_limit_bytes=VMEM_LIMIT)


def _rms(x, g):
    return x * lax.rsqrt(jnp.mean(x * x, axis=-1, keepdims=True) + EPS) * g


def _silu(x):
    return x * jax.nn.sigmoid(x)


def _norm_matmul_body(*refs, has_side):
    if has_side:
        x_ref, g_ref, w_ref, ws_ref, o_ref, os_ref, xn_ref = refs
    else:
        x_ref, g_ref, w_ref, o_ref, xn_ref = refs

    @pl.when(pl.program_id(1) == 0)
    def _():
        xn_ref[...] = _rms(x_ref[...], g_ref[...]).astype(BF16)
        if has_side:
            os_ref[...] = jnp.dot(xn_ref[...], ws_ref[...], preferred_element_type=F32)

    o_ref[...] = jnp.dot(xn_ref[...], w_ref[...], preferred_element_type=F32).astype(o_ref.dtype)


def norm_matmul(x, g, w, w_side=None, *, tm, tn, n=None):
    m, k = x.shape
    n = w.shape[1] if n is None else n
    has_side = w_side is not None
    in_specs = [pl.BlockSpec((tm, k), lambda i, j: (i, 0)),
                pl.BlockSpec((1, k), lambda i, j: (0, 0)),
                pl.BlockSpec((k, tn), lambda i, j: (0, j))]
    out_specs = [pl.BlockSpec((tm, tn), lambda i, j: (i, j))]
    out_shape = [jax.ShapeDtypeStruct((m, n), BF16)]
    args = [x, g.reshape(1, k), w]
    if has_side:
        ns = w_side.shape[1]
        in_specs.append(pl.BlockSpec((k, ns), lambda i, j: (0, 0)))
        out_specs.append(pl.BlockSpec((tm, ns), lambda i, j: (i, 0)))
        out_shape.append(jax.ShapeDtypeStruct((m, ns), F32))
        args.append(w_side)
    outs = pl.pallas_call(
        functools.partial(_norm_matmul_body, has_side=has_side),
        grid=(m // tm, n // tn),
        in_specs=in_specs, out_specs=out_specs, out_shape=out_shape,
        scratch_shapes=[pltpu.VMEM((tm, k), BF16)],
        compiler_params=_params("parallel", "arbitrary"),
        name="norm_matmul_side" if has_side else "norm_matmul",
    )(*args)
    return outs if has_side else outs[0]


def _matmul_res_body(*refs):
    *a_refs, w_ref, r_ref, o_ref = refs
    acc = r_ref[...]
    k0 = 0
    for a_ref in a_refs:
        kp = a_ref.shape[1]
        acc = acc + jnp.dot(a_ref[...], w_ref[k0:k0 + kp, :], preferred_element_type=F32)
        k0 += kp
    o_ref[...] = acc


def matmul_residual(a_pieces, w, res, *, tm):
    m = res.shape[0]
    k, n = w.shape
    return pl.pallas_call(
        _matmul_res_body,
        grid=(m // tm,),
        in_specs=[pl.BlockSpec((tm, a.shape[1]), lambda i: (i, 0)) for a in a_pieces]
        + [pl.BlockSpec((k, n), lambda i: (0, 0)),
           pl.BlockSpec((tm, n), lambda i: (i, 0))],
        out_specs=pl.BlockSpec((tm, n), lambda i: (i, 0)),
        out_shape=jax.ShapeDtypeStruct((m, n), F32),
        compiler_params=_params("parallel"),
        name="matmul_residual",
    )(*a_pieces, w, res)


def _recurrence_levels(q, k, v, la, st_ref):
    n_ch = REC_T // CHUNK
    n_sub = CHUNK // SUB
    dv = v.shape[1]
    r_i = lax.broadcasted_iota(I32, (CHUNK, CHUNK), 0)
    c_i = lax.broadcasted_iota(I32, (CHUNK, CHUNK), 1)
    tri = (c_i <= r_i).astype(F32)
    local = [jnp.dot(tri, la[c * CHUNK:(c + 1) * CHUNK], precision=HIGHEST, preferred_element_type=F32)
             for c in range(n_ch)]
    cums = [local[0]]
    for c in range(1, n_ch):
        cums.append(local[c] + cums[-1][CHUNK - 1:CHUNK, :])
    cum = jnp.concatenate(cums, axis=0)
    last = cum[REC_T - 1:REC_T, :]
    vb = v.astype(BF16)
    st = st_ref[...]
    yield

    o_inter = lax.dot_general((q * jnp.exp(cum)).astype(BF16), st.astype(BF16), NT_DIMS,
                              preferred_element_type=F32)
    kv = lax.dot_general(vb, (k * jnp.exp(last - cum)).astype(BF16), TN_DIMS, preferred_element_type=F32)
    rows = lax.broadcasted_iota(I32, (SUB, 1), 0)
    ones = jnp.ones((q.shape[1], LANES), BF16)
    s_cross, s_sub, diag = {}, {}, {}
    for c in range(n_ch):
        c0 = c * CHUNK
        if c > 0:
            ref_pt = cum[c0 - 1:c0, :]
            q_t = (q[c0:c0 + CHUNK] * jnp.exp(cum[c0:c0 + CHUNK] - ref_pt)).astype(BF16)
            k_t = (k[:c0] * jnp.exp(ref_pt - cum[:c0])).astype(BF16)
            s_cross[c] = lax.dot_general(q_t, k_t, NT_DIMS, preferred_element_type=F32)
        for b in range(n_sub):
            lo = c0 + b * SUB
            q_b, k_b, c_b = q[lo:lo + SUB], k[lo:lo + SUB], cum[lo:lo + SUB]
            if b > 0:
                ref_pt = cum[lo - 1:lo, :]
                q_t = (q_b * jnp.exp(c_b - ref_pt)).astype(BF16)
                k_t = (k[c0:lo] * jnp.exp(ref_pt - cum[c0:lo])).astype(BF16)
                s_sub[c, b] = lax.dot_general(q_t, k_t, NT_DIMS, preferred_element_type=F32)
            terms = []
            for j in range(SUB):
                e = jnp.exp(jnp.where(rows >= j, c_b - c_b[j:j + 1, :], NEG_INF))
                terms.append(q_b * k_b[j:j + 1, :] * e)
            diag[c, b] = jnp.dot(jnp.concatenate(terms, axis=0).astype(BF16), ones, preferred_element_type=F32)

    yield
    cross = {c: jnp.dot(s_cross[c].astype(BF16), vb[:c * CHUNK], preferred_element_type=F32) for c in s_cross}
    sub = {cb: jnp.dot(s_sub[cb].astype(BF16), v[cb[0] * CHUNK:cb[0] * CHUNK + cb[1] * SUB].astype(BF16),
                       preferred_element_type=F32) for cb in s_sub}
    st_ref[...] = st * jnp.exp(last) + kv
    yield

    outs = []
    for c in range(n_ch):
        for b in range(n_sub):
            lo = c * CHUNK + b * SUB
            acc = o_inter[lo:lo + SUB]
            if c > 0:
                acc = acc + cross[c][b * SUB:(b + 1) * SUB]
            if b > 0:
                acc = acc + sub[c, b]
            v_b = v[lo:lo + SUB]
            for j in range(SUB):
                col = diag[c, b][j * SUB:(j + 1) * SUB]
                if dv > LANES:
                    col = jnp.concatenate([col] * (dv // LANES), axis=1)
                acc = acc + col * v_b[j:j + 1, :]
            outs.append(acc)
    return jnp.concatenate(outs, axis=0)


def _run_heads(steps):
    results = [None] * len(steps)
    live = list(range(len(steps)))
    while live:
        for n in list(live):
            try:
                next(steps[n])
            except StopIteration as done:
                results[n] = done.value
                live.remove(n)
    return results


def _log_sigmoid(z):
    return jnp.minimum(z, 0.0) - jnp.log(1.0 + jnp.exp(-jnp.abs(z)))


def _hgrn2_body(hq_ref, hf_ref, hi_ref, hg_ref, lb_ref, nw_ref, o_ref, st_ref):
    @pl.when(pl.program_id(2) == 0)
    def _():
        st_ref[...] = jnp.zeros_like(st_ref)

    steps = []
    for hh in range(REC_HP):
        cols = slice(hh * HG_DK, (hh + 1) * HG_DK)
        lb = lb_ref[:, cols]
        forget = lb + (1.0 - lb) * jax.nn.sigmoid(hf_ref[0, :, cols].astype(F32))
        q = _silu(hq_ref[0, :, cols].astype(F32)) * (HG_DK ** -0.5)
        steps.append(_recurrence_levels(q, 1.0 - forget, hi_ref[0, :, cols].astype(F32), jnp.log(forget),
                                        st_ref.at[hh]))
    for hh, o in enumerate(_run_heads(steps)):
        cols = slice(hh * HG_DV, (hh + 1) * HG_DV)
        o_ref[0, :, cols] = (_rms(o, nw_ref[...]) * _silu(hg_ref[0, :, cols].astype(F32))).astype(o_ref.dtype)


def _gla_body(gq_ref, gk_ref, gv_ref, gg_ref, glow_ref, gk2_ref, gb_ref, nw_ref, o_ref, st_ref):
    @pl.when(pl.program_id(2) == 0)
    def _():
        st_ref[...] = jnp.zeros_like(st_ref)

    z = jnp.dot(glow_ref[0], gk2_ref[...], precision=HIGHEST, preferred_element_type=F32) + gb_ref[...]
    la = _log_sigmoid(z) / GLA_GATE_NORM
    steps = []
    for hh in range(REC_HP):
        kc = slice(hh * GLA_DK, (hh + 1) * GLA_DK)
        vc = slice(hh * GLA_DV, (hh + 1) * GLA_DV)
        q = gq_ref[0, :, kc].astype(F32) * (GLA_DK ** -0.5)
        steps.append(_recurrence_levels(q, gk_ref[0, :, kc].astype(F32), gv_ref[0, :, vc].astype(F32), la[:, kc],
                                        st_ref.at[hh]))
    for hh, o in enumerate(_run_heads(steps)):
        vc = slice(hh * GLA_DV, (hh + 1) * GLA_DV)
        o_ref[0, :, vc] = (_rms(o, nw_ref[...]) * _silu(gg_ref[0, :, vc].astype(F32))).astype(o_ref.dtype)


def hgrn2_mix(proj, lb, hg_norm):
    b, s, _ = proj.shape

    hsteps = HG_HEADS // REC_HP

    def col(base):
        return pl.BlockSpec((1, REC_T, REC_HP * HG_DK), lambda bi, h, t: (bi, t, base + h))

    return pl.pallas_call(
        _hgrn2_body,
        grid=(b, hsteps, s // REC_T),
        in_specs=[col(0), col(hsteps), col(2 * hsteps), col(3 * hsteps),
                  pl.BlockSpec((1, REC_HP * HG_DK), lambda bi, h, t: (0, h)),
                  pl.BlockSpec((1, HG_DV), lambda bi, h, t: (0, 0))],
        out_specs=pl.BlockSpec((1, REC_T, REC_HP * HG_DV), lambda bi, h, t: (bi, t, h)),
        out_shape=jax.ShapeDtypeStruct((b, s, HG_HEADS * HG_DV), BF16),
        scratch_shapes=[pltpu.VMEM((REC_HP, HG_DV, HG_DK), F32)],
        compiler_params=_params("parallel", "parallel", "arbitrary"),
        name="hgrn2_mix",
    )(proj, proj, proj, proj, lb.reshape(1, HG_QK), hg_norm.reshape(1, HG_DV))


def gla_mix(proj, glow, gk2_pad, gk_bias, gla_norm):
    b, s, _ = proj.shape
    hsteps = GLA_HEADS // REC_HP
    kw, vw = REC_HP * GLA_DK, REC_HP * GLA_DV
    q0 = 4 * HG_QK // kw
    k0 = q0 + hsteps
    v0 = (4 * HG_QK + 2 * GLA_HEADS * GLA_DK) // vw
    g0 = v0 + hsteps
    return pl.pallas_call(
        _gla_body,
        grid=(b, hsteps, s // REC_T),
        in_specs=[pl.BlockSpec((1, REC_T, kw), lambda bi, h, t: (bi, t, q0 + h)),
                  pl.BlockSpec((1, REC_T, kw), lambda bi, h, t: (bi, t, k0 + h)),
                  pl.BlockSpec((1, REC_T, vw), lambda bi, h, t: (bi, t, v0 + h)),
                  pl.BlockSpec((1, REC_T, vw), lambda bi, h, t: (bi, t, g0 + h)),
                  pl.BlockSpec((1, REC_T, LANES), lambda bi, h, t: (bi, t, 0)),
                  pl.BlockSpec((LANES, kw), lambda bi, h, t: (0, h)),
                  pl.BlockSpec((1, kw), lambda bi, h, t: (0, h)),
                  pl.BlockSpec((1, GLA_DV), lambda bi, h, t: (0, 0))],
        out_specs=pl.BlockSpec((1, REC_T, vw), lambda bi, h, t: (bi, t, h)),
        out_shape=jax.ShapeDtypeStruct((b, s, GLA_HEADS * GLA_DV), BF16),
        scratch_shapes=[pltpu.VMEM((REC_HP, GLA_DV, GLA_DK), F32)],
        compiler_params=_params("parallel", "parallel", "arbitrary"),
        name="gla_mix",
    )(proj, proj, proj, proj, glow, gk2_pad, gk_bias.reshape(1, -1), gla_norm.reshape(1, GLA_DV))


def _moba_body(q_ref, k_ref, v_ref, o_ref, kmean_ref, vt_ref, bias_ref, sa_ref, sb_ref, *, n_blocks):
    qi = pl.program_id(2)
    blk, dh = MOBA_BLOCK, MOBA_DH
    heads = [slice(hh * dh, (hh + 1) * dh) for hh in range(MOBA_HP)]

    @pl.when(qi == 0)
    def _():
        for n in range(n_blocks):
            rs = slice(n * blk, (n + 1) * blk)
            for hh, cols in enumerate(heads):
                kmean_ref[hh, n:n + 1, :] = jnp.mean(k_ref[0, rs, cols].astype(F32), axis=0, keepdims=True)
                vt_ref[hh, 0:dh, rs] = v_ref[0, rs, cols].astype(F32).T.astype(BF16)
                vt_ref[hh, dh:dh + ONES_ROWS, rs] = jnp.ones((ONES_ROWS, blk), BF16)

    c = (MOBA_DH ** -0.5) * LOG2_E
    own = pl.ds(pl.multiple_of(qi * blk, blk), blk)
    brow = lax.broadcasted_iota(I32, (n_blocks, blk), 0)
    past = brow < qi
    k_i = lax.broadcasted_iota(I32, (blk, blk), 0)
    q_i = lax.broadcasted_iota(I32, (blk, blk), 1)

    qs = [q_ref[0, :, cols] for cols in heads]
    gates = [lax.dot_general(kmean_ref[hh], qs[hh].astype(F32), NT_DIMS, precision=HIGHEST,
                             preferred_element_type=F32) for hh in range(MOBA_HP)]
    own_s = [lax.dot_general(k_ref[0, own, cols], qs[hh], NT_DIMS, preferred_element_type=F32)
             for hh, cols in enumerate(heads)]
    r01_first = pl.ds(0, 2 * blk)
    for hh, cols in enumerate(heads):
        sa_ref[hh] = lax.dot_general(k_ref[0, r01_first, cols], qs[hh], NT_DIMS, preferred_element_type=F32)
    carry0 = []
    for hh in range(MOBA_HP):
        gate = jnp.where(past, gates[hh], NEG_INF)
        rank = jnp.zeros((n_blocks, blk), I32)
        for m in range(n_blocks):
            gm = gate[m:m + 1, :]
            rank = rank + jnp.where(gm > gate, 1, jnp.where(gm == gate, jnp.where(brow > m, 1, 0), 0))
        keep = jnp.where(past, jnp.where(rank < MOBA_TOPK, 1, 0), 0)
        bias_ref[hh] = jnp.where(keep > 0, 0.0, MASK_BIAS)
    for hh in range(MOBA_HP):
        s = jnp.where(k_i <= q_i, own_s[hh], NEG_INF)
        m0 = jnp.max(s, axis=0, keepdims=True)
        p = jnp.exp2((s - m0) * c)
        acc0 = jnp.dot(vt_ref[hh, :, own], p.astype(BF16), preferred_element_type=F32)
        carry0 += [m0, acc0]

    n_pairs = (qi + 1) // 2

    def score_pair(pair, dst_ref):
        p = jnp.minimum(pair, n_blocks // 2 - 1)
        r01 = pl.ds(pl.multiple_of(p * 2 * blk, 2 * blk), 2 * blk)
        for hh, cols in enumerate(heads):
            dst_ref[hh] = lax.dot_general(k_ref[0, r01, cols], qs[hh], NT_DIMS, preferred_element_type=F32)

    def absorb_pair(pair, src_ref, carry):
        n0 = 2 * pair
        r01 = pl.ds(pl.multiple_of(n0 * blk, 2 * blk), 2 * blk)
        out = []
        for hh, cols in enumerate(heads):
            m_run, acc = carry[2 * hh:2 * hh + 2]
            b0 = bias_ref[hh, pl.ds(n0, 1), :]
            b1 = bias_ref[hh, pl.ds(n0 + 1, 1), :]
            s0 = src_ref[hh, 0:blk, :]
            s1 = src_ref[hh, blk:2 * blk, :]
            m_new = jnp.maximum(m_run, jnp.maximum(jnp.max(s0, axis=0, keepdims=True) + b0,
                                                   jnp.max(s1, axis=0, keepdims=True) + b1))
            alpha = jnp.exp2((m_run - m_new) * c)
            p01 = jnp.concatenate([jnp.exp2((s0 - (m_new - b0)) * c).astype(BF16),
                                   jnp.exp2((s1 - (m_new - b1)) * c).astype(BF16)], axis=0)
            acc = alpha * acc + jnp.dot(vt_ref[hh, :, r01], p01, preferred_element_type=F32)
            out += [m_new, acc]
        return tuple(out)

    def two_pairs(t, carry):
        score_pair(2 * t + 1, sb_ref)
        carry = absorb_pair(2 * t, sa_ref, carry)

        def second(carry):
            score_pair(2 * t + 2, sa_ref)
            return absorb_pair(2 * t + 1, sb_ref, carry)

        return lax.cond(2 * t + 1 < n_pairs, second, lambda carry: carry, carry)

    fin = lax.fori_loop(0, (n_pairs + 1) // 2, two_pairs, tuple(carry0))
    for hh, cols in enumerate(heads):
        acc = fin[2 * hh + 1]
        o_ref[0, :, cols] = (acc[0:dh] / acc[dh:dh + 1]).T.astype(o_ref.dtype)


def moba_attention(qkv):
    b, s, _ = qkv.shape
    n_blocks = s // MOBA_BLOCK
    hw = MOBA_HP * MOBA_DH
    hsteps = MOBA_HEADS // MOBA_HP
    return pl.pallas_call(
        functools.partial(_moba_body, n_blocks=n_blocks),
        grid=(b, hsteps, n_blocks),
        in_specs=[pl.BlockSpec((1, MOBA_BLOCK, hw), lambda bi, h, t: (bi, t, h)),
                  pl.BlockSpec((1, s, hw), lambda bi, h, t: (bi, 0, hsteps + h)),
                  pl.BlockSpec((1, s, hw), lambda bi, h, t: (bi, 0, 2 * hsteps + h))],
        out_specs=pl.BlockSpec((1, MOBA_BLOCK, hw), lambda bi, h, t: (bi, t, h)),
        out_shape=jax.ShapeDtypeStruct((b, s, D_MODEL), BF16),
        scratch_shapes=[pltpu.VMEM((MOBA_HP, n_blocks, MOBA_DH), F32),
                        pltpu.VMEM((MOBA_HP, MOBA_DH + ONES_ROWS, s), BF16),
                        pltpu.VMEM((MOBA_HP, n_blocks, MOBA_BLOCK), F32),
                        pltpu.VMEM((MOBA_HP, 2 * MOBA_BLOCK, MOBA_BLOCK), F32),
                        pltpu.VMEM((MOBA_HP, 2 * MOBA_BLOCK, MOBA_BLOCK), F32)],
        compiler_params=_params("parallel", "parallel", "arbitrary"),
        name="moba_attention",
    )(qkv, qkv, qkv)


def _router_body(h_ref, g_ref, wh_ref, wl_ref, b_ref, idx_ref, wgt_ref, cnt_ref, x3_ref, run_ref):
    step = pl.program_id(0)

    @pl.when(step == 0)
    def _():
        run_ref[...] = jnp.zeros_like(run_ref)

    tm = h_ref.shape[0]
    xn = _rms(h_ref[...], g_ref[...])
    for ch in range(ROW_CHUNKS):
        x3_ref[pl.ds(ch, tm, stride=ROW_CHUNKS), :] = xn[:, ch * LANES:(ch + 1) * LANES]
    x_hi = xn.astype(BF16)
    x_lo = (xn - x_hi.astype(F32)).astype(BF16)
    logits = (jnp.dot(x_hi, wh_ref[...], preferred_element_type=F32)
              + (jnp.dot(x_hi, wl_ref[...], preferred_element_type=F32)
                 + jnp.dot(x_lo, wh_ref[...], preferred_element_type=F32))) + b_ref[...]
    lane = lax.broadcasted_iota(I32, (tm, LANES), 1)

    def first_max(vals):
        top = jnp.max(vals, axis=-1, keepdims=True)
        where = jnp.min(jnp.where(vals == top, lane, LANES), axis=-1, keepdims=True)
        return top, where

    g_logits = jnp.where(lane < N_GROUPS, logits, NEG_INF)
    g_top, grp = first_max(g_logits)
    p_grp = 1.0 / jnp.sum(jnp.exp(g_logits - g_top), axis=-1, keepdims=True)
    lo = EXPERT_LANE0 + grp * EXPERTS_PER_GROUP
    e_logits = jnp.where((lane >= lo) & (lane < lo + EXPERTS_PER_GROUP), logits, NEG_INF)
    v0, j0 = first_max(e_logits)
    v1, j1 = first_max(jnp.where(lane == j0, NEG_INF, e_logits))
    t = jnp.exp(v1 - v0)
    w0 = p_grp / (1.0 + t)
    w1 = p_grp * t / (1.0 + t)

    hit0 = lane == j0
    hit1 = lane == j1
    member = jnp.where(hit0 | hit1, 1.0, 0.0)
    r_i = lax.broadcasted_iota(I32, (tm, tm), 0)
    c_i = lax.broadcasted_iota(I32, (tm, tm), 1)
    before = jnp.where(c_i < r_i, 1.0, 0.0).astype(BF16)
    prior = jnp.dot(before, member.astype(BF16), preferred_element_type=F32) + run_ref[...]
    rank0 = jnp.sum(jnp.where(hit0, prior, 0.0), axis=-1, keepdims=True).astype(I32)
    rank1 = jnp.sum(jnp.where(hit1, prior, 0.0), axis=-1, keepdims=True).astype(I32)
    run_ref[...] = run_ref[...] + jnp.sum(member, axis=0, keepdims=True)
    cnt_ref[...] = run_ref[...]

    idx_ref[...] = jnp.where(lane == 0, j0 - EXPERT_LANE0,
                             jnp.where(lane == 1, j1 - EXPERT_LANE0,
                                       jnp.where(lane == 2, rank0, jnp.where(lane == 3, rank1, 0))))
    wgt_ref[...] = jnp.where(lane == 0, w0, jnp.where(lane == 1, w1, 0.0))


def moe_router(h2d, g, w_cat, b_cat):
    n_tok, d = h2d.shape
    w_hi = w_cat.astype(BF16)
    return pl.pallas_call(
        _router_body,
        grid=(n_tok // ROUTE_TM,),
        in_specs=[pl.BlockSpec((ROUTE_TM, d), lambda i: (i, 0)),
                  pl.BlockSpec((1, d), lambda i: (0, 0)),
                  pl.BlockSpec((d, LANES), lambda i: (0, 0)),
                  pl.BlockSpec((d, LANES), lambda i: (0, 0)),
                  pl.BlockSpec((1, LANES), lambda i: (0, 0))],
        out_specs=[pl.BlockSpec((ROUTE_TM, LANES), lambda i: (i, 0)),
                   pl.BlockSpec((ROUTE_TM, LANES), lambda i: (i, 0)),
                   pl.BlockSpec((1, LANES), lambda i: (0, 0)),
                   pl.BlockSpec((ROUTE_TM * ROW_CHUNKS, LANES), lambda i: (i, 0))],
        out_shape=[jax.ShapeDtypeStruct((n_tok, LANES), I32),
                   jax.ShapeDtypeStruct((n_tok, LANES), F32),
                   jax.ShapeDtypeStruct((1, LANES), F32),
                   jax.ShapeDtypeStruct((n_tok * ROW_CHUNKS, LANES), F32)],
        scratch_shapes=[pltpu.VMEM((1, LANES), F32)],
        compiler_params=_params("arbitrary"),
        name="moe_router",
    )(h2d, g.reshape(1, d), w_hi, (w_cat - w_hi.astype(F32)).astype(BF16), b_cat)


INV_UNROLL = 16
DUMP_ROWS = MOE_BM


def _expert_body(be_ref, nu_ref, dest_ref, gfill_hbm, sfill_hbm, x3_hbm, w1_ref, w3_ref, w2_ref, o2_hbm,
                 gsrc_ref, sdst_ref, xbuf, ybuf, gsem, ssem, isem, w1_bf, w3_bf, w2_bf, *, n_tok, n_blocks):
    i = pl.program_id(0)
    n_used = nu_ref[0]
    slot = i % 2
    plane_rows = (n_tok + DUMP_ROWS) * ROW_CHUNKS

    def gather_row(block, r, to_slot):
        src = pl.multiple_of(gsrc_ref[block * MOE_BM + r], ROW_CHUNKS)
        return pltpu.make_async_copy(x3_hbm.at[pl.ds(src, ROW_CHUNKS)],
                                     xbuf.at[to_slot, pl.ds(r * BUF_PITCH, ROW_CHUNKS)], gsem.at[to_slot])

    def scatter_row(block, r, from_slot):
        dst = pl.multiple_of(sdst_ref[block * MOE_BM + r], ROW_CHUNKS)
        return pltpu.make_async_copy(ybuf.at[from_slot, pl.ds(r * BUF_PITCH, ROW_CHUNKS)],
                                     o2_hbm.at[pl.ds(dst, ROW_CHUNKS)], ssem.at[from_slot])

    block_rows = MOE_BM * ROW_CHUNKS

    def gather_wait(of_slot):
        pltpu.make_async_copy(x3_hbm.at[pl.ds(0, block_rows)], xbuf.at[of_slot, pl.ds(0, block_rows)],
                              gsem.at[of_slot]).wait()

    def scatter_wait(of_slot):
        pltpu.make_async_copy(ybuf.at[of_slot, pl.ds(0, block_rows)], o2_hbm.at[pl.ds(0, block_rows)],
                              ssem.at[of_slot]).wait()

    @pl.when(i == 0)
    def _():
        for fill_hbm, table in ((gfill_hbm, gsrc_ref), (sfill_hbm, sdst_ref)):
            fill = pltpu.make_async_copy(fill_hbm, table, isem)
            fill.start()
            fill.wait()

        def body(c, carry):
            base = c * INV_UNROLL
            rows = [dest_ref[base + u] for u in range(INV_UNROLL)]
            tok_row0 = c * (INV_UNROLL // 2 * ROW_CHUNKS)
            for u in range(INV_UNROLL):
                gsrc_ref[rows[u]] = tok_row0 + (u // 2) * ROW_CHUNKS
                sdst_ref[rows[u]] = tok_row0 + ((u % 2) * plane_rows + (u // 2) * ROW_CHUNKS)
            return carry
        lax.fori_loop(0, dest_ref.shape[0] // INV_UNROLL, body, 0)
        ybuf[...] = jnp.zeros_like(ybuf)
        for plane in range(2):
            init = pltpu.make_async_copy(ybuf.at[plane, pl.ds(0, block_rows)],
                                         o2_hbm.at[pl.ds(plane * plane_rows + n_tok * ROW_CHUNKS, block_rows)], isem)
            init.start()
            init.wait()
        for r in range(MOE_BM):
            gather_row(0, r, 0).start()

    @pl.when(i < n_used)
    def _():
        gather_wait(slot)

        @pl.when((i == 0) | (be_ref[i] != be_ref[jnp.maximum(i - 1, 0)]))
        def _():
            w1_bf[...] = w1_ref[...].astype(BF16)
            w3_bf[...] = w3_ref[...].astype(BF16)
            w2_bf[...] = w2_ref[...].astype(BF16)

        @pl.when(i >= 1)
        def _():
            scatter_wait(slot)

        prev = jnp.where(i == 0, n_blocks - 1, i - 1)
        for r in range(MOE_BM):
            scatter_row(prev, r, 1 - slot).start()
        xn = jnp.concatenate([xbuf[slot, pl.ds(ch, MOE_BM, stride=BUF_PITCH), :] for ch in range(ROW_CHUNKS)],
                             axis=1).astype(BF16)
        for r in range(MOE_BM):
            gather_row(i + 1, r, 1 - slot).start()
        h1 = jnp.dot(xn, w1_bf[...], preferred_element_type=F32)
        h3 = jnp.dot(xn, w3_bf[...], preferred_element_type=F32)
        act = (_silu(h1) * h3).astype(BF16)
        y = jnp.dot(act, w2_bf[...], preferred_element_type=F32)
        for ch in range(ROW_CHUNKS):
            ybuf[slot, pl.ds(ch, MOE_BM, stride=BUF_PITCH), :] = y[:, ch * LANES:(ch + 1) * LANES]

    @pl.when(i == n_used)
    def _():
        for r in range(MOE_BM):
            scatter_row(i - 1, r, 1 - slot).start()
        gather_wait(slot)
        scatter_wait(slot)
        scatter_wait(1 - slot)


def moe_experts(x3, dest_flat, w1, w3, w2, layer, block_e, n_used):
    plane_rows = (x3.shape[0] // ROW_CHUNKS + DUMP_ROWS) * ROW_CHUNKS
    n_tok = x3.shape[0] // ROW_CHUNKS
    d = D_MODEL
    n_blocks = block_e.shape[0]
    n_rows = n_blocks * MOE_BM
    f = w1.shape[-1]

    def w_map(i, be, nu, dest):
        return (layer, be[i], 0, 0)

    grid_spec = pltpu.PrefetchScalarGridSpec(
        num_scalar_prefetch=3,
        grid=(n_blocks,),
        in_specs=[pl.BlockSpec(memory_space=pl.ANY),
                  pl.BlockSpec(memory_space=pl.ANY),
                  pl.BlockSpec(memory_space=pl.ANY),
                  pl.BlockSpec((None, None, d, f), w_map),
                  pl.BlockSpec((None, None, d, f), w_map),
                  pl.BlockSpec((None, None, f, d), w_map)],
        out_specs=pl.BlockSpec(memory_space=pl.ANY),
        scratch_shapes=[pltpu.SMEM((n_rows,), I32),
                        pltpu.SMEM((n_rows,), I32),
                        pltpu.VMEM((2, MOE_BM * BUF_PITCH, LANES), F32),
                        pltpu.VMEM((2, MOE_BM * BUF_PITCH, LANES), F32),
                        pltpu.SemaphoreType.DMA((2,)),
                        pltpu.SemaphoreType.DMA((2,)),
                        pltpu.SemaphoreType.DMA(()),
                        pltpu.VMEM((d, f), BF16), pltpu.VMEM((d, f), BF16), pltpu.VMEM((f, d), BF16)],
    )
    pad_row = jnp.arange(n_rows, dtype=I32) % MOE_BM
    o2 = pl.pallas_call(
        functools.partial(_expert_body, n_tok=n_tok, n_blocks=n_blocks),
        grid_spec=grid_spec,
        out_shape=jax.ShapeDtypeStruct((2 * plane_rows, LANES), F32),
        compiler_params=_params("arbitrary"),
        name="moe_experts",
    )(block_e, n_used, dest_flat, jnp.zeros((n_rows,), I32), (n_tok + pad_row) * ROW_CHUNKS, x3, w1, w3, w2)
    return o2.reshape(2, plane_rows, LANES)


def _combine_body(*refs, final_norm):
    if final_norm:
        h_ref, wgt_ref, o2_ref, gf_ref, o_ref = refs
    else:
        h_ref, wgt_ref, o2_ref, o_ref = refs
    wgt = wgt_ref[...]
    y0, y1 = (jnp.concatenate([o2_ref[s, pl.ds(ch, COMBINE_T, stride=ROW_CHUNKS), :] for ch in range(ROW_CHUNKS)], axis=1)
              for s in range(2))
    out = h_ref[...] + wgt[:, 0:1] * y0 + wgt[:, 1:2] * y1
    if final_norm:
        out = _rms(out, gf_ref[...])
    o_ref[...] = out


def moe_combine(h2d, wgt, o2, g_final=None):
    n_tok, d = h2d.shape
    final_norm = g_final is not None
    in_specs = [pl.BlockSpec((COMBINE_T, d), lambda i: (i, 0)),
                pl.BlockSpec((COMBINE_T, LANES), lambda i: (i, 0)),
                pl.BlockSpec((2, COMBINE_T * ROW_CHUNKS, LANES), lambda i: (0, i, 0))]
    args = [h2d, wgt, o2]
    if final_norm:
        in_specs.append(pl.BlockSpec((1, d), lambda i: (0, 0)))
        args.append(g_final.reshape(1, d))
    return pl.pallas_call(
        functools.partial(_combine_body, final_norm=final_norm),
        grid=(n_tok // COMBINE_T,),
        in_specs=in_specs,
        out_specs=pl.BlockSpec((COMBINE_T, d), lambda i: (i, 0)),
        out_shape=jax.ShapeDtypeStruct((n_tok, d), F32),
        compiler_params=_params("parallel"),
        name="moe_combine_final" if final_norm else "moe_combine",
    )(*args)


def hierarchical_moe(h2d, layer, norm_g, w_group, b_group, w_router, b_router, w1, w3, w2, g_final=None):
    n_tok, d = h2d.shape
    pad_l = LANES - N_GROUPS - N_EXPERTS
    w_cat = jnp.concatenate([w_group, w_router, jnp.zeros((d, pad_l), F32)], axis=1)
    b_cat = jnp.concatenate([b_group, b_router, jnp.zeros((pad_l,), F32)]).reshape(1, LANES)
    idx, wgt, cnt, x3 = moe_router(h2d, norm_g, w_cat, b_cat)

    counts = cnt[0, EXPERT_LANE0:EXPERT_LANE0 + N_EXPERTS].astype(I32)
    padded = (counts + MOE_BM - 1) // MOE_BM * MOE_BM
    pends = jnp.cumsum(padded)
    pstarts = pends - padded
    n_blocks = (n_tok * 2) // MOE_BM + N_EXPERTS
    blk_row0 = jnp.arange(n_blocks, dtype=I32) * MOE_BM
    block_e = jnp.minimum(jnp.sum(pends[None, :] <= blk_row0[:, None], axis=1), N_EXPERTS - 1).astype(I32)
    n_used = (pends[-1:] // MOE_BM).astype(I32)
    e_iota = jnp.arange(N_EXPERTS, dtype=I32)
    row0 = jnp.sum(jnp.where(idx[:, 0:2, None] == e_iota, pstarts, 0), axis=-1)
    dest = (row0 + idx[:, 2:4]).astype(I32).reshape(-1)

    o2 = moe_experts(x3, dest, w1, w3, w2, layer, block_e, n_used)
    return moe_combine(h2d, wgt, o2, g_final)


def kernel(x, norm_mix, norm_ffn, norm_final, w_in_even, hg_lb_logits, hg_norm, gla_gk2, gla_gk_bias,
           gla_norm, w_out_even, w_qkv_odd, w_o_odd, router_group_w, router_group_b, router_expert_w,
           router_expert_b, expert_w1, expert_w3, expert_w2):
    b, s, d = x.shape
    n_tok = b * s
    depth = norm_mix.shape[0]
    lb_table = jnp.cumsum(jax.nn.softmax(hg_lb_logits.astype(F32), axis=0), axis=0)
    h = x.reshape(n_tok, d)
    for l in range(depth):
        if l % 2 == 0:
            e = l // 2
            w_in = w_in_even[e]
            w_low = jnp.pad(w_in[:, MAIN_IN:], ((0, 0), (0, LANES - GLA_RANK))).astype(BF16)
            proj, glow = norm_matmul(h, norm_mix[l], w_in.astype(BF16), w_low, tm=PROJ_TM, tn=PROJ_TN, n=MAIN_IN)
            proj = proj.reshape(b, s, MAIN_IN)
            gk2_pad = jnp.pad(gla_gk2[e], ((0, LANES - GLA_RANK), (0, 0)))
            o_hg = hgrn2_mix(proj, lb_table[l], hg_norm[e])
            o_gla = gla_mix(proj, glow.reshape(b, s, LANES), gk2_pad, gla_gk_bias[e], gla_norm[e])
            h = matmul_residual([o_hg.reshape(n_tok, -1), o_gla.reshape(n_tok, -1)], w_out_even[e].astype(BF16), h,
                                tm=OUT_TM)
        else:
            o = l // 2
            qkv = norm_matmul(h, norm_mix[l], w_qkv_odd[o].astype(BF16), tm=PROJ_TM, tn=PROJ_TN)
            attn = moba_attention(qkv.reshape(b, s, 3 * d))
            h = matmul_residual([attn.reshape(n_tok, d)], w_o_odd[o].astype(BF16), h, tm=OUT_TM)
        h = hierarchical_moe(h, l, norm_ffn[l], router_group_w[l], router_group_b[l], router_expert_w[l],
                             router_expert_b[l], expert_w1, expert_w3, expert_w2,
                             g_final=norm_final if l == depth - 1 else None)
    return h.reshape(b, s, d)
```

```python
import functools

import jax
import jax.numpy as jnp
from jax import lax
from jax.experimental import pallas as pl
from jax.experimental.pallas import tpu as pltpu

F32 = jnp.float32
BF16 = jnp.bfloat16
I32 = jnp.int32
HIGHEST = lax.Precision.HIGHEST
NEG_INF = float("-inf")
LOG2_E = 1.4426950408889634

EPS = 1e-6
D_MODEL = 2048

HG_HEADS, HG_DK, HG_DV = 8, 128, 128
GLA_HEADS, GLA_DK, GLA_DV = 4, 128, 256
GLA_RANK = 16
GLA_GATE_NORM = 16.0
HG_QK = HG_HEADS * HG_DK
MAIN_IN = 4 * HG_QK + 2 * GLA_HEADS * GLA_DK + 2 * GLA_HEADS * GLA_DV
CHUNK = 64
SUB = 8
REC_T = 256
HG_HP, GLA_HP = HG_HEADS, GLA_HEADS

MOBA_HEADS, MOBA_DH = 16, 128
MOBA_BLOCK = 256
MOBA_TOPK = 3
MOBA_HP = 4
ONES_ROWS = 16
MASK_BIAS = -1e30

N_GROUPS, EXPERTS_PER_GROUP = 4, 8
N_EXPERTS = N_GROUPS * EXPERTS_PER_GROUP
D_EXPERT = D_MODEL // 4
EXPERT_LANE0 = N_GROUPS
ROUTE_TM = 256
MOE_BM = 256
COMBINE_T = 512
ROW_CHUNKS = D_MODEL // 128
BUF_PITCH = ROW_CHUNKS + 8

PROJ_TM = 1024
IN_TN, QKV_TN = 1792, 2048
OUT_TM = 512

LANES = 128
VMEM_LIMIT = 56 * 1024 * 1024

NT_DIMS = (((1,), (1,)), ((), ()))
TN_DIMS = (((0,), (0,)), ((), ()))


def _params(*sem):
    return pltpu.CompilerParams(dimension_semantics=sem, vmem_limit_bytes=VMEM_LIMIT)


def _rms(x, g):
    return x * lax.rsqrt(jnp.mean(x * x, axis=-1, keepdims=True) + EPS) * g


def _silu(x):
    return x * jax.nn.sigmoid(x)


def _norm_matmul_body(*refs, has_side):
    if has_side:
        x_ref, g_ref, w_ref, ws_ref, o_ref, os_ref, xn_ref = refs
    else:
        x_ref, g_ref, w_ref, o_ref, xn_ref = refs

    @pl.when(pl.program_id(1) == 0)
    def _():
        xn_ref[...] = _rms(x_ref[...], g_ref[...]).astype(BF16)
        if has_side:
            os_ref[...] = jnp.dot(xn_ref[...], ws_ref[...], preferred_element_type=F32)

    o_ref[...] = jnp.dot(xn_ref[...], w_ref[...], preferred_element_type=F32).astype(o_ref.dtype)


def norm_matmul(x, g, w, w_side=None, *, tm, tn, n=None):
    m, k = x.shape
    n = w.shape[1] if n is None else n
    has_side = w_side is not None
    in_specs = [pl.BlockSpec((tm, k), lambda i, j: (i, 0)),
                pl.BlockSpec((1, k), lambda i, j: (0, 0)),
                pl.BlockSpec((k, tn), lambda i, j: (0, j))]
    out_specs = [pl.BlockSpec((tm, tn), lambda i, j: (i, j))]
    out_shape = [jax.ShapeDtypeStruct((m, n), BF16)]
    args = [x, g.reshape(1, k), w]
    if has_side:
        ns = w_side.shape[1]
        in_specs.append(pl.BlockSpec((k, ns), lambda i, j: (0, 0)))
        out_specs.append(pl.BlockSpec((tm, ns), lambda i, j: (i, 0)))
        out_shape.append(jax.ShapeDtypeStruct((m, ns), F32))
        args.append(w_side)
    outs = pl.pallas_call(
        functools.partial(_norm_matmul_body, has_side=has_side),
        grid=(m // tm, n // tn),
        in_specs=in_specs, out_specs=out_specs, out_shape=out_shape,
        scratch_shapes=[pltpu.VMEM((tm, k), BF16)],
        compiler_params=_params("parallel", "arbitrary"),
        name="norm_matmul_side" if has_side else "norm_matmul",
    )(*args)
    return outs if has_side else outs[0]


def _matmul_res_body(*refs):
    *a_refs, w_ref, r_ref, o_ref = refs
    acc = r_ref[...]
    k0 = 0
    for a_ref in a_refs:
        kp = a_ref.shape[1]
        acc = acc + jnp.dot(a_ref[...], w_ref[k0:k0 + kp, :], preferred_element_type=F32)
        k0 += kp
    o_ref[...] = acc


def matmul_residual(a_pieces, w, res, *, tm):
    m = res.shape[0]
    k, n = w.shape
    return pl.pallas_call(
        _matmul_res_body,
        grid=(m // tm,),
        in_specs=[pl.BlockSpec((tm, a.shape[1]), lambda i: (i, 0)) for a in a_pieces]
        + [pl.BlockSpec((k, n), lambda i: (0, 0)),
           pl.BlockSpec((tm, n), lambda i: (i, 0))],
        out_specs=pl.BlockSpec((tm, n), lambda i: (i, 0)),
        out_shape=jax.ShapeDtypeStruct((m, n), F32),
        compiler_params=_params("parallel"),
        name="matmul_residual",
    )(*a_pieces, w, res)


def _recurrence_levels(q, k, v, la, st_ref):
    n_ch = REC_T // CHUNK
    n_sub = CHUNK // SUB
    dv = v.shape[1]
    r_i = lax.broadcasted_iota(I32, (CHUNK, CHUNK), 0)
    c_i = lax.broadcasted_iota(I32, (CHUNK, CHUNK), 1)
    tri = (c_i <= r_i).astype(F32)
    local = [jnp.dot(tri, la[c * CHUNK:(c + 1) * CHUNK], precision=HIGHEST, preferred_element_type=F32)
             for c in range(n_ch)]
    cums = [local[0]]
    for c in range(1, n_ch):
        cums.append(local[c] + cums[-1][CHUNK - 1:CHUNK, :])
    cum = jnp.concatenate(cums, axis=0)
    last = cum[REC_T - 1:REC_T, :]
    vb = v.astype(BF16)
    st = st_ref[...]
    yield

    o_inter = lax.dot_general((q * jnp.exp(cum)).astype(BF16), st.astype(BF16), NT_DIMS,
                              preferred_element_type=F32)
    kv = lax.dot_general(vb, (k * jnp.exp(last - cum)).astype(BF16), TN_DIMS, preferred_element_type=F32)
    rows = lax.broadcasted_iota(I32, (SUB, 1), 0)
    ones = jnp.ones((q.shape[1], LANES), BF16)
    s_cross, s_sub, diag = {}, {}, {}
    for c in range(n_ch):
        c0 = c * CHUNK
        if c > 0:
            ref_pt = cum[c0 - 1:c0, :]
            q_t = (q[c0:c0 + CHUNK] * jnp.exp(cum[c0:c0 + CHUNK] - ref_pt)).astype(BF16)
            k_t = (k[:c0] * jnp.exp(ref_pt - cum[:c0])).astype(BF16)
            s_cross[c] = lax.dot_general(q_t, k_t, NT_DIMS, preferred_element_type=F32)
        for b in range(n_sub):
            lo = c0 + b * SUB
            q_b, k_b, c_b = q[lo:lo + SUB], k[lo:lo + SUB], cum[lo:lo + SUB]
            if b > 0:
                ref_pt = cum[lo - 1:lo, :]
                q_t = (q_b * jnp.exp(c_b - ref_pt)).astype(BF16)
                k_t = (k[c0:lo] * jnp.exp(ref_pt - cum[c0:lo])).astype(BF16)
                s_sub[c, b] = lax.dot_general(q_t, k_t, NT_DIMS, preferred_element_type=F32)
            terms = []
            for j in range(SUB):
                e = jnp.exp(jnp.where(rows >= j, c_b - c_b[j:j + 1, :], NEG_INF))
                terms.append(q_b * k_b[j:j + 1, :] * e)
            diag[c, b] = jnp.dot(jnp.concatenate(terms, axis=0).astype(BF16), ones, preferred_element_type=F32)

    yield
    cross = {c: jnp.dot(s_cross[c].astype(BF16), vb[:c * CHUNK], preferred_element_type=F32) for c in s_cross}
    sub = {cb: jnp.dot(s_sub[cb].astype(BF16), v[cb[0] * CHUNK:cb[0] * CHUNK + cb[1] * SUB].astype(BF16),
                       preferred_element_type=F32) for cb in s_sub}
    st_ref[...] = st * jnp.exp(last) + kv
    yield

    outs = []
    for c in range(n_ch):
        for b in range(n_sub):
            lo = c * CHUNK + b * SUB
            acc = o_inter[lo:lo + SUB]
            if c > 0:
                acc = acc + cross[c][b * SUB:(b + 1) * SUB]
            if b > 0:
                acc = acc + sub[c, b]
            v_b = v[lo:lo + SUB]
            for j in range(SUB):
                col = diag[c, b][j * SUB:(j + 1) * SUB]
                if dv > LANES:
                    col = jnp.concatenate([col] * (dv // LANES), axis=1)
                acc = acc + col * v_b[j:j + 1, :]
            outs.append(acc)
    return jnp.concatenate(outs, axis=0)


def _run_heads(steps):
    results = [None] * len(steps)
    live = list(range(len(steps)))
    while live:
        for n in list(live):
            try:
                next(steps[n])
            except StopIteration as done:
                results[n] = done.value
                live.remove(n)
    return results


def _log_sigmoid(z):
    return jnp.minimum(z, 0.0) - jnp.log(1.0 + jnp.exp(-jnp.abs(z)))


def _hgrn2_body(hq_ref, hf_ref, hi_ref, hg_ref, lb_ref, nw_ref, o_ref, st_ref, *, hp):
    @pl.when(pl.program_id(2) == 0)
    def _():
        st_ref[...] = jnp.zeros_like(st_ref)

    steps = []
    for hh in range(hp):
        cols = slice(hh * HG_DK, (hh + 1) * HG_DK)
        lb = lb_ref[:, cols]
        forget = lb + (1.0 - lb) * jax.nn.sigmoid(hf_ref[0, :, cols].astype(F32))
        q = _silu(hq_ref[0, :, cols].astype(F32)) * (HG_DK ** -0.5)
        steps.append(_recurrence_levels(q, 1.0 - forget, hi_ref[0, :, cols].astype(F32), jnp.log(forget),
                                        st_ref.at[hh]))
    for hh, o in enumerate(_run_heads(steps)):
        cols = slice(hh * HG_DV, (hh + 1) * HG_DV)
        o_ref[0, :, cols] = (_rms(o, nw_ref[...]) * _silu(hg_ref[0, :, cols].astype(F32))).astype(o_ref.dtype)


def _gla_body(gq_ref, gk_ref, gv_ref, gg_ref, glow_ref, gk2_ref, gb_ref, nw_ref, o_ref, st_ref, *, hp):
    @pl.when(pl.program_id(2) == 0)
    def _():
        st_ref[...] = jnp.zeros_like(st_ref)

    z = jnp.dot(glow_ref[0], gk2_ref[...], precision=HIGHEST, preferred_element_type=F32) + gb_ref[...]
    la = _log_sigmoid(z) / GLA_GATE_NORM
    steps = []
    for hh in range(hp):
        kc = slice(hh * GLA_DK, (hh + 1) * GLA_DK)
        vc = slice(hh * GLA_DV, (hh + 1) * GLA_DV)
        q = gq_ref[0, :, kc].astype(F32) * (GLA_DK ** -0.5)
        steps.append(_recurrence_levels(q, gk_ref[0, :, kc].astype(F32), gv_ref[0, :, vc].astype(F32), la[:, kc],
                                        st_ref.at[hh]))
    for hh, o in enumerate(_run_heads(steps)):
        vc = slice(hh * GLA_DV, (hh + 1) * GLA_DV)
        o_ref[0, :, vc] = (_rms(o, nw_ref[...]) * _silu(gg_ref[0, :, vc].astype(F32))).astype(o_ref.dtype)


def hgrn2_mix(proj, lb, hg_norm):
    b, s, _ = proj.shape

    hp = HG_HP
    hsteps = HG_HEADS // hp

    def col(base):
        return pl.BlockSpec((1, REC_T, hp * HG_DK), lambda bi, h, t: (bi, t, base + h))

    return pl.pallas_call(
        functools.partial(_hgrn2_body, hp=hp),
        grid=(b, hsteps, s // REC_T),
        in_specs=[col(0), col(hsteps), col(2 * hsteps), col(3 * hsteps),
                  pl.BlockSpec((1, hp * HG_DK), lambda bi, h, t: (0, h)),
                  pl.BlockSpec((1, HG_DV), lambda bi, h, t: (0, 0))],
        out_specs=pl.BlockSpec((1, REC_T, hp * HG_DV), lambda bi, h, t: (bi, t, h)),
        out_shape=jax.ShapeDtypeStruct((b, s, HG_HEADS * HG_DV), BF16),
        scratch_shapes=[pltpu.VMEM((hp, HG_DV, HG_DK), F32)],
        compiler_params=_params("parallel", "parallel", "arbitrary"),
        name="hgrn2_mix",
    )(proj, proj, proj, proj, lb.reshape(1, HG_QK), hg_norm.reshape(1, HG_DV))


def gla_mix(proj, glow, gk2_pad, gk_bias, gla_norm):
    b, s, _ = proj.shape
    hp = GLA_HP
    hsteps = GLA_HEADS // hp
    kw, vw = hp * GLA_DK, hp * GLA_DV
    q0 = 4 * HG_QK // kw
    k0 = q0 + hsteps
    v0 = (4 * HG_QK + 2 * GLA_HEADS * GLA_DK) // vw
    g0 = v0 + hsteps
    return pl.pallas_call(
        functools.partial(_gla_body, hp=hp),
        grid=(b, hsteps, s // REC_T),
        in_specs=[pl.BlockSpec((1, REC_T, kw), lambda bi, h, t: (bi, t, q0 + h)),
                  pl.BlockSpec((1, REC_T, kw), lambda bi, h, t: (bi, t, k0 + h)),
                  pl.BlockSpec((1, REC_T, vw), lambda bi, h, t: (bi, t, v0 + h)),
                  pl.BlockSpec((1, REC_T, vw), lambda bi, h, t: (bi, t, g0 + h)),
                  pl.BlockSpec((1, REC_T, LANES), lambda bi, h, t: (bi, t, 0)),
                  pl.BlockSpec((LANES, kw), lambda bi, h, t: (0, h)),
                  pl.BlockSpec((1, kw), lambda bi, h, t: (0, h)),
                  pl.BlockSpec((1, GLA_DV), lambda bi, h, t: (0, 0))],
        out_specs=pl.BlockSpec((1, REC_T, vw), lambda bi, h, t: (bi, t, h)),
        out_shape=jax.ShapeDtypeStruct((b, s, GLA_HEADS * GLA_DV), BF16),
        scratch_shapes=[pltpu.VMEM((hp, GLA_DV, GLA_DK), F32)],
        compiler_params=_params("parallel", "parallel", "arbitrary"),
        name="gla_mix",
    )(proj, proj, proj, proj, glow, gk2_pad, gk_bias.reshape(1, -1), gla_norm.reshape(1, GLA_DV))


def _moba_body(q_ref, k_ref, v_ref, o_ref, kmean_ref, vt_ref, bias_ref, sa_ref, sb_ref, *, n_blocks):
    qi = pl.program_id(2)
    blk, dh = MOBA_BLOCK, MOBA_DH
    heads = [slice(hh * dh, (hh + 1) * dh) for hh in range(MOBA_HP)]

    @pl.when(qi == 0)
    def _():
        for n in range(n_blocks):
            rs = slice(n * blk, (n + 1) * blk)
            for hh, cols in enumerate(heads):
                kmean_ref[hh, n:n + 1, :] = jnp.mean(k_ref[0, rs, cols].astype(F32), axis=0, keepdims=True)
                vt_ref[hh, 0:dh, rs] = v_ref[0, rs, cols].astype(F32).T.astype(BF16)
                vt_ref[hh, dh:dh + ONES_ROWS, rs] = jnp.ones((ONES_ROWS, blk), BF16)

    c = (MOBA_DH ** -0.5) * LOG2_E
    own = pl.ds(pl.multiple_of(qi * blk, blk), blk)
    brow = lax.broadcasted_iota(I32, (n_blocks, blk), 0)
    past = brow < qi
    k_i = lax.broadcasted_iota(I32, (blk, blk), 0)
    q_i = lax.broadcasted_iota(I32, (blk, blk), 1)

    qs = [q_ref[0, :, cols] for cols in heads]
    gates = [lax.dot_general(kmean_ref[hh], qs[hh].astype(F32), NT_DIMS, precision=HIGHEST,
                             preferred_element_type=F32) for hh in range(MOBA_HP)]
    own_s = [lax.dot_general(k_ref[0, own, cols], qs[hh], NT_DIMS, preferred_element_type=F32)
             for hh, cols in enumerate(heads)]
    r01_first = pl.ds(0, 2 * blk)
    for hh, cols in enumerate(heads):
        sa_ref[hh] = lax.dot_general(k_ref[0, r01_first, cols], qs[hh], NT_DIMS, preferred_element_type=F32)
    carry0 = []
    for hh in range(MOBA_HP):
        gate = jnp.where(past, gates[hh], NEG_INF)
        rank = jnp.zeros((n_blocks, blk), I32)
        for m in range(n_blocks):
            gm = gate[m:m + 1, :]
            rank = rank + jnp.where(gm > gate, 1, jnp.where(gm == gate, jnp.where(brow > m, 1, 0), 0))
        keep = jnp.where(past, jnp.where(rank < MOBA_TOPK, 1, 0), 0)
        bias_ref[hh] = jnp.where(keep > 0, 0.0, MASK_BIAS)
    for hh in range(MOBA_HP):
        s = jnp.where(k_i <= q_i, own_s[hh], NEG_INF)
        m0 = jnp.max(s, axis=0, keepdims=True)
        p = jnp.exp2((s - m0) * c)
        acc0 = jnp.dot(vt_ref[hh, :, own], p.astype(BF16), preferred_element_type=F32)
        carry0 += [m0, acc0]

    n_pairs = (qi + 1) // 2

    def score_pair(pair, dst_ref):
        p = jnp.minimum(pair, n_blocks // 2 - 1)
        r01 = pl.ds(pl.multiple_of(p * 2 * blk, 2 * blk), 2 * blk)
        for hh, cols in enumerate(heads):
            dst_ref[hh] = lax.dot_general(k_ref[0, r01, cols], qs[hh], NT_DIMS, preferred_element_type=F32)

    def absorb_pair(pair, src_ref, carry):
        n0 = 2 * pair
        r01 = pl.ds(pl.multiple_of(n0 * blk, 2 * blk), 2 * blk)
        out = []
        for hh, cols in enumerate(heads):
            m_run, acc = carry[2 * hh:2 * hh + 2]
            b0 = bias_ref[hh, pl.ds(n0, 1), :]
            b1 = bias_ref[hh, pl.ds(n0 + 1, 1), :]
            s0 = src_ref[hh, 0:blk, :]
            s1 = src_ref[hh, blk:2 * blk, :]
            m_new = jnp.maximum(m_run, jnp.maximum(jnp.max(s0, axis=0, keepdims=True) + b0,
                                                   jnp.max(s1, axis=0, keepdims=True) + b1))
            alpha = jnp.exp2((m_run - m_new) * c)
            p01 = jnp.concatenate([jnp.exp2((s0 - (m_new - b0)) * c).astype(BF16),
                                   jnp.exp2((s1 - (m_new - b1)) * c).astype(BF16)], axis=0)
            acc = alpha * acc + jnp.dot(vt_ref[hh, :, r01], p01, preferred_element_type=F32)
            out += [m_new, acc]
        return tuple(out)

    def two_pairs(t, carry):
        score_pair(2 * t + 1, sb_ref)
        carry = absorb_pair(2 * t, sa_ref, carry)

        def second(carry):
            score_pair(2 * t + 2, sa_ref)
            return absorb_pair(2 * t + 1, sb_ref, carry)

        return lax.cond(2 * t + 1 < n_pairs, second, lambda carry: carry, carry)

    fin = lax.fori_loop(0, (n_pairs + 1) // 2, two_pairs, tuple(carry0))
    for hh, cols in enumerate(heads):
        acc = fin[2 * hh + 1]
        o_ref[0, :, cols] = (acc[0:dh] / acc[dh:dh + 1]).T.astype(o_ref.dtype)


def moba_attention(qkv):
    b, s, _ = qkv.shape
    n_blocks = s // MOBA_BLOCK
    hw = MOBA_HP * MOBA_DH
    hsteps = MOBA_HEADS // MOBA_HP
    return pl.pallas_call(
        functools.partial(_moba_body, n_blocks=n_blocks),
        grid=(b, hsteps, n_blocks),
        in_specs=[pl.BlockSpec((1, MOBA_BLOCK, hw), lambda bi, h, t: (bi, t, h)),
                  pl.BlockSpec((1, s, hw), lambda bi, h, t: (bi, 0, hsteps + h)),
                  pl.BlockSpec((1, s, hw), lambda bi, h, t: (bi, 0, 2 * hsteps + h))],
        out_specs=pl.BlockSpec((1, MOBA_BLOCK, hw), lambda bi, h, t: (bi, t, h)),
        out_shape=jax.ShapeDtypeStruct((b, s, D_MODEL), BF16),
        scratch_shapes=[pltpu.VMEM((MOBA_HP, n_blocks, MOBA_DH), F32),
                        pltpu.VMEM((MOBA_HP, MOBA_DH + ONES_ROWS, s), BF16),
                        pltpu.VMEM((MOBA_HP, n_blocks, MOBA_BLOCK), F32),
                        pltpu.VMEM((MOBA_HP, 2 * MOBA_BLOCK, MOBA_BLOCK), F32),
                        pltpu.VMEM((MOBA_HP, 2 * MOBA_BLOCK, MOBA_BLOCK), F32)],
        compiler_params=_params("parallel", "parallel", "arbitrary"),
        name="moba_attention",
    )(qkv, qkv, qkv)


def _router_body(h_ref, g_ref, wh_ref, wl_ref, b_ref, idx_ref, wgt_ref, cnt_ref, x3_ref, run_ref):
    step = pl.program_id(0)

    @pl.when(step == 0)
    def _():
        run_ref[...] = jnp.zeros_like(run_ref)

    tm = h_ref.shape[0]
    xn = _rms(h_ref[...], g_ref[...])
    for ch in range(ROW_CHUNKS):
        x3_ref[pl.ds(ch, tm, stride=ROW_CHUNKS), :] = xn[:, ch * LANES:(ch + 1) * LANES]
    x_hi = xn.astype(BF16)
    x_lo = (xn - x_hi.astype(F32)).astype(BF16)
    logits = (jnp.dot(x_hi, wh_ref[...], preferred_element_type=F32)
              + (jnp.dot(x_hi, wl_ref[...], preferred_element_type=F32)
                 + jnp.dot(x_lo, wh_ref[...], preferred_element_type=F32))) + b_ref[...]
    lane = lax.broadcasted_iota(I32, (tm, LANES), 1)

    def first_max(vals):
        top = jnp.max(vals, axis=-1, keepdims=True)
        where = jnp.min(jnp.where(vals == top, lane, LANES), axis=-1, keepdims=True)
        return top, where

    g_logits = jnp.where(lane < N_GROUPS, logits, NEG_INF)
    g_top, grp = first_max(g_logits)
    p_grp = 1.0 / jnp.sum(jnp.exp(g_logits - g_top), axis=-1, keepdims=True)
    lo = EXPERT_LANE0 + grp * EXPERTS_PER_GROUP
    e_logits = jnp.where((lane >= lo) & (lane < lo + EXPERTS_PER_GROUP), logits, NEG_INF)
    v0, j0 = first_max(e_logits)
    v1, j1 = first_max(jnp.where(lane == j0, NEG_INF, e_logits))
    t = jnp.exp(v1 - v0)
    w0 = p_grp / (1.0 + t)
    w1 = p_grp * t / (1.0 + t)

    hit0 = lane == j0
    hit1 = lane == j1
    member = jnp.where(hit0 | hit1, 1.0, 0.0)
    r_i = lax.broadcasted_iota(I32, (tm, tm), 0)
    c_i = lax.broadcasted_iota(I32, (tm, tm), 1)
    before = jnp.where(c_i < r_i, 1.0, 0.0).astype(BF16)
    prior = jnp.dot(before, member.astype(BF16), preferred_element_type=F32) + run_ref[...]
    rank0 = jnp.sum(jnp.where(hit0, prior, 0.0), axis=-1, keepdims=True).astype(I32)
    rank1 = jnp.sum(jnp.where(hit1, prior, 0.0), axis=-1, keepdims=True).astype(I32)
    run_ref[...] = run_ref[...] + jnp.sum(member, axis=0, keepdims=True)
    cnt_ref[...] = run_ref[...]

    idx_ref[...] = jnp.where(lane == 0, j0 - EXPERT_LANE0,
                             jnp.where(lane == 1, j1 - EXPERT_LANE0,
                                       jnp.where(lane == 2, rank0, jnp.where(lane == 3, rank1, 0))))
    wgt_ref[...] = jnp.where(lane == 0, w0, jnp.where(lane == 1, w1, 0.0))


def moe_router(h2d, g, w_cat, b_cat):
    n_tok, d = h2d.shape
    w_hi = w_cat.astype(BF16)
    return pl.pallas_call(
        _router_body,
        grid=(n_tok // ROUTE_TM,),
        in_specs=[pl.BlockSpec((ROUTE_TM, d), lambda i: (i, 0)),
                  pl.BlockSpec((1, d), lambda i: (0, 0)),
                  pl.BlockSpec((d, LANES), lambda i: (0, 0)),
                  pl.BlockSpec((d, LANES), lambda i: (0, 0)),
                  pl.BlockSpec((1, LANES), lambda i: (0, 0))],
        out_specs=[pl.BlockSpec((ROUTE_TM, LANES), lambda i: (i, 0)),
                   pl.BlockSpec((ROUTE_TM, LANES), lambda i: (i, 0)),
                   pl.BlockSpec((1, LANES), lambda i: (0, 0)),
                   pl.BlockSpec((ROUTE_TM * ROW_CHUNKS, LANES), lambda i: (i, 0))],
        out_shape=[jax.ShapeDtypeStruct((n_tok, LANES), I32),
                   jax.ShapeDtypeStruct((n_tok, LANES), F32),
                   jax.ShapeDtypeStruct((1, LANES), F32),
                   jax.ShapeDtypeStruct((n_tok * ROW_CHUNKS, LANES), F32)],
        scratch_shapes=[pltpu.VMEM((1, LANES), F32)],
        compiler_params=_params("arbitrary"),
        name="moe_router",
    )(h2d, g.reshape(1, d), w_hi, (w_cat - w_hi.astype(F32)).astype(BF16), b_cat)


INV_UNROLL = 16
DUMP_ROWS = MOE_BM


def _expert_body(be_ref, nu_ref, dest_ref, gfill_hbm, sfill_hbm, x3_hbm, w1_ref, w3_ref, w2_ref, o2_hbm,
                 gsrc_ref, sdst_ref, xbuf, ybuf, gsem, ssem, isem, w1_bf, w3_bf, w2_bf, *, n_tok, n_blocks):
    i = pl.program_id(0)
    n_used = nu_ref[0]
    slot = i % 2
    plane_rows = (n_tok + DUMP_ROWS) * ROW_CHUNKS

    def gather_row(block, r, to_slot):
        src = pl.multiple_of(gsrc_ref[block * MOE_BM + r], ROW_CHUNKS)
        return pltpu.make_async_copy(x3_hbm.at[pl.ds(src, ROW_CHUNKS)],
                                     xbuf.at[to_slot, pl.ds(r * BUF_PITCH, ROW_CHUNKS)], gsem.at[to_slot])

    def scatter_row(block, r, from_slot):
        dst = pl.multiple_of(sdst_ref[block * MOE_BM + r], ROW_CHUNKS)
        return pltpu.make_async_copy(ybuf.at[from_slot, pl.ds(r * BUF_PITCH, ROW_CHUNKS)],
                                     o2_hbm.at[pl.ds(dst, ROW_CHUNKS)], ssem.at[from_slot])

    block_rows = MOE_BM * ROW_CHUNKS

    def gather_wait(of_slot):
        pltpu.make_async_copy(x3_hbm.at[pl.ds(0, block_rows)], xbuf.at[of_slot, pl.ds(0, block_rows)],
                              gsem.at[of_slot]).wait()

    def scatter_wait(of_slot):
        pltpu.make_async_copy(ybuf.at[of_slot, pl.ds(0, block_rows)], o2_hbm.at[pl.ds(0, block_rows)],
                              ssem.at[of_slot]).wait()

    @pl.when(i == 0)
    def _():
        for fill_hbm, table in ((gfill_hbm, gsrc_ref), (sfill_hbm, sdst_ref)):
            fill = pltpu.make_async_copy(fill_hbm, table, isem)
            fill.start()
            fill.wait()

        def body(c, carry):
            base = c * INV_UNROLL
            rows = [dest_ref[base + u] for u in range(INV_UNROLL)]
            tok_row0 = c * (INV_UNROLL // 2 * ROW_CHUNKS)
            for u in range(INV_UNROLL):
                gsrc_ref[rows[u]] = tok_row0 + (u // 2) * ROW_CHUNKS
                sdst_ref[rows[u]] = tok_row0 + ((u % 2) * plane_rows + (u // 2) * ROW_CHUNKS)
            return carry
        lax.fori_loop(0, dest_ref.shape[0] // INV_UNROLL, body, 0)
        ybuf[...] = jnp.zeros_like(ybuf)
        for plane in range(2):
            init = pltpu.make_async_copy(ybuf.at[plane, pl.ds(0, block_rows)],
                                         o2_hbm.at[pl.ds(plane * plane_rows + n_tok * ROW_CHUNKS, block_rows)], isem)
            init.start()
            init.wait()
        for r in range(MOE_BM):
            gather_row(0, r, 0).start()

    @pl.when(i < n_used)
    def _():
        gather_wait(slot)

        @pl.when((i == 0) | (be_ref[i] != be_ref[jnp.maximum(i - 1, 0)]))
        def _():
            w1_bf[...] = w1_ref[...].astype(BF16)
            w3_bf[...] = w3_ref[...].astype(BF16)
            w2_bf[...] = w2_ref[...].astype(BF16)

        @pl.when(i >= 1)
        def _():
            scatter_wait(slot)

        prev = jnp.where(i == 0, n_blocks - 1, i - 1)
        for r in range(MOE_BM):
            scatter_row(prev, r, 1 - slot).start()
        xn = jnp.concatenate([xbuf[slot, pl.ds(ch, MOE_BM, stride=BUF_PITCH), :] for ch in range(ROW_CHUNKS)],
                             axis=1).astype(BF16)
        for r in range(MOE_BM):
            gather_row(i + 1, r, 1 - slot).start()
        h1 = jnp.dot(xn, w1_bf[...], preferred_element_type=F32)
        h3 = jnp.dot(xn, w3_bf[...], preferred_element_type=F32)
        act = (_silu(h1) * h3).astype(BF16)
        y = jnp.dot(act, w2_bf[...], preferred_element_type=F32)
        for ch in range(ROW_CHUNKS):
            ybuf[slot, pl.ds(ch, MOE_BM, stride=BUF_PITCH), :] = y[:, ch * LANES:(ch + 1) * LANES]

    @pl.when(i == n_used)
    def _():
        for r in range(MOE_BM):
            scatter_row(i - 1, r, 1 - slot).start()
        gather_wait(slot)
        scatter_wait(slot)
        scatter_wait(1 - slot)


def moe_experts(x3, dest_flat, w1, w3, w2, layer, block_e, n_used):
    plane_rows = (x3.shape[0] // ROW_CHUNKS + DUMP_ROWS) * ROW_CHUNKS
    n_tok = x3.shape[0] // ROW_CHUNKS
    d = D_MODEL
    n_blocks = block_e.shape[0]
    n_rows = n_blocks * MOE_BM
    f = w1.shape[-1]

    def w_map(i, be, nu, dest):
        return (layer, be[i], 0, 0)

    grid_spec = pltpu.PrefetchScalarGridSpec(
        num_scalar_prefetch=3,
        grid=(n_blocks,),
        in_specs=[pl.BlockSpec(memory_space=pl.ANY),
                  pl.BlockSpec(memory_space=pl.ANY),
                  pl.BlockSpec(memory_space=pl.ANY),
                  pl.BlockSpec((None, None, d, f), w_map),
                  pl.BlockSpec((None, None, d, f), w_map),
                  pl.BlockSpec((None, None, f, d), w_map)],
        out_specs=pl.BlockSpec(memory_space=pl.ANY),
        scratch_shapes=[pltpu.SMEM((n_rows,), I32),
                        pltpu.SMEM((n_rows,), I32),
                        pltpu.VMEM((2, MOE_BM * BUF_PITCH, LANES), F32),
                        pltpu.VMEM((2, MOE_BM * BUF_PITCH, LANES), F32),
                        pltpu.SemaphoreType.DMA((2,)),
                        pltpu.SemaphoreType.DMA((2,)),
                        pltpu.SemaphoreType.DMA(()),
                        pltpu.VMEM((d, f), BF16), pltpu.VMEM((d, f), BF16), pltpu.VMEM((f, d), BF16)],
    )
    pad_row = jnp.arange(n_rows, dtype=I32) % MOE_BM
    o2 = pl.pallas_call(
        functools.partial(_expert_body, n_tok=n_tok, n_blocks=n_blocks),
        grid_spec=grid_spec,
        out_shape=jax.ShapeDtypeStruct((2 * plane_rows, LANES), F32),
        compiler_params=_params("arbitrary"),
        name="moe_experts",
    )(block_e, n_used, dest_flat, jnp.zeros((n_rows,), I32), (n_tok + pad_row) * ROW_CHUNKS, x3, w1, w3, w2)
    return o2.reshape(2, plane_rows, LANES)


def _combine_body(*refs, final_norm):
    if final_norm:
        h_ref, wgt_ref, o2_ref, gf_ref, o_ref = refs
    else:
        h_ref, wgt_ref, o2_ref, o_ref = refs
    wgt = wgt_ref[...]
    y0, y1 = (jnp.concatenate([o2_ref[s, pl.ds(ch, COMBINE_T, stride=ROW_CHUNKS), :] for ch in range(ROW_CHUNKS)], axis=1)
              for s in range(2))
    out = h_ref[...] + wgt[:, 0:1] * y0 + wgt[:, 1:2] * y1
    if final_norm:
        out = _rms(out, gf_ref[...])
    o_ref[...] = out


def moe_combine(h2d, wgt, o2, g_final=None):
    n_tok, d = h2d.shape
    final_norm = g_final is not None
    in_specs = [pl.BlockSpec((COMBINE_T, d), lambda i: (i, 0)),
                pl.BlockSpec((COMBINE_T, LANES), lambda i: (i, 0)),
                pl.BlockSpec((2, COMBINE_T * ROW_CHUNKS, LANES), lambda i: (0, i, 0))]
    args = [h2d, wgt, o2]
    if final_norm:
        in_specs.append(pl.BlockSpec((1, d), lambda i: (0, 0)))
        args.append(g_final.reshape(1, d))
    return pl.pallas_call(
        functools.partial(_combine_body, final_norm=final_norm),
        grid=(n_tok // COMBINE_T,),
        in_specs=in_specs,
        out_specs=pl.BlockSpec((COMBINE_T, d), lambda i: (i, 0)),
        out_shape=jax.ShapeDtypeStruct((n_tok, d), F32),
        compiler_params=_params("parallel"),
        name="moe_combine_final" if final_norm else "moe_combine",
    )(*args)


def hierarchical_moe(h2d, layer, norm_g, w_group, b_group, w_router, b_router, w1, w3, w2, g_final=None):
    n_tok, d = h2d.shape
    pad_l = LANES - N_GROUPS - N_EXPERTS
    w_cat = jnp.concatenate([w_group, w_router, jnp.zeros((d, pad_l), F32)], axis=1)
    b_cat = jnp.concatenate([b_group, b_router, jnp.zeros((pad_l,), F32)]).reshape(1, LANES)
    idx, wgt, cnt, x3 = moe_router(h2d, norm_g, w_cat, b_cat)

    counts = cnt[0, EXPERT_LANE0:EXPERT_LANE0 + N_EXPERTS].astype(I32)
    padded = (counts + MOE_BM - 1) // MOE_BM * MOE_BM
    pends = jnp.cumsum(padded)
    pstarts = pends - padded
    n_blocks = (n_tok * 2) // MOE_BM + N_EXPERTS
    blk_row0 = jnp.arange(n_blocks, dtype=I32) * MOE_BM
    block_e = jnp.minimum(jnp.sum(pends[None, :] <= blk_row0[:, None], axis=1), N_EXPERTS - 1).astype(I32)
    n_used = (pends[-1:] // MOE_BM).astype(I32)
    e_iota = jnp.arange(N_EXPERTS, dtype=I32)
    row0 = jnp.sum(jnp.where(idx[:, 0:2, None] == e_iota, pstarts, 0), axis=-1)
    dest = (row0 + idx[:, 2:4]).astype(I32).reshape(-1)

    o2 = moe_experts(x3, dest, w1, w3, w2, layer, block_e, n_used)
    return moe_combine(h2d, wgt, o2, g_final)


def kernel(x, norm_mix, norm_ffn, norm_final, w_in_even, hg_lb_logits, hg_norm, gla_gk2, gla_gk_bias,
           gla_norm, w_out_even, w_qkv_odd, w_o_odd, router_group_w, router_group_b, router_expert_w,
           router_expert_b, expert_w1, expert_w3, expert_w2):
    b, s, d = x.shape
    n_tok = b * s
    depth = norm_mix.shape[0]
    lb_table = jnp.cumsum(jax.nn.softmax(hg_lb_logits.astype(F32), axis=0), axis=0)
    h = x.reshape(n_tok, d)
    for l in range(depth):
        if l % 2 == 0:
            e = l // 2
            w_in = w_in_even[e]
            w_low = jnp.pad(w_in[:, MAIN_IN:], ((0, 0), (0, LANES - GLA_RANK))).astype(BF16)
            proj, glow = norm_matmul(h, norm_mix[l], w_in.astype(BF16), w_low, tm=PROJ_TM, tn=IN_TN, n=MAIN_IN)
            proj = proj.reshape(b, s, MAIN_IN)
            gk2_pad = jnp.pad(gla_gk2[e], ((0, LANES - GLA_RANK), (0, 0)))
            o_hg = hgrn2_mix(proj, lb_table[l], hg_norm[e])
            o_gla = gla_mix(proj, glow.reshape(b, s, LANES), gk2_pad, gla_gk_bias[e], gla_norm[e])
            h = matmul_residual([o_hg.reshape(n_tok, -1), o_gla.reshape(n_tok, -1)], w_out_even[e].astype(BF16), h,
                                tm=OUT_TM)
        else:
            o = l // 2
            qkv = norm_matmul(h, norm_mix[l], w_qkv_odd[o].astype(BF16), tm=PROJ_TM, tn=QKV_TN)
            attn = moba_attention(qkv.reshape(b, s, 3 * d))
            h = matmul_residual([attn.reshape(n_tok, d)], w_o_odd[o].astype(BF16), h, tm=OUT_TM)
        h = hierarchical_moe(h, l, norm_ffn[l], router_group_w[l], router_group_b[l], router_expert_w[l],
                             router_expert_b[l], expert_w1, expert_w3, expert_w2,
                             g_final=norm_final if l == depth - 1 else None)
    return h.reshape(b, s, d)
```

```python
import functools

import jax
import jax.numpy as jnp
from jax import lax
from jax.experimental import pallas as pl
from jax.experimental.pallas import tpu as pltpu

F32 = jnp.float32
BF16 = jnp.bfloat16
I32 = jnp.int32
HIGHEST = lax.Precision.HIGHEST
NEG_INF = float("-inf")
LOG2_E = 1.4426950408889634

EPS = 1e-6
D_MODEL = 2048

HG_HEADS, HG_DK, HG_DV = 8, 128, 128
GLA_HEADS, GLA_DK, GLA_DV = 4, 128, 256
GLA_RANK = 16
GLA_GATE_NORM = 16.0
HG_QK = HG_HEADS * HG_DK
MAIN_IN = 4 * HG_QK + 2 * GLA_HEADS * GLA_DK + 2 * GLA_HEADS * GLA_DV
CHUNK = 64
SUB = 8
REC_T = 256
HG_HP, GLA_HP = HG_HEADS, GLA_HEADS

MOBA_HEADS, MOBA_DH = 16, 128
MOBA_BLOCK = 256
MOBA_TOPK = 3
MOBA_HP = 4
ONES_ROWS = 16
MASK_BIAS = -1e30

N_GROUPS, EXPERTS_PER_GROUP = 4, 8
N_EXPERTS = N_GROUPS * EXPERTS_PER_GROUP
D_EXPERT = D_MODEL // 4
EXPERT_LANE0 = N_GROUPS
ROUTE_TM = 256
MOE_BM = 256
COMBINE_T = 512
ROW_CHUNKS = D_MODEL // 128
BUF_PITCH = ROW_CHUNKS + 8

PROJ_TM = 1024
IN_TN, QKV_TN = 1792, 2048
OUT_TM = 512

LANES = 128
VMEM_LIMIT = 56 * 1024 * 1024

NT_DIMS = (((1,), (1,)), ((), ()))
TN_DIMS = (((0,), (0,)), ((), ()))


def _params(*sem):
    return pltpu.CompilerParams(dimension_semantics=sem, vmem_limit_bytes=VMEM_LIMIT)


def _rms(x, g):
    return x * lax.rsqrt(jnp.mean(x * x, axis=-1, keepdims=True) + EPS) * g


def _silu(x):
    return x * jax.nn.sigmoid(x)


def _norm_matmul_body(*refs, has_side):
    if has_side:
        x_ref, g_ref, w_ref, ws_ref, o_ref, os_ref, xn_ref = refs
    else:
        x_ref, g_ref, w_ref, o_ref, xn_ref = refs

    @pl.when(pl.program_id(1) == 0)
    def _():
        xn_ref[...] = _rms(x_ref[...], g_ref[...]).astype(BF16)
        if has_side:
            os_ref[...] = jnp.dot(xn_ref[...], ws_ref[...], preferred_element_type=F32)

    o_ref[...] = jnp.dot(xn_ref[...], w_ref[...], preferred_element_type=F32).astype(o_ref.dtype)


def norm_matmul(x, g, w, w_side=None, *, tm, tn, n=None):
    m, k = x.shape
    n = w.shape[1] if n is None else n
    has_side = w_side is not None
    in_specs = [pl.BlockSpec((tm, k), lambda i, j: (i, 0)),
                pl.BlockSpec((1, k), lambda i, j: (0, 0)),
                pl.BlockSpec((k, tn), lambda i, j: (0, j))]
    out_specs = [pl.BlockSpec((tm, tn), lambda i, j: (i, j))]
    out_shape = [jax.ShapeDtypeStruct((m, n), BF16)]
    args = [x, g.reshape(1, k), w]
    if has_side:
        ns = w_side.shape[1]
        in_specs.append(pl.BlockSpec((k, ns), lambda i, j: (0, 0)))
        out_specs.append(pl.BlockSpec((tm, ns), lambda i, j: (i, 0)))
        out_shape.append(jax.ShapeDtypeStruct((m, ns), F32))
        args.append(w_side)
    outs = pl.pallas_call(
        functools.partial(_norm_matmul_body, has_side=has_side),
        grid=(m // tm, n // tn),
        in_specs=in_specs, out_specs=out_specs, out_shape=out_shape,
        scratch_shapes=[pltpu.VMEM((tm, k), BF16)],
        compiler_params=_params("parallel", "arbitrary"),
        name="norm_matmul_side" if has_side else "norm_matmul",
    )(*args)
    return outs if has_side else outs[0]


def _matmul_res_body(*refs):
    *a_refs, w_ref, r_ref, o_ref = refs
    acc = r_ref[...]
    k0 = 0
    for a_ref in a_refs:
        kp = a_ref.shape[1]
        acc = acc + jnp.dot(a_ref[...], w_ref[k0:k0 + kp, :], preferred_element_type=F32)
        k0 += kp
    o_ref[...] = acc


def matmul_residual(a_pieces, w, res, *, tm):
    m = res.shape[0]
    k, n = w.shape
    return pl.pallas_call(
        _matmul_res_body,
        grid=(m // tm,),
        in_specs=[pl.BlockSpec((tm, a.shape[1]), lambda i: (i, 0)) for a in a_pieces]
        + [pl.BlockSpec((k, n), lambda i: (0, 0)),
           pl.BlockSpec((tm, n), lambda i: (i, 0))],
        out_specs=pl.BlockSpec((tm, n), lambda i: (i, 0)),
        out_shape=jax.ShapeDtypeStruct((m, n), F32),
        compiler_params=_params("parallel"),
        name="matmul_residual",
    )(*a_pieces, w, res)


def _dot_bf16_by_f32(a, b):
    parts = []
    rest = b
    for _ in range(3):
        piece = rest.astype(BF16)
        rest = rest - piece.astype(F32)
        parts.append(jnp.dot(a, piece, preferred_element_type=F32))
    return parts[0] + (parts[1] + parts[2])


def _recurrence_levels(q, k, v, la, st_ref):
    n_ch = REC_T // CHUNK
    n_sub = CHUNK // SUB
    dv = v.shape[1]
    r_i = lax.broadcasted_iota(I32, (CHUNK, CHUNK), 0)
    c_i = lax.broadcasted_iota(I32, (CHUNK, CHUNK), 1)
    tri = jnp.where(c_i <= r_i, 1.0, 0.0).astype(BF16)
    local = [_dot_bf16_by_f32(tri, la[c * CHUNK:(c + 1) * CHUNK]) for c in range(n_ch)]
    cums = [local[0]]
    for c in range(1, n_ch):
        cums.append(local[c] + cums[-1][CHUNK - 1:CHUNK, :])
    cum = jnp.concatenate(cums, axis=0)
    last = cum[REC_T - 1:REC_T, :]
    vb = v.astype(BF16)
    st = st_ref[...]
    yield

    o_inter = lax.dot_general((q * jnp.exp(cum)).astype(BF16), st.astype(BF16), NT_DIMS,
                              preferred_element_type=F32)
    kv = lax.dot_general(vb, (k * jnp.exp(last - cum)).astype(BF16), TN_DIMS, preferred_element_type=F32)
    rows = lax.broadcasted_iota(I32, (SUB, 1), 0)
    ones = jnp.ones((q.shape[1], LANES), BF16)
    s_cross, s_sub, diag = {}, {}, {}
    for c in range(n_ch):
        c0 = c * CHUNK
        if c > 0:
            ref_pt = cum[c0 - 1:c0, :]
            q_t = (q[c0:c0 + CHUNK] * jnp.exp(cum[c0:c0 + CHUNK] - ref_pt)).astype(BF16)
            k_t = (k[:c0] * jnp.exp(ref_pt - cum[:c0])).astype(BF16)
            s_cross[c] = lax.dot_general(q_t, k_t, NT_DIMS, preferred_element_type=F32)
        for b in range(n_sub):
            lo = c0 + b * SUB
            q_b, k_b, c_b = q[lo:lo + SUB], k[lo:lo + SUB], cum[lo:lo + SUB]
            if b > 0:
                ref_pt = cum[lo - 1:lo, :]
                q_t = (q_b * jnp.exp(c_b - ref_pt)).astype(BF16)
                k_t = (k[c0:lo] * jnp.exp(ref_pt - cum[c0:lo])).astype(BF16)
                s_sub[c, b] = lax.dot_general(q_t, k_t, NT_DIMS, preferred_element_type=F32)
            terms = []
            for j in range(SUB):
                e = jnp.exp(jnp.where(rows >= j, c_b - c_b[j:j + 1, :], NEG_INF))
                terms.append(q_b * k_b[j:j + 1, :] * e)
            diag[c, b] = jnp.dot(jnp.concatenate(terms, axis=0).astype(BF16), ones, preferred_element_type=F32)

    yield
    cross = {c: jnp.dot(s_cross[c].astype(BF16), vb[:c * CHUNK], preferred_element_type=F32) for c in s_cross}
    sub = {cb: jnp.dot(s_sub[cb].astype(BF16), v[cb[0] * CHUNK:cb[0] * CHUNK + cb[1] * SUB].astype(BF16),
                       preferred_element_type=F32) for cb in s_sub}
    st_ref[...] = st * jnp.exp(last) + kv
    yield

    outs = []
    for c in range(n_ch):
        for b in range(n_sub):
            lo = c * CHUNK + b * SUB
            acc = o_inter[lo:lo + SUB]
            if c > 0:
                acc = acc + cross[c][b * SUB:(b + 1) * SUB]
            if b > 0:
                acc = acc + sub[c, b]
            v_b = v[lo:lo + SUB]
            for j in range(SUB):
                col = diag[c, b][j * SUB:(j + 1) * SUB]
                if dv > LANES:
                    col = jnp.concatenate([col] * (dv // LANES), axis=1)
                acc = acc + col * v_b[j:j + 1, :]
            outs.append(acc)
    return jnp.concatenate(outs, axis=0)


def _run_heads(steps):
    results = [None] * len(steps)
    live = list(range(len(steps)))
    while live:
        for n in list(live):
            try:
                next(steps[n])
            except StopIteration as done:
                results[n] = done.value
                live.remove(n)
    return results


def _log_sigmoid(z):
    return jnp.minimum(z, 0.0) - jnp.log(1.0 + jnp.exp(-jnp.abs(z)))


def _hgrn2_body(hq_ref, hf_ref, hi_ref, hg_ref, lb_ref, nw_ref, o_ref, st_ref, *, hp):
    @pl.when(pl.program_id(2) == 0)
    def _():
        st_ref[...] = jnp.zeros_like(st_ref)

    steps = []
    for hh in range(hp):
        cols = slice(hh * HG_DK, (hh + 1) * HG_DK)
        lb = lb_ref[:, cols]
        forget = lb + (1.0 - lb) * jax.nn.sigmoid(hf_ref[0, :, cols].astype(F32))
        q = _silu(hq_ref[0, :, cols].astype(F32)) * (HG_DK ** -0.5)
        steps.append(_recurrence_levels(q, 1.0 - forget, hi_ref[0, :, cols].astype(F32), jnp.log(forget),
                                        st_ref.at[hh]))
    for hh, o in enumerate(_run_heads(steps)):
        cols = slice(hh * HG_DV, (hh + 1) * HG_DV)
        o_ref[0, :, cols] = (_rms(o, nw_ref[...]) * _silu(hg_ref[0, :, cols].astype(F32))).astype(o_ref.dtype)


def _gla_body(gq_ref, gk_ref, gv_ref, gg_ref, glow_ref, gk2_ref, gb_ref, nw_ref, o_ref, st_ref, *, hp):
    @pl.when(pl.program_id(2) == 0)
    def _():
        st_ref[...] = jnp.zeros_like(st_ref)

    z = jnp.dot(glow_ref[0], gk2_ref[...], precision=HIGHEST, preferred_element_type=F32) + gb_ref[...]
    la = _log_sigmoid(z) / GLA_GATE_NORM
    steps = []
    for hh in range(hp):
        kc = slice(hh * GLA_DK, (hh + 1) * GLA_DK)
        vc = slice(hh * GLA_DV, (hh + 1) * GLA_DV)
        q = gq_ref[0, :, kc].astype(F32) * (GLA_DK ** -0.5)
        steps.append(_recurrence_levels(q, gk_ref[0, :, kc].astype(F32), gv_ref[0, :, vc].astype(F32), la[:, kc],
                                        st_ref.at[hh]))
    for hh, o in enumerate(_run_heads(steps)):
        vc = slice(hh * GLA_DV, (hh + 1) * GLA_DV)
        o_ref[0, :, vc] = (_rms(o, nw_ref[...]) * _silu(gg_ref[0, :, vc].astype(F32))).astype(o_ref.dtype)


def hgrn2_mix(proj, lb, hg_norm):
    b, s, _ = proj.shape

    hp = HG_HP
    hsteps = HG_HEADS // hp

    def col(base):
        return pl.BlockSpec((1, REC_T, hp * HG_DK), lambda bi, h, t: (bi, t, base + h))

    return pl.pallas_call(
        functools.partial(_hgrn2_body, hp=hp),
        grid=(b, hsteps, s // REC_T),
        in_specs=[col(0), col(hsteps), col(2 * hsteps), col(3 * hsteps),
                  pl.BlockSpec((1, hp * HG_DK), lambda bi, h, t: (0, h)),
                  pl.BlockSpec((1, HG_DV), lambda bi, h, t: (0, 0))],
        out_specs=pl.BlockSpec((1, REC_T, hp * HG_DV), lambda bi, h, t: (bi, t, h)),
        out_shape=jax.ShapeDtypeStruct((b, s, HG_HEADS * HG_DV), BF16),
        scratch_shapes=[pltpu.VMEM((hp, HG_DV, HG_DK), F32)],
        compiler_params=_params("parallel", "parallel", "arbitrary"),
        name="hgrn2_mix",
    )(proj, proj, proj, proj, lb.reshape(1, HG_QK), hg_norm.reshape(1, HG_DV))


def gla_mix(proj, glow, gk2_pad, gk_bias, gla_norm):
    b, s, _ = proj.shape
    hp = GLA_HP
    hsteps = GLA_HEADS // hp
    kw, vw = hp * GLA_DK, hp * GLA_DV
    q0 = 4 * HG_QK // kw
    k0 = q0 + hsteps
    v0 = (4 * HG_QK + 2 * GLA_HEADS * GLA_DK) // vw
    g0 = v0 + hsteps
    return pl.pallas_call(
        functools.partial(_gla_body, hp=hp),
        grid=(b, hsteps, s // REC_T),
        in_specs=[pl.BlockSpec((1, REC_T, kw), lambda bi, h, t: (bi, t, q0 + h)),
                  pl.BlockSpec((1, REC_T, kw), lambda bi, h, t: (bi, t, k0 + h)),
                  pl.BlockSpec((1, REC_T, vw), lambda bi, h, t: (bi, t, v0 + h)),
                  pl.BlockSpec((1, REC_T, vw), lambda bi, h, t: (bi, t, g0 + h)),
                  pl.BlockSpec((1, REC_T, LANES), lambda bi, h, t: (bi, t, 0)),
                  pl.BlockSpec((LANES, kw), lambda bi, h, t: (0, h)),
                  pl.BlockSpec((1, kw), lambda bi, h, t: (0, h)),
                  pl.BlockSpec((1, GLA_DV), lambda bi, h, t: (0, 0))],
        out_specs=pl.BlockSpec((1, REC_T, vw), lambda bi, h, t: (bi, t, h)),
        out_shape=jax.ShapeDtypeStruct((b, s, GLA_HEADS * GLA_DV), BF16),
        scratch_shapes=[pltpu.VMEM((hp, GLA_DV, GLA_DK), F32)],
        compiler_params=_params("parallel", "parallel", "arbitrary"),
        name="gla_mix",
    )(proj, proj, proj, proj, glow, gk2_pad, gk_bias.reshape(1, -1), gla_norm.reshape(1, GLA_DV))


def _dot_f32_by_bf16(a, b):
    parts = []
    rest = a
    for _ in range(3):
        piece = rest.astype(BF16)
        rest = rest - piece.astype(F32)
        parts.append(lax.dot_general(piece, b, NT_DIMS, preferred_element_type=F32))
    return parts[0] + (parts[1] + parts[2])


def _moba_body(q_ref, k_ref, v_ref, o_ref, kmean_ref, vt_ref, bias_ref, sa_ref, sb_ref, *, n_blocks):
    qi = pl.program_id(2)
    blk, dh = MOBA_BLOCK, MOBA_DH
    heads = [slice(hh * dh, (hh + 1) * dh) for hh in range(MOBA_HP)]

    @pl.when(qi == 0)
    def _():
        for n in range(n_blocks):
            rs = slice(n * blk, (n + 1) * blk)
            for hh, cols in enumerate(heads):
                kmean_ref[hh, n:n + 1, :] = jnp.mean(k_ref[0, rs, cols].astype(F32), axis=0, keepdims=True)
                vt_ref[hh, 0:dh, rs] = v_ref[0, rs, cols].astype(F32).T.astype(BF16)
                vt_ref[hh, dh:dh + ONES_ROWS, rs] = jnp.ones((ONES_ROWS, blk), BF16)

    c = (MOBA_DH ** -0.5) * LOG2_E
    own = pl.ds(pl.multiple_of(qi * blk, blk), blk)
    brow = lax.broadcasted_iota(I32, (n_blocks, blk), 0)
    past = brow < qi
    k_i = lax.broadcasted_iota(I32, (blk, blk), 0)
    q_i = lax.broadcasted_iota(I32, (blk, blk), 1)

    qs = [q_ref[0, :, cols] for cols in heads]
    gates = [_dot_f32_by_bf16(kmean_ref[hh], qs[hh]) for hh in range(MOBA_HP)]
    own_s = [lax.dot_general(k_ref[0, own, cols], qs[hh], NT_DIMS, preferred_element_type=F32)
             for hh, cols in enumerate(heads)]
    r01_first = pl.ds(0, 2 * blk)
    for hh, cols in enumerate(heads):
        sa_ref[hh] = lax.dot_general(k_ref[0, r01_first, cols], qs[hh], NT_DIMS, preferred_element_type=F32)
    carry0 = []
    for hh in range(MOBA_HP):
        gate = jnp.where(past, gates[hh], NEG_INF)
        rank = jnp.zeros((n_blocks, blk), I32)
        for m in range(n_blocks):
            gm = gate[m:m + 1, :]
            rank = rank + jnp.where(gm > gate, 1, jnp.where(gm == gate, jnp.where(brow > m, 1, 0), 0))
        keep = jnp.where(past, jnp.where(rank < MOBA_TOPK, 1, 0), 0)
        bias_ref[hh] = jnp.where(keep > 0, 0.0, MASK_BIAS)
    for hh in range(MOBA_HP):
        s = jnp.where(k_i <= q_i, own_s[hh], NEG_INF)
        m0 = jnp.max(s, axis=0, keepdims=True)
        p = jnp.exp2((s - m0) * c)
        acc0 = jnp.dot(vt_ref[hh, :, own], p.astype(BF16), preferred_element_type=F32)
        carry0 += [m0, acc0]

    n_pairs = (qi + 1) // 2

    def score_pair(pair, dst_ref):
        p = jnp.minimum(pair, n_blocks // 2 - 1)
        r01 = pl.ds(pl.multiple_of(p * 2 * blk, 2 * blk), 2 * blk)
        for hh, cols in enumerate(heads):
            dst_ref[hh] = lax.dot_general(k_ref[0, r01, cols], qs[hh], NT_DIMS, preferred_element_type=F32)

    def absorb_pair(pair, src_ref, carry):
        n0 = 2 * pair
        r01 = pl.ds(pl.multiple_of(n0 * blk, 2 * blk), 2 * blk)
        out = []
        for hh, cols in enumerate(heads):
            m_run, acc = carry[2 * hh:2 * hh + 2]
            b0 = bias_ref[hh, pl.ds(n0, 1), :]
            b1 = bias_ref[hh, pl.ds(n0 + 1, 1), :]
            s0 = src_ref[hh, 0:blk, :]
            s1 = src_ref[hh, blk:2 * blk, :]
            m_new = jnp.maximum(m_run, jnp.maximum(jnp.max(s0, axis=0, keepdims=True) + b0,
                                                   jnp.max(s1, axis=0, keepdims=True) + b1))
            alpha = jnp.exp2((m_run - m_new) * c)
            p01 = jnp.concatenate([jnp.exp2((s0 - (m_new - b0)) * c).astype(BF16),
                                   jnp.exp2((s1 - (m_new - b1)) * c).astype(BF16)], axis=0)
            acc = alpha * acc + jnp.dot(vt_ref[hh, :, r01], p01, preferred_element_type=F32)
            out += [m_new, acc]
        return tuple(out)

    def two_pairs(t, carry):
        score_pair(2 * t + 1, sb_ref)
        carry = absorb_pair(2 * t, sa_ref, carry)

        def second(carry):
            score_pair(2 * t + 2, sa_ref)
            return absorb_pair(2 * t + 1, sb_ref, carry)

        return lax.cond(2 * t + 1 < n_pairs, second, lambda carry: carry, carry)

    fin = lax.fori_loop(0, (n_pairs + 1) // 2, two_pairs, tuple(carry0))
    for hh, cols in enumerate(heads):
        acc = fin[2 * hh + 1]
        o_ref[0, :, cols] = (acc[0:dh] / acc[dh:dh + 1]).T.astype(o_ref.dtype)


def moba_attention(qkv):
    b, s, _ = qkv.shape
    n_blocks = s // MOBA_BLOCK
    hw = MOBA_HP * MOBA_DH
    hsteps = MOBA_HEADS // MOBA_HP
    return pl.pallas_call(
        functools.partial(_moba_body, n_blocks=n_blocks),
        grid=(b, hsteps, n_blocks),
        in_specs=[pl.BlockSpec((1, MOBA_BLOCK, hw), lambda bi, h, t: (bi, t, h)),
                  pl.BlockSpec((1, s, hw), lambda bi, h, t: (bi, 0, hsteps + h)),
                  pl.BlockSpec((1, s, hw), lambda bi, h, t: (bi, 0, 2 * hsteps + h))],
        out_specs=pl.BlockSpec((1, MOBA_BLOCK, hw), lambda bi, h, t: (bi, t, h)),
        out_shape=jax.ShapeDtypeStruct((b, s, D_MODEL), BF16),
        scratch_shapes=[pltpu.VMEM((MOBA_HP, n_blocks, MOBA_DH), F32),
                        pltpu.VMEM((MOBA_HP, MOBA_DH + ONES_ROWS, s), BF16),
                        pltpu.VMEM((MOBA_HP, n_blocks, MOBA_BLOCK), F32),
                        pltpu.VMEM((MOBA_HP, 2 * MOBA_BLOCK, MOBA_BLOCK), F32),
                        pltpu.VMEM((MOBA_HP, 2 * MOBA_BLOCK, MOBA_BLOCK), F32)],
        compiler_params=_params("parallel", "parallel", "arbitrary"),
        name="moba_attention",
    )(qkv, qkv, qkv)


def _router_body(h_ref, g_ref, wh_ref, wl_ref, b_ref, idx_ref, wgt_ref, cnt_ref, x3_ref, run_ref):
    step = pl.program_id(0)

    @pl.when(step == 0)
    def _():
        run_ref[...] = jnp.zeros_like(run_ref)

    tm = h_ref.shape[0]
    xn = _rms(h_ref[...], g_ref[...])
    for ch in range(ROW_CHUNKS):
        x3_ref[pl.ds(ch, tm, stride=ROW_CHUNKS), :] = xn[:, ch * LANES:(ch + 1) * LANES]
    x_hi = xn.astype(BF16)
    x_lo = (xn - x_hi.astype(F32)).astype(BF16)
    logits = (jnp.dot(x_hi, wh_ref[...], preferred_element_type=F32)
              + (jnp.dot(x_hi, wl_ref[...], preferred_element_type=F32)
                 + jnp.dot(x_lo, wh_ref[...], preferred_element_type=F32))) + b_ref[...]
    lane = lax.broadcasted_iota(I32, (tm, LANES), 1)

    def first_max(vals):
        top = jnp.max(vals, axis=-1, keepdims=True)
        where = jnp.min(jnp.where(vals == top, lane, LANES), axis=-1, keepdims=True)
        return top, where

    g_logits = jnp.where(lane < N_GROUPS, logits, NEG_INF)
    g_top, grp = first_max(g_logits)
    p_grp = 1.0 / jnp.sum(jnp.exp(g_logits - g_top), axis=-1, keepdims=True)
    lo = EXPERT_LANE0 + grp * EXPERTS_PER_GROUP
    e_logits = jnp.where((lane >= lo) & (lane < lo + EXPERTS_PER_GROUP), logits, NEG_INF)
    v0, j0 = first_max(e_logits)
    v1, j1 = first_max(jnp.where(lane == j0, NEG_INF, e_logits))
    t = jnp.exp(v1 - v0)
    w0 = p_grp / (1.0 + t)
    w1 = p_grp * t / (1.0 + t)

    hit0 = lane == j0
    hit1 = lane == j1
    member = jnp.where(hit0 | hit1, 1.0, 0.0)
    r_i = lax.broadcasted_iota(I32, (tm, tm), 0)
    c_i = lax.broadcasted_iota(I32, (tm, tm), 1)
    before = jnp.where(c_i < r_i, 1.0, 0.0).astype(BF16)
    prior = jnp.dot(before, member.astype(BF16), preferred_element_type=F32) + run_ref[...]
    rank0 = jnp.sum(jnp.where(hit0, prior, 0.0), axis=-1, keepdims=True).astype(I32)
    rank1 = jnp.sum(jnp.where(hit1, prior, 0.0), axis=-1, keepdims=True).astype(I32)
    run_ref[...] = run_ref[...] + jnp.sum(member, axis=0, keepdims=True)
    cnt_ref[...] = run_ref[...]

    idx_ref[...] = jnp.where(lane == 0, j0 - EXPERT_LANE0,
                             jnp.where(lane == 1, j1 - EXPERT_LANE0,
                                       jnp.where(lane == 2, rank0, jnp.where(lane == 3, rank1, 0))))
    wgt_ref[...] = jnp.where(lane == 0, w0, jnp.where(lane == 1, w1, 0.0))


def moe_router(h2d, g, w_cat, b_cat):
    n_tok, d = h2d.shape
    w_hi = w_cat.astype(BF16)
    return pl.pallas_call(
        _router_body,
        grid=(n_tok // ROUTE_TM,),
        in_specs=[pl.BlockSpec((ROUTE_TM, d), lambda i: (i, 0)),
                  pl.BlockSpec((1, d), lambda i: (0, 0)),
                  pl.BlockSpec((d, LANES), lambda i: (0, 0)),
                  pl.BlockSpec((d, LANES), lambda i: (0, 0)),
                  pl.BlockSpec((1, LANES), lambda i: (0, 0))],
        out_specs=[pl.BlockSpec((ROUTE_TM, LANES), lambda i: (i, 0)),
                   pl.BlockSpec((ROUTE_TM, LANES), lambda i: (i, 0)),
                   pl.BlockSpec((1, LANES), lambda i: (0, 0)),
                   pl.BlockSpec((ROUTE_TM * ROW_CHUNKS, LANES), lambda i: (i, 0))],
        out_shape=[jax.ShapeDtypeStruct((n_tok, LANES), I32),
                   jax.ShapeDtypeStruct((n_tok, LANES), F32),
                   jax.ShapeDtypeStruct((1, LANES), F32),
                   jax.ShapeDtypeStruct((n_tok * ROW_CHUNKS, LANES), F32)],
        scratch_shapes=[pltpu.VMEM((1, LANES), F32)],
        compiler_params=_params("arbitrary"),
        name="moe_router",
    )(h2d, g.reshape(1, d), w_hi, (w_cat - w_hi.astype(F32)).astype(BF16), b_cat)


INV_UNROLL = 16
DUMP_ROWS = MOE_BM


def _expert_body(be_ref, nu_ref, dest_ref, gfill_hbm, sfill_hbm, x3_hbm, w1_ref, w3_ref, w2_ref, o2_hbm,
                 gsrc_ref, sdst_ref, xbuf, ybuf, gsem, ssem, isem, w1_bf, w3_bf, w2_bf, *, n_tok, n_blocks):
    i = pl.program_id(0)
    n_used = nu_ref[0]
    slot = i % 2
    plane_rows = (n_tok + DUMP_ROWS) * ROW_CHUNKS

    def gather_row(block, r, to_slot):
        src = pl.multiple_of(gsrc_ref[block * MOE_BM + r], ROW_CHUNKS)
        return pltpu.make_async_copy(x3_hbm.at[pl.ds(src, ROW_CHUNKS)],
                                     xbuf.at[to_slot, pl.ds(r * BUF_PITCH, ROW_CHUNKS)], gsem.at[to_slot])

    def scatter_row(block, r, from_slot):
        dst = pl.multiple_of(sdst_ref[block * MOE_BM + r], ROW_CHUNKS)
        return pltpu.make_async_copy(ybuf.at[from_slot, pl.ds(r * BUF_PITCH, ROW_CHUNKS)],
                                     o2_hbm.at[pl.ds(dst, ROW_CHUNKS)], ssem.at[from_slot])

    block_rows = MOE_BM * ROW_CHUNKS

    def gather_wait(of_slot):
        pltpu.make_async_copy(x3_hbm.at[pl.ds(0, block_rows)], xbuf.at[of_slot, pl.ds(0, block_rows)],
                              gsem.at[of_slot]).wait()

    def scatter_wait(of_slot):
        pltpu.make_async_copy(ybuf.at[of_slot, pl.ds(0, block_rows)], o2_hbm.at[pl.ds(0, block_rows)],
                              ssem.at[of_slot]).wait()

    @pl.when(i == 0)
    def _():
        for fill_hbm, table in ((gfill_hbm, gsrc_ref), (sfill_hbm, sdst_ref)):
            fill = pltpu.make_async_copy(fill_hbm, table, isem)
            fill.start()
            fill.wait()

        def body(c, carry):
            base = c * INV_UNROLL
            rows = [dest_ref[base + u] for u in range(INV_UNROLL)]
            tok_row0 = c * (INV_UNROLL // 2 * ROW_CHUNKS)
            for u in range(INV_UNROLL):
                gsrc_ref[rows[u]] = tok_row0 + (u // 2) * ROW_CHUNKS
                sdst_ref[rows[u]] = tok_row0 + ((u % 2) * plane_rows + (u // 2) * ROW_CHUNKS)
            return carry
        lax.fori_loop(0, dest_ref.shape[0] // INV_UNROLL, body, 0)
        ybuf[...] = jnp.zeros_like(ybuf)
        for plane in range(2):
            init = pltpu.make_async_copy(ybuf.at[plane, pl.ds(0, block_rows)],
                                         o2_hbm.at[pl.ds(plane * plane_rows + n_tok * ROW_CHUNKS, block_rows)], isem)
            init.start()
            init.wait()
        for r in range(MOE_BM):
            gather_row(0, r, 0).start()

    @pl.when(i < n_used)
    def _():
        gather_wait(slot)

        @pl.when((i == 0) | (be_ref[i] != be_ref[jnp.maximum(i - 1, 0)]))
        def _():
            w1_bf[...] = w1_ref[...].astype(BF16)
            w3_bf[...] = w3_ref[...].astype(BF16)
            w2_bf[...] = w2_ref[...].astype(BF16)

        @pl.when(i >= 1)
        def _():
            scatter_wait(slot)

        prev = jnp.where(i == 0, n_blocks - 1, i - 1)
        for r in range(MOE_BM):
            scatter_row(prev, r, 1 - slot).start()
        xn = jnp.concatenate([xbuf[slot, pl.ds(ch, MOE_BM, stride=BUF_PITCH), :] for ch in range(ROW_CHUNKS)],
                             axis=1).astype(BF16)
        for r in range(MOE_BM):
            gather_row(i + 1, r, 1 - slot).start()
        h1 = jnp.dot(xn, w1_bf[...], preferred_element_type=F32)
        h3 = jnp.dot(xn, w3_bf[...], preferred_element_type=F32)
        act = (_silu(h1) * h3).astype(BF16)
        y = jnp.dot(act, w2_bf[...], preferred_element_type=F32)
        for ch in range(ROW_CHUNKS):
            ybuf[slot, pl.ds(ch, MOE_BM, stride=BUF_PITCH), :] = y[:, ch * LANES:(ch + 1) * LANES]

    @pl.when(i == n_used)
    def _():
        for r in range(MOE_BM):
            scatter_row(i - 1, r, 1 - slot).start()
        gather_wait(slot)
        scatter_wait(slot)
        scatter_wait(1 - slot)


def moe_experts(x3, dest_flat, w1, w3, w2, layer, block_e, n_used):
    plane_rows = (x3.shape[0] // ROW_CHUNKS + DUMP_ROWS) * ROW_CHUNKS
    n_tok = x3.shape[0] // ROW_CHUNKS
    d = D_MODEL
    n_blocks = block_e.shape[0]
    n_rows = n_blocks * MOE_BM
    f = w1.shape[-1]

    def w_map(i, be, nu, dest):
        return (layer, be[i], 0, 0)

    grid_spec = pltpu.PrefetchScalarGridSpec(
        num_scalar_prefetch=3,
        grid=(n_blocks,),
        in_specs=[pl.BlockSpec(memory_space=pl.ANY),
                  pl.BlockSpec(memory_space=pl.ANY),
                  pl.BlockSpec(memory_space=pl.ANY),
                  pl.BlockSpec((None, None, d, f), w_map),
                  pl.BlockSpec((None, None, d, f), w_map),
                  pl.BlockSpec((None, None, f, d), w_map)],
        out_specs=pl.BlockSpec(memory_space=pl.ANY),
        scratch_shapes=[pltpu.SMEM((n_rows,), I32),
                        pltpu.SMEM((n_rows,), I32),
                        pltpu.VMEM((2, MOE_BM * BUF_PITCH, LANES), F32),
                        pltpu.VMEM((2, MOE_BM * BUF_PITCH, LANES), F32),
                        pltpu.SemaphoreType.DMA((2,)),
                        pltpu.SemaphoreType.DMA((2,)),
                        pltpu.SemaphoreType.DMA(()),
                        pltpu.VMEM((d, f), BF16), pltpu.VMEM((d, f), BF16), pltpu.VMEM((f, d), BF16)],
    )
    pad_row = jnp.arange(n_rows, dtype=I32) % MOE_BM
    o2 = pl.pallas_call(
        functools.partial(_expert_body, n_tok=n_tok, n_blocks=n_blocks),
        grid_spec=grid_spec,
        out_shape=jax.ShapeDtypeStruct((2 * plane_rows, LANES), F32),
        compiler_params=_params("arbitrary"),
        name="moe_experts",
    )(block_e, n_used, dest_flat, jnp.zeros((n_rows,), I32), (n_tok + pad_row) * ROW_CHUNKS, x3, w1, w3, w2)
    return o2.reshape(2, plane_rows, LANES)


def _combine_body(*refs, final_norm):
    if final_norm:
        h_ref, wgt_ref, o2_ref, gf_ref, o_ref = refs
    else:
        h_ref, wgt_ref, o2_ref, o_ref = refs
    wgt = wgt_ref[...]
    y0, y1 = (jnp.concatenate([o2_ref[s, pl.ds(ch, COMBINE_T, stride=ROW_CHUNKS), :] for ch in range(ROW_CHUNKS)], axis=1)
              for s in range(2))
    out = h_ref[...] + wgt[:, 0:1] * y0 + wgt[:, 1:2] * y1
    if final_norm:
        out = _rms(out, gf_ref[...])
    o_ref[...] = out


def moe_combine(h2d, wgt, o2, g_final=None):
    n_tok, d = h2d.shape
    final_norm = g_final is not None
    in_specs = [pl.BlockSpec((COMBINE_T, d), lambda i: (i, 0)),
                pl.BlockSpec((COMBINE_T, LANES), lambda i: (i, 0)),
                pl.BlockSpec((2, COMBINE_T * ROW_CHUNKS, LANES), lambda i: (0, i, 0))]
    args = [h2d, wgt, o2]
    if final_norm:
        in_specs.append(pl.BlockSpec((1, d), lambda i: (0, 0)))
        args.append(g_final.reshape(1, d))
    return pl.pallas_call(
        functools.partial(_combine_body, final_norm=final_norm),
        grid=(n_tok // COMBINE_T,),
        in_specs=in_specs,
        out_specs=pl.BlockSpec((COMBINE_T, d), lambda i: (i, 0)),
        out_shape=jax.ShapeDtypeStruct((n_tok, d), F32),
        compiler_params=_params("parallel"),
        name="moe_combine_final" if final_norm else "moe_combine",
    )(*args)


def hierarchical_moe(h2d, layer, norm_g, w_group, b_group, w_router, b_router, w1, w3, w2, g_final=None):
    n_tok, d = h2d.shape
    pad_l = LANES - N_GROUPS - N_EXPERTS
    w_cat = jnp.concatenate([w_group, w_router, jnp.zeros((d, pad_l), F32)], axis=1)
    b_cat = jnp.concatenate([b_group, b_router, jnp.zeros((pad_l,), F32)]).reshape(1, LANES)
    idx, wgt, cnt, x3 = moe_router(h2d, norm_g, w_cat, b_cat)

    counts = cnt[0, EXPERT_LANE0:EXPERT_LANE0 + N_EXPERTS].astype(I32)
    padded = (counts + MOE_BM - 1) // MOE_BM * MOE_BM
    pends = jnp.cumsum(padded)
    pstarts = pends - padded
    n_blocks = (n_tok * 2) // MOE_BM + N_EXPERTS
    blk_row0 = jnp.arange(n_blocks, dtype=I32) * MOE_BM
    block_e = jnp.minimum(jnp.sum(pends[None, :] <= blk_row0[:, None], axis=1), N_EXPERTS - 1).astype(I32)
    n_used = (pends[-1:] // MOE_BM).astype(I32)
    e_iota = jnp.arange(N_EXPERTS, dtype=I32)
    row0 = jnp.sum(jnp.where(idx[:, 0:2, None] == e_iota, pstarts, 0), axis=-1)
    dest = (row0 + idx[:, 2:4]).astype(I32).reshape(-1)

    o2 = moe_experts(x3, dest, w1, w3, w2, layer, block_e, n_used)
    return moe_combine(h2d, wgt, o2, g_final)


def kernel(x, norm_mix, norm_ffn, norm_final, w_in_even, hg_lb_logits, hg_norm, gla_gk2, gla_gk_bias,
           gla_norm, w_out_even, w_qkv_odd, w_o_odd, router_group_w, router_group_b, router_expert_w,
           router_expert_b, expert_w1, expert_w3, expert_w2):
    b, s, d = x.shape
    n_tok = b * s
    depth = norm_mix.shape[0]
    lb_table = jnp.cumsum(jax.nn.softmax(hg_lb_logits.astype(F32), axis=0), axis=0)
    h = x.reshape(n_tok, d)
    for l in range(depth):
        if l % 2 == 0:
            e = l // 2
            w_in = w_in_even[e]
            w_low = jnp.pad(w_in[:, MAIN_IN:], ((0, 0), (0, LANES - GLA_RANK))).astype(BF16)
            proj, glow = norm_matmul(h, norm_mix[l], w_in.astype(BF16), w_low, tm=PROJ_TM, tn=IN_TN, n=MAIN_IN)
            proj = proj.reshape(b, s, MAIN_IN)
            gk2_pad = jnp.pad(gla_gk2[e], ((0, LANES - GLA_RANK), (0, 0)))
            o_hg = hgrn2_mix(proj, lb_table[l], hg_norm[e])
            o_gla = gla_mix(proj, glow.reshape(b, s, LANES), gk2_pad, gla_gk_bias[e], gla_norm[e])
            h = matmul_residual([o_hg.reshape(n_tok, -1), o_gla.reshape(n_tok, -1)], w_out_even[e].astype(BF16), h,
                                tm=OUT_TM)
        else:
            o = l // 2
            qkv = norm_matmul(h, norm_mix[l], w_qkv_odd[o].astype(BF16), tm=PROJ_TM, tn=QKV_TN)
            attn = moba_attention(qkv.reshape(b, s, 3 * d))
            h = matmul_residual([attn.reshape(n_tok, d)], w_o_odd[o].astype(BF16), h, tm=OUT_TM)
        h = hierarchical_moe(h, l, norm_ffn[l], router_group_w[l], router_group_b[l], router_expert_w[l],
                             router_expert_b[l], expert_w1, expert_w3, expert_w2,
                             g_final=norm_final if l == depth - 1 else None)
    return h.reshape(b, s, d)
```

```python
import functools

import jax
import jax.numpy as jnp
from jax import lax
from jax.experimental import pallas as pl
from jax.experimental.pallas import tpu as pltpu

F32 = jnp.float32
BF16 = jnp.bfloat16
I32 = jnp.int32
NEG_INF = float("-inf")
LOG2_E = 1.4426950408889634

EPS = 1e-6
D_MODEL = 2048

HG_HEADS, HG_DK, HG_DV = 8, 128, 128
GLA_HEADS, GLA_DK, GLA_DV = 4, 128, 256
GLA_RANK = 16
GLA_GATE_NORM = 16.0
HG_QK = HG_HEADS * HG_DK
MAIN_IN = 4 * HG_QK + 2 * GLA_HEADS * GLA_DK + 2 * GLA_HEADS * GLA_DV
CHUNK = 64
SUB = 8
REC_T = 256
HG_HP, GLA_HP = HG_HEADS, GLA_HEADS

MOBA_HEADS, MOBA_DH = 16, 128
MOBA_BLOCK = 256
MOBA_TOPK = 3
MOBA_HP = 4
ONES_ROWS = 16
MASK_BIAS = -1e30

N_GROUPS, EXPERTS_PER_GROUP = 4, 8
N_EXPERTS = N_GROUPS * EXPERTS_PER_GROUP
EXPERT_LANE0 = N_GROUPS
ROUTE_TM = 256
MOE_BM = 256
COMBINE_T = 512
LANES = 128
ROW_CHUNKS = D_MODEL // LANES
BUF_PITCH = ROW_CHUNKS + 8

PROJ_TM = 1024
IN_TN, QKV_TN = 1792, 2048
OUT_TM = 512

VMEM_LIMIT = 56 * 1024 * 1024

NT_DIMS = (((1,), (1,)), ((), ()))
TN_DIMS = (((0,), (0,)), ((), ()))


def _params(*sem):
    return pltpu.CompilerParams(dimension_semantics=sem, vmem_limit_bytes=VMEM_LIMIT)


def _rms(x, g):
    return x * lax.rsqrt(jnp.mean(x * x, axis=-1, keepdims=True) + EPS) * g


def _silu(x):
    return x * jax.nn.sigmoid(x)


def _norm_matmul_body(*refs, has_side):
    if has_side:
        x_ref, g_ref, w_ref, ws_ref, o_ref, os_ref, xn_ref = refs
    else:
        x_ref, g_ref, w_ref, o_ref, xn_ref = refs

    @pl.when(pl.program_id(1) == 0)
    def _():
        xn_ref[...] = _rms(x_ref[...], g_ref[...]).astype(BF16)
        if has_side:
            os_ref[...] = jnp.dot(xn_ref[...], ws_ref[...], preferred_element_type=F32)

    o_ref[...] = jnp.dot(xn_ref[...], w_ref[...], preferred_element_type=F32).astype(o_ref.dtype)


def norm_matmul(x, g, w, w_side=None, *, tm, tn, n=None):
    m, k = x.shape
    n = w.shape[1] if n is None else n
    has_side = w_side is not None
    in_specs = [pl.BlockSpec((tm, k), lambda i, j: (i, 0)),
                pl.BlockSpec((1, k), lambda i, j: (0, 0)),
                pl.BlockSpec((k, tn), lambda i, j: (0, j))]
    out_specs = [pl.BlockSpec((tm, tn), lambda i, j: (i, j))]
    out_shape = [jax.ShapeDtypeStruct((m, n), BF16)]
    args = [x, g.reshape(1, k), w]
    if has_side:
        ns = w_side.shape[1]
        in_specs.append(pl.BlockSpec((k, ns), lambda i, j: (0, 0)))
        out_specs.append(pl.BlockSpec((tm, ns), lambda i, j: (i, 0)))
        out_shape.append(jax.ShapeDtypeStruct((m, ns), F32))
        args.append(w_side)
    outs = pl.pallas_call(
        functools.partial(_norm_matmul_body, has_side=has_side),
        grid=(m // tm, n // tn),
        in_specs=in_specs, out_specs=out_specs, out_shape=out_shape,
        scratch_shapes=[pltpu.VMEM((tm, k), BF16)],
        compiler_params=_params("parallel", "arbitrary"),
        name="norm_matmul_side" if has_side else "norm_matmul",
    )(*args)
    return outs if has_side else outs[0]


def _matmul_res_body(*refs):
    *a_refs, w_ref, r_ref, o_ref = refs
    acc = r_ref[...]
    k0 = 0
    for a_ref in a_refs:
        kp = a_ref.shape[1]
        acc = acc + jnp.dot(a_ref[...], w_ref[k0:k0 + kp, :], preferred_element_type=F32)
        k0 += kp
    o_ref[...] = acc


def matmul_residual(a_pieces, w, res, *, tm):
    m = res.shape[0]
    k, n = w.shape
    return pl.pallas_call(
        _matmul_res_body,
        grid=(m // tm,),
        in_specs=[pl.BlockSpec((tm, a.shape[1]), lambda i: (i, 0)) for a in a_pieces]
        + [pl.BlockSpec((k, n), lambda i: (0, 0)),
           pl.BlockSpec((tm, n), lambda i: (i, 0))],
        out_specs=pl.BlockSpec((tm, n), lambda i: (i, 0)),
        out_shape=jax.ShapeDtypeStruct((m, n), F32),
        compiler_params=_params("parallel"),
        name="matmul_residual",
    )(*a_pieces, w, res)


def _dot_bf16_by_f32(a, b):
    parts = []
    rest = b
    for _ in range(3):
        piece = rest.astype(BF16)
        rest = rest - piece.astype(F32)
        parts.append(jnp.dot(a, piece, preferred_element_type=F32))
    return parts[0] + (parts[1] + parts[2])


def _recurrence_levels(q, k, v, la, st_ref):
    n_ch = REC_T // CHUNK
    n_sub = CHUNK // SUB
    dv = v.shape[1]
    r_i = lax.broadcasted_iota(I32, (CHUNK, CHUNK), 0)
    c_i = lax.broadcasted_iota(I32, (CHUNK, CHUNK), 1)
    tri = jnp.where(c_i <= r_i, 1.0, 0.0).astype(BF16)
    local = [_dot_bf16_by_f32(tri, la[c * CHUNK:(c + 1) * CHUNK]) for c in range(n_ch)]
    cums = [local[0]]
    for c in range(1, n_ch):
        cums.append(local[c] + cums[-1][CHUNK - 1:CHUNK, :])
    cum = jnp.concatenate(cums, axis=0)
    last = cum[REC_T - 1:REC_T, :]
    vb = v.astype(BF16)
    st = st_ref[...]
    yield

    o_inter = lax.dot_general((q * jnp.exp(cum)).astype(BF16), st.astype(BF16), NT_DIMS,
                              preferred_element_type=F32)
    kv = lax.dot_general(vb, (k * jnp.exp(last - cum)).astype(BF16), TN_DIMS, preferred_element_type=F32)
    rows = lax.broadcasted_iota(I32, (SUB, 1), 0)
    ones = jnp.ones((q.shape[1], LANES), BF16)
    s_cross, s_sub, diag = {}, {}, {}
    for c in range(n_ch):
        c0 = c * CHUNK
        if c > 0:
            ref_pt = cum[c0 - 1:c0, :]
            q_t = (q[c0:c0 + CHUNK] * jnp.exp(cum[c0:c0 + CHUNK] - ref_pt)).astype(BF16)
            k_t = (k[:c0] * jnp.exp(ref_pt - cum[:c0])).astype(BF16)
            s_cross[c] = lax.dot_general(q_t, k_t, NT_DIMS, preferred_element_type=F32)
        for b in range(n_sub):
            lo = c0 + b * SUB
            q_b, k_b, c_b = q[lo:lo + SUB], k[lo:lo + SUB], cum[lo:lo + SUB]
            if b > 0:
                ref_pt = cum[lo - 1:lo, :]
                q_t = (q_b * jnp.exp(c_b - ref_pt)).astype(BF16)
                k_t = (k[c0:lo] * jnp.exp(ref_pt - cum[c0:lo])).astype(BF16)
                s_sub[c, b] = lax.dot_general(q_t, k_t, NT_DIMS, preferred_element_type=F32)
            terms = []
            for j in range(SUB):
                e = jnp.exp(jnp.where(rows >= j, c_b - c_b[j:j + 1, :], NEG_INF))
                terms.append(q_b * k_b[j:j + 1, :] * e)
            diag[c, b] = jnp.dot(jnp.concatenate(terms, axis=0).astype(BF16), ones, preferred_element_type=F32)

    yield
    cross = {c: jnp.dot(s_cross[c].astype(BF16), vb[:c * CHUNK], preferred_element_type=F32) for c in s_cross}
    sub = {cb: jnp.dot(s_sub[cb].astype(BF16), v[cb[0] * CHUNK:cb[0] * CHUNK + cb[1] * SUB].astype(BF16),
                       preferred_element_type=F32) for cb in s_sub}
    st_ref[...] = st * jnp.exp(last) + kv
    yield

    outs = []
    for c in range(n_ch):
        for b in range(n_sub):
            lo = c * CHUNK + b * SUB
            acc = o_inter[lo:lo + SUB]
            if c > 0:
                acc = acc + cross[c][b * SUB:(b + 1) * SUB]
            if b > 0:
                acc = acc + sub[c, b]
            v_b = v[lo:lo + SUB]
            for j in range(SUB):
                col = diag[c, b][j * SUB:(j + 1) * SUB]
                if dv > LANES:
                    col = jnp.concatenate([col] * (dv // LANES), axis=1)
                acc = acc + col * v_b[j:j + 1, :]
            outs.append(acc)
    return jnp.concatenate(outs, axis=0)


def _run_heads(steps):
    results = [None] * len(steps)
    live = list(range(len(steps)))
    while live:
        for n in list(live):
            try:
                next(steps[n])
            except StopIteration as done:
                results[n] = done.value
                live.remove(n)
    return results


def _log_sigmoid(z):
    return jnp.minimum(z, 0.0) - jnp.log(1.0 + jnp.exp(-jnp.abs(z)))


def _hgrn2_body(hq_ref, hf_ref, hi_ref, hg_ref, lb_ref, nw_ref, o_ref, st_ref, *, hp):
    @pl.when(pl.program_id(2) == 0)
    def _():
        st_ref[...] = jnp.zeros_like(st_ref)

    steps = []
    for hh in range(hp):
        cols = slice(hh * HG_DK, (hh + 1) * HG_DK)
        lb = lb_ref[:, cols]
        forget = lb + (1.0 - lb) * jax.nn.sigmoid(hf_ref[0, :, cols].astype(F32))
        q = _silu(hq_ref[0, :, cols].astype(F32)) * (HG_DK ** -0.5)
        steps.append(_recurrence_levels(q, 1.0 - forget, hi_ref[0, :, cols].astype(F32), jnp.log(forget),
                                        st_ref.at[hh]))
    for hh, o in enumerate(_run_heads(steps)):
        cols = slice(hh * HG_DV, (hh + 1) * HG_DV)
        o_ref[0, :, cols] = (_rms(o, nw_ref[...]) * _silu(hg_ref[0, :, cols].astype(F32))).astype(o_ref.dtype)


def _gla_body(gq_ref, gk_ref, gv_ref, gg_ref, glow_ref, gk2_ref, gb_ref, nw_ref, o_ref, st_ref, *, hp):
    @pl.when(pl.program_id(2) == 0)
    def _():
        st_ref[...] = jnp.zeros_like(st_ref)

    g_hi = glow_ref[0].astype(BF16)
    g_lo = (glow_ref[0] - g_hi.astype(F32)).astype(BF16)
    w_hi = gk2_ref[...].astype(BF16)
    w_lo = (gk2_ref[...] - w_hi.astype(F32)).astype(BF16)
    z = (jnp.dot(g_hi, w_hi, preferred_element_type=F32)
         + (jnp.dot(g_hi, w_lo, preferred_element_type=F32) + jnp.dot(g_lo, w_hi, preferred_element_type=F32))
         + gb_ref[...])
    la = _log_sigmoid(z) / GLA_GATE_NORM
    steps = []
    for hh in range(hp):
        kc = slice(hh * GLA_DK, (hh + 1) * GLA_DK)
        vc = slice(hh * GLA_DV, (hh + 1) * GLA_DV)
        q = gq_ref[0, :, kc].astype(F32) * (GLA_DK ** -0.5)
        steps.append(_recurrence_levels(q, gk_ref[0, :, kc].astype(F32), gv_ref[0, :, vc].astype(F32), la[:, kc],
                                        st_ref.at[hh]))
    for hh, o in enumerate(_run_heads(steps)):
        vc = slice(hh * GLA_DV, (hh + 1) * GLA_DV)
        o_ref[0, :, vc] = (_rms(o, nw_ref[...]) * _silu(gg_ref[0, :, vc].astype(F32))).astype(o_ref.dtype)


def hgrn2_mix(proj, lb, hg_norm):
    b, s, _ = proj.shape

    hp = HG_HP
    hsteps = HG_HEADS // hp

    def col(base):
        return pl.BlockSpec((1, REC_T, hp * HG_DK), lambda bi, h, t: (bi, t, base + h))

    return pl.pallas_call(
        functools.partial(_hgrn2_body, hp=hp),
        grid=(b, hsteps, s // REC_T),
        in_specs=[col(0), col(hsteps), col(2 * hsteps), col(3 * hsteps),
                  pl.BlockSpec((1, hp * HG_DK), lambda bi, h, t: (0, h)),
                  pl.BlockSpec((1, HG_DV), lambda bi, h, t: (0, 0))],
        out_specs=pl.BlockSpec((1, REC_T, hp * HG_DV), lambda bi, h, t: (bi, t, h)),
        out_shape=jax.ShapeDtypeStruct((b, s, HG_HEADS * HG_DV), BF16),
        scratch_shapes=[pltpu.VMEM((hp, HG_DV, HG_DK), F32)],
        compiler_params=_params("parallel", "parallel", "arbitrary"),
        name="hgrn2_mix",
    )(proj, proj, proj, proj, lb.reshape(1, HG_QK), hg_norm.reshape(1, HG_DV))


def gla_mix(proj, glow, gk2_pad, gk_bias, gla_norm):
    b, s, _ = proj.shape
    hp = GLA_HP
    hsteps = GLA_HEADS // hp
    kw, vw = hp * GLA_DK, hp * GLA_DV
    q0 = 4 * HG_QK // kw
    k0 = q0 + hsteps
    v0 = (4 * HG_QK + 2 * GLA_HEADS * GLA_DK) // vw
    g0 = v0 + hsteps
    return pl.pallas_call(
        functools.partial(_gla_body, hp=hp),
        grid=(b, hsteps, s // REC_T),
        in_specs=[pl.BlockSpec((1, REC_T, kw), lambda bi, h, t: (bi, t, q0 + h)),
                  pl.BlockSpec((1, REC_T, kw), lambda bi, h, t: (bi, t, k0 + h)),
                  pl.BlockSpec((1, REC_T, vw), lambda bi, h, t: (bi, t, v0 + h)),
                  pl.BlockSpec((1, REC_T, vw), lambda bi, h, t: (bi, t, g0 + h)),
                  pl.BlockSpec((1, REC_T, LANES), lambda bi, h, t: (bi, t, 0)),
                  pl.BlockSpec((LANES, kw), lambda bi, h, t: (0, h)),
                  pl.BlockSpec((1, kw), lambda bi, h, t: (0, h)),
                  pl.BlockSpec((1, GLA_DV), lambda bi, h, t: (0, 0))],
        out_specs=pl.BlockSpec((1, REC_T, vw), lambda bi, h, t: (bi, t, h)),
        out_shape=jax.ShapeDtypeStruct((b, s, GLA_HEADS * GLA_DV), BF16),
        scratch_shapes=[pltpu.VMEM((hp, GLA_DV, GLA_DK), F32)],
        compiler_params=_params("parallel", "parallel", "arbitrary"),
        name="gla_mix",
    )(proj, proj, proj, proj, glow, gk2_pad, gk_bias.reshape(1, -1), gla_norm.reshape(1, GLA_DV))


def _dot_f32_by_bf16(a, b):
    parts = []
    rest = a
    for _ in range(3):
        piece = rest.astype(BF16)
        rest = rest - piece.astype(F32)
        parts.append(lax.dot_general(piece, b, NT_DIMS, preferred_element_type=F32))
    return parts[0] + (parts[1] + parts[2])


def _moba_body(q_ref, k_ref, v_ref, o_ref, kmean_ref, vt_ref, bias_ref, sa_ref, sb_ref, *, n_blocks):
    qi = pl.program_id(2)
    blk, dh = MOBA_BLOCK, MOBA_DH
    heads = [slice(hh * dh, (hh + 1) * dh) for hh in range(MOBA_HP)]

    @pl.when(qi == 0)
    def _():
        for n in range(n_blocks):
            rs = slice(n * blk, (n + 1) * blk)
            for hh, cols in enumerate(heads):
                kmean_ref[hh, n:n + 1, :] = jnp.mean(k_ref[0, rs, cols].astype(F32), axis=0, keepdims=True)
                vt_ref[hh, 0:dh, rs] = v_ref[0, rs, cols].astype(F32).T.astype(BF16)
                vt_ref[hh, dh:dh + ONES_ROWS, rs] = jnp.ones((ONES_ROWS, blk), BF16)

    c = (MOBA_DH ** -0.5) * LOG2_E
    own = pl.ds(pl.multiple_of(qi * blk, blk), blk)
    brow = lax.broadcasted_iota(I32, (n_blocks, blk), 0)
    past = brow < qi
    k_i = lax.broadcasted_iota(I32, (blk, blk), 0)
    q_i = lax.broadcasted_iota(I32, (blk, blk), 1)

    qs = [q_ref[0, :, cols] for cols in heads]
    gates = [_dot_f32_by_bf16(kmean_ref[hh], qs[hh]) for hh in range(MOBA_HP)]
    own_s = [lax.dot_general(k_ref[0, own, cols], qs[hh], NT_DIMS, preferred_element_type=F32)
             for hh, cols in enumerate(heads)]
    r01_first = pl.ds(0, 2 * blk)
    for hh, cols in enumerate(heads):
        sa_ref[hh] = lax.dot_general(k_ref[0, r01_first, cols], qs[hh], NT_DIMS, preferred_element_type=F32)
    carry0 = []
    for hh in range(MOBA_HP):
        gate = jnp.where(past, gates[hh], NEG_INF)
        rank = jnp.zeros((n_blocks, blk), I32)
        for m in range(n_blocks):
            gm = gate[m:m + 1, :]
            rank = rank + jnp.where(gm > gate, 1, jnp.where(gm == gate, jnp.where(brow > m, 1, 0), 0))
        keep = jnp.where(past, jnp.where(rank < MOBA_TOPK, 1, 0), 0)
        bias_ref[hh] = jnp.where(keep > 0, 0.0, MASK_BIAS)
    for hh in range(MOBA_HP):
        s = jnp.where(k_i <= q_i, own_s[hh], NEG_INF)
        m0 = jnp.max(s, axis=0, keepdims=True)
        p = jnp.exp2((s - m0) * c)
        acc0 = jnp.dot(vt_ref[hh, :, own], p.astype(BF16), preferred_element_type=F32)
        carry0 += [m0, acc0]

    n_pairs = (qi + 1) // 2

    def score_pair(pair, dst_ref):
        p = jnp.minimum(pair, n_blocks // 2 - 1)
        r01 = pl.ds(pl.multiple_of(p * 2 * blk, 2 * blk), 2 * blk)
        for hh, cols in enumerate(heads):
            dst_ref[hh] = lax.dot_general(k_ref[0, r01, cols], qs[hh], NT_DIMS, preferred_element_type=F32)

    def absorb_pair(pair, src_ref, carry):
        n0 = 2 * pair
        r01 = pl.ds(pl.multiple_of(n0 * blk, 2 * blk), 2 * blk)
        out = []
        for hh, cols in enumerate(heads):
            m_run, acc = carry[2 * hh:2 * hh + 2]
            b0 = bias_ref[hh, pl.ds(n0, 1), :]
            b1 = bias_ref[hh, pl.ds(n0 + 1, 1), :]
            s0 = src_ref[hh, 0:blk, :]
            s1 = src_ref[hh, blk:2 * blk, :]
            m_new = jnp.maximum(m_run, jnp.maximum(jnp.max(s0, axis=0, keepdims=True) + b0,
                                                   jnp.max(s1, axis=0, keepdims=True) + b1))
            alpha = jnp.exp2((m_run - m_new) * c)
            p01 = jnp.concatenate([jnp.exp2((s0 - (m_new - b0)) * c).astype(BF16),
                                   jnp.exp2((s1 - (m_new - b1)) * c).astype(BF16)], axis=0)
            acc = alpha * acc + jnp.dot(vt_ref[hh, :, r01], p01, preferred_element_type=F32)
            out += [m_new, acc]
        return tuple(out)

    def two_pairs(t, carry):
        score_pair(2 * t + 1, sb_ref)
        carry = absorb_pair(2 * t, sa_ref, carry)

        def second(carry):
            score_pair(2 * t + 2, sa_ref)
            return absorb_pair(2 * t + 1, sb_ref, carry)

        return lax.cond(2 * t + 1 < n_pairs, second, lambda carry: carry, carry)

    fin = lax.fori_loop(0, (n_pairs + 1) // 2, two_pairs, tuple(carry0))
    for hh, cols in enumerate(heads):
        acc = fin[2 * hh + 1]
        o_ref[0, :, cols] = (acc[0:dh] / acc[dh:dh + 1]).T.astype(o_ref.dtype)


def moba_attention(qkv):
    b, s, _ = qkv.shape
    n_blocks = s // MOBA_BLOCK
    hw = MOBA_HP * MOBA_DH
    hsteps = MOBA_HEADS // MOBA_HP
    return pl.pallas_call(
        functools.partial(_moba_body, n_blocks=n_blocks),
        grid=(b, hsteps, n_blocks),
        in_specs=[pl.BlockSpec((1, MOBA_BLOCK, hw), lambda bi, h, t: (bi, t, h)),
                  pl.BlockSpec((1, s, hw), lambda bi, h, t: (bi, 0, hsteps + h)),
                  pl.BlockSpec((1, s, hw), lambda bi, h, t: (bi, 0, 2 * hsteps + h))],
        out_specs=pl.BlockSpec((1, MOBA_BLOCK, hw), lambda bi, h, t: (bi, t, h)),
        out_shape=jax.ShapeDtypeStruct((b, s, D_MODEL), BF16),
        scratch_shapes=[pltpu.VMEM((MOBA_HP, n_blocks, MOBA_DH), F32),
                        pltpu.VMEM((MOBA_HP, MOBA_DH + ONES_ROWS, s), BF16),
                        pltpu.VMEM((MOBA_HP, n_blocks, MOBA_BLOCK), F32),
                        pltpu.VMEM((MOBA_HP, 2 * MOBA_BLOCK, MOBA_BLOCK), F32),
                        pltpu.VMEM((MOBA_HP, 2 * MOBA_BLOCK, MOBA_BLOCK), F32)],
        compiler_params=_params("parallel", "parallel", "arbitrary"),
        name="moba_attention",
    )(qkv, qkv, qkv)


def _router_body(h_ref, g_ref, wh_ref, wl_ref, b_ref, idx_ref, wgt_ref, cnt_ref, x3_ref, run_ref):
    step = pl.program_id(0)

    @pl.when(step == 0)
    def _():
        run_ref[...] = jnp.zeros_like(run_ref)

    tm = h_ref.shape[0]
    xn = _rms(h_ref[...], g_ref[...])
    for ch in range(ROW_CHUNKS):
        x3_ref[pl.ds(ch, tm, stride=ROW_CHUNKS), :] = xn[:, ch * LANES:(ch + 1) * LANES]
    x_hi = xn.astype(BF16)
    x_lo = (xn - x_hi.astype(F32)).astype(BF16)
    logits = (jnp.dot(x_hi, wh_ref[...], preferred_element_type=F32)
              + (jnp.dot(x_hi, wl_ref[...], preferred_element_type=F32)
                 + jnp.dot(x_lo, wh_ref[...], preferred_element_type=F32))) + b_ref[...]
    lane = lax.broadcasted_iota(I32, (tm, LANES), 1)

    def first_max(vals):
        top = jnp.max(vals, axis=-1, keepdims=True)
        where = jnp.min(jnp.where(vals == top, lane, LANES), axis=-1, keepdims=True)
        return top, where

    g_logits = jnp.where(lane < N_GROUPS, logits, NEG_INF)
    g_top, grp = first_max(g_logits)
    p_grp = 1.0 / jnp.sum(jnp.exp(g_logits - g_top), axis=-1, keepdims=True)
    lo = EXPERT_LANE0 + grp * EXPERTS_PER_GROUP
    e_logits = jnp.where((lane >= lo) & (lane < lo + EXPERTS_PER_GROUP), logits, NEG_INF)
    v0, j0 = first_max(e_logits)
    v1, j1 = first_max(jnp.where(lane == j0, NEG_INF, e_logits))
    t = jnp.exp(v1 - v0)
    w0 = p_grp / (1.0 + t)
    w1 = p_grp * t / (1.0 + t)

    hit0 = lane == j0
    hit1 = lane == j1
    member = jnp.where(hit0 | hit1, 1.0, 0.0)
    r_i = lax.broadcasted_iota(I32, (tm, tm), 0)
    c_i = lax.broadcasted_iota(I32, (tm, tm), 1)
    before = jnp.where(c_i < r_i, 1.0, 0.0).astype(BF16)
    prior = jnp.dot(before, member.astype(BF16), preferred_element_type=F32) + run_ref[...]
    rank0 = jnp.sum(jnp.where(hit0, prior, 0.0), axis=-1, keepdims=True).astype(I32)
    rank1 = jnp.sum(jnp.where(hit1, prior, 0.0), axis=-1, keepdims=True).astype(I32)
    run_ref[...] = run_ref[...] + jnp.sum(member, axis=0, keepdims=True)
    cnt_ref[...] = run_ref[...]

    idx_ref[...] = jnp.where(lane == 0, j0 - EXPERT_LANE0,
                             jnp.where(lane == 1, j1 - EXPERT_LANE0,
                                       jnp.where(lane == 2, rank0, jnp.where(lane == 3, rank1, 0))))
    wgt_ref[...] = jnp.where(lane == 0, w0, jnp.where(lane == 1, w1, 0.0))


def moe_router(h2d, g, w_cat, b_cat):
    n_tok, d = h2d.shape
    w_hi = w_cat.astype(BF16)
    return pl.pallas_call(
        _router_body,
        grid=(n_tok // ROUTE_TM,),
        in_specs=[pl.BlockSpec((ROUTE_TM, d), lambda i: (i, 0)),
                  pl.BlockSpec((1, d), lambda i: (0, 0)),
                  pl.BlockSpec((d, LANES), lambda i: (0, 0)),
                  pl.BlockSpec((d, LANES), lambda i: (0, 0)),
                  pl.BlockSpec((1, LANES), lambda i: (0, 0))],
        out_specs=[pl.BlockSpec((ROUTE_TM, LANES), lambda i: (i, 0)),
                   pl.BlockSpec((ROUTE_TM, LANES), lambda i: (i, 0)),
                   pl.BlockSpec((1, LANES), lambda i: (0, 0)),
                   pl.BlockSpec((ROUTE_TM * ROW_CHUNKS, LANES), lambda i: (i, 0))],
        out_shape=[jax.ShapeDtypeStruct((n_tok, LANES), I32),
                   jax.ShapeDtypeStruct((n_tok, LANES), F32),
                   jax.ShapeDtypeStruct((1, LANES), F32),
                   jax.ShapeDtypeStruct((n_tok * ROW_CHUNKS, LANES), F32)],
        scratch_shapes=[pltpu.VMEM((1, LANES), F32)],
        compiler_params=_params("arbitrary"),
        name="moe_router",
    )(h2d, g.reshape(1, d), w_hi, (w_cat - w_hi.astype(F32)).astype(BF16), b_cat)


INV_UNROLL = 16
DUMP_ROWS = MOE_BM


def _expert_body(be_ref, nu_ref, dest_ref, gfill_hbm, sfill_hbm, x3_hbm, w1_ref, w3_ref, w2_ref, o2_hbm,
                 gsrc_ref, sdst_ref, xbuf, ybuf, gsem, ssem, isem, w1_bf, w3_bf, w2_bf, *, n_tok, n_blocks):
    i = pl.program_id(0)
    n_used = nu_ref[0]
    slot = i % 2
    plane_rows = (n_tok + DUMP_ROWS) * ROW_CHUNKS

    def gather_row(block, r, to_slot):
        src = pl.multiple_of(gsrc_ref[block * MOE_BM + r], ROW_CHUNKS)
        return pltpu.make_async_copy(x3_hbm.at[pl.ds(src, ROW_CHUNKS)],
                                     xbuf.at[to_slot, pl.ds(r * BUF_PITCH, ROW_CHUNKS)], gsem.at[to_slot])

    def scatter_row(block, r, from_slot):
        dst = pl.multiple_of(sdst_ref[block * MOE_BM + r], ROW_CHUNKS)
        return pltpu.make_async_copy(ybuf.at[from_slot, pl.ds(r * BUF_PITCH, ROW_CHUNKS)],
                                     o2_hbm.at[pl.ds(dst, ROW_CHUNKS)], ssem.at[from_slot])

    block_rows = MOE_BM * ROW_CHUNKS

    def gather_wait(of_slot):
        pltpu.make_async_copy(x3_hbm.at[pl.ds(0, block_rows)], xbuf.at[of_slot, pl.ds(0, block_rows)],
                              gsem.at[of_slot]).wait()

    def scatter_wait(of_slot):
        pltpu.make_async_copy(ybuf.at[of_slot, pl.ds(0, block_rows)], o2_hbm.at[pl.ds(0, block_rows)],
                              ssem.at[of_slot]).wait()

    @pl.when(i == 0)
    def _():
        for fill_hbm, table in ((gfill_hbm, gsrc_ref), (sfill_hbm, sdst_ref)):
            fill = pltpu.make_async_copy(fill_hbm, table, isem)
            fill.start()
            fill.wait()

        def body(c, carry):
            base = c * INV_UNROLL
            rows = [dest_ref[base + u] for u in range(INV_UNROLL)]
            tok_row0 = c * (INV_UNROLL // 2 * ROW_CHUNKS)
            for u in range(INV_UNROLL):
                gsrc_ref[rows[u]] = tok_row0 + (u // 2) * ROW_CHUNKS
                sdst_ref[rows[u]] = tok_row0 + ((u % 2) * plane_rows + (u // 2) * ROW_CHUNKS)
            return carry
        lax.fori_loop(0, dest_ref.shape[0] // INV_UNROLL, body, 0)
        ybuf[...] = jnp.zeros_like(ybuf)
        for plane in range(2):
            init = pltpu.make_async_copy(ybuf.at[plane, pl.ds(0, block_rows)],
                                         o2_hbm.at[pl.ds(plane * plane_rows + n_tok * ROW_CHUNKS, block_rows)], isem)
            init.start()
            init.wait()
        for r in range(MOE_BM):
            gather_row(0, r, 0).start()

    @pl.when(i < n_used)
    def _():
        gather_wait(slot)

        @pl.when((i == 0) | (be_ref[i] != be_ref[jnp.maximum(i - 1, 0)]))
        def _():
            w1_bf[...] = w1_ref[...].astype(BF16)
            w3_bf[...] = w3_ref[...].astype(BF16)
            w2_bf[...] = w2_ref[...].astype(BF16)

        @pl.when(i >= 1)
        def _():
            scatter_wait(slot)

        prev = jnp.where(i == 0, n_blocks - 1, i - 1)
        for r in range(MOE_BM):
            scatter_row(prev, r, 1 - slot).start()
        xn = jnp.concatenate([xbuf[slot, pl.ds(ch, MOE_BM, stride=BUF_PITCH), :] for ch in range(ROW_CHUNKS)],
                             axis=1).astype(BF16)
        for r in range(MOE_BM):
            gather_row(i + 1, r, 1 - slot).start()
        h1 = jnp.dot(xn, w1_bf[...], preferred_element_type=F32)
        h3 = jnp.dot(xn, w3_bf[...], preferred_element_type=F32)
        act = (_silu(h1) * h3).astype(BF16)
        y = jnp.dot(act, w2_bf[...], preferred_element_type=F32)
        for ch in range(ROW_CHUNKS):
            ybuf[slot, pl.ds(ch, MOE_BM, stride=BUF_PITCH), :] = y[:, ch * LANES:(ch + 1) * LANES]

    @pl.when(i == n_used)
    def _():
        for r in range(MOE_BM):
            scatter_row(i - 1, r, 1 - slot).start()
        gather_wait(slot)
        scatter_wait(slot)
        scatter_wait(1 - slot)


def moe_experts(x3, dest_flat, w1, w3, w2, layer, block_e, n_used):
    plane_rows = (x3.shape[0] // ROW_CHUNKS + DUMP_ROWS) * ROW_CHUNKS
    n_tok = x3.shape[0] // ROW_CHUNKS
    d = D_MODEL
    n_blocks = block_e.shape[0]
    n_rows = n_blocks * MOE_BM
    f = w1.shape[-1]

    def w_map(i, be, nu, dest):
        return (layer, be[i], 0, 0)

    grid_spec = pltpu.PrefetchScalarGridSpec(
        num_scalar_prefetch=3,
        grid=(n_blocks,),
        in_specs=[pl.BlockSpec(memory_space=pl.ANY),
                  pl.BlockSpec(memory_space=pl.ANY),
                  pl.BlockSpec(memory_space=pl.ANY),
                  pl.BlockSpec((None, None, d, f), w_map),
                  pl.BlockSpec((None, None, d, f), w_map),
                  pl.BlockSpec((None, None, f, d), w_map)],
        out_specs=pl.BlockSpec(memory_space=pl.ANY),
        scratch_shapes=[pltpu.SMEM((n_rows,), I32),
                        pltpu.SMEM((n_rows,), I32),
                        pltpu.VMEM((2, MOE_BM * BUF_PITCH, LANES), F32),
                        pltpu.VMEM((2, MOE_BM * BUF_PITCH, LANES), F32),
                        pltpu.SemaphoreType.DMA((2,)),
                        pltpu.SemaphoreType.DMA((2,)),
                        pltpu.SemaphoreType.DMA(()),
                        pltpu.VMEM((d, f), BF16), pltpu.VMEM((d, f), BF16), pltpu.VMEM((f, d), BF16)],
    )
    pad_row = jnp.arange(n_rows, dtype=I32) % MOE_BM
    o2 = pl.pallas_call(
        functools.partial(_expert_body, n_tok=n_tok, n_blocks=n_blocks),
        grid_spec=grid_spec,
        out_shape=jax.ShapeDtypeStruct((2 * plane_rows, LANES), F32),
        compiler_params=_params("arbitrary"),
        name="moe_experts",
    )(block_e, n_used, dest_flat, jnp.zeros((n_rows,), I32), (n_tok + pad_row) * ROW_CHUNKS, x3, w1, w3, w2)
    return o2.reshape(2, plane_rows, LANES)


def _combine_body(*refs, final_norm):
    if final_norm:
        h_ref, wgt_ref, o2_ref, gf_ref, o_ref = refs
    else:
        h_ref, wgt_ref, o2_ref, o_ref = refs
    wgt = wgt_ref[...]
    y0, y1 = (jnp.concatenate([o2_ref[s, pl.ds(ch, COMBINE_T, stride=ROW_CHUNKS), :] for ch in range(ROW_CHUNKS)], axis=1)
              for s in range(2))
    out = h_ref[...] + wgt[:, 0:1] * y0 + wgt[:, 1:2] * y1
    if final_norm:
        out = _rms(out, gf_ref[...])
    o_ref[...] = out


def moe_combine(h2d, wgt, o2, g_final=None):
    n_tok, d = h2d.shape
    final_norm = g_final is not None
    in_specs = [pl.BlockSpec((COMBINE_T, d), lambda i: (i, 0)),
                pl.BlockSpec((COMBINE_T, LANES), lambda i: (i, 0)),
                pl.BlockSpec((2, COMBINE_T * ROW_CHUNKS, LANES), lambda i: (0, i, 0))]
    args = [h2d, wgt, o2]
    if final_norm:
        in_specs.append(pl.BlockSpec((1, d), lambda i: (0, 0)))
        args.append(g_final.reshape(1, d))
    return pl.pallas_call(
        functools.partial(_combine_body, final_norm=final_norm),
        grid=(n_tok // COMBINE_T,),
        in_specs=in_specs,
        out_specs=pl.BlockSpec((COMBINE_T, d), lambda i: (i, 0)),
        out_shape=jax.ShapeDtypeStruct((n_tok, d), F32),
        compiler_params=_params("parallel"),
        name="moe_combine_final" if final_norm else "moe_combine",
    )(*args)


def hierarchical_moe(h2d, layer, norm_g, w_group, b_group, w_router, b_router, w1, w3, w2, g_final=None):
    n_tok, d = h2d.shape
    pad_l = LANES - N_GROUPS - N_EXPERTS
    w_cat = jnp.concatenate([w_group, w_router, jnp.zeros((d, pad_l), F32)], axis=1)
    b_cat = jnp.concatenate([b_group, b_router, jnp.zeros((pad_l,), F32)]).reshape(1, LANES)
    idx, wgt, cnt, x3 = moe_router(h2d, norm_g, w_cat, b_cat)

    counts = cnt[0, EXPERT_LANE0:EXPERT_LANE0 + N_EXPERTS].astype(I32)
    padded = (counts + MOE_BM - 1) // MOE_BM * MOE_BM
    pends = jnp.cumsum(padded)
    pstarts = pends - padded
    n_blocks = (n_tok * 2) // MOE_BM + N_EXPERTS
    blk_row0 = jnp.arange(n_blocks, dtype=I32) * MOE_BM
    block_e = jnp.minimum(jnp.sum(pends[None, :] <= blk_row0[:, None], axis=1), N_EXPERTS - 1).astype(I32)
    n_used = (pends[-1:] // MOE_BM).astype(I32)
    e_iota = jnp.arange(N_EXPERTS, dtype=I32)
    row0 = jnp.sum(jnp.where(idx[:, 0:2, None] == e_iota, pstarts, 0), axis=-1)
    dest = (row0 + idx[:, 2:4]).astype(I32).reshape(-1)

    o2 = moe_experts(x3, dest, w1, w3, w2, layer, block_e, n_used)
    return moe_combine(h2d, wgt, o2, g_final)


def kernel(x, norm_mix, norm_ffn, norm_final, w_in_even, hg_lb_logits, hg_norm, gla_gk2, gla_gk_bias,
           gla_norm, w_out_even, w_qkv_odd, w_o_odd, router_group_w, router_group_b, router_expert_w,
           router_expert_b, expert_w1, expert_w3, expert_w2):
    b, s, d = x.shape
    n_tok = b * s
    depth = norm_mix.shape[0]
    lb_table = jnp.cumsum(jax.nn.softmax(hg_lb_logits.astype(F32), axis=0), axis=0)
    h = x.reshape(n_tok, d)
    for l in range(depth):
        if l % 2 == 0:
            e = l // 2
            w_in = w_in_even[e]
            w_low = jnp.pad(w_in[:, MAIN_IN:], ((0, 0), (0, LANES - GLA_RANK))).astype(BF16)
            proj, glow = norm_matmul(h, norm_mix[l], w_in.astype(BF16), w_low, tm=PROJ_TM, tn=IN_TN, n=MAIN_IN)
            proj = proj.reshape(b, s, MAIN_IN)
            gk2_pad = jnp.pad(gla_gk2[e], ((0, LANES - GLA_RANK), (0, 0)))
            o_hg = hgrn2_mix(proj, lb_table[l], hg_norm[e])
            o_gla = gla_mix(proj, glow.reshape(b, s, LANES), gk2_pad, gla_gk_bias[e], gla_norm[e])
            h = matmul_residual([o_hg.reshape(n_tok, -1), o_gla.reshape(n_tok, -1)], w_out_even[e].astype(BF16), h,
                                tm=OUT_TM)
        else:
            o = l // 2
            qkv = norm_matmul(h, norm_mix[l], w_qkv_odd[o].astype(BF16), tm=PROJ_TM, tn=QKV_TN)
            attn = moba_attention(qkv.reshape(b, s, 3 * d))
            h = matmul_residual([attn.reshape(n_tok, d)], w_o_odd[o].astype(BF16), h, tm=OUT_TM)
        h = hierarchical_moe(h, l, norm_ffn[l], router_group_w[l], router_group_b[l], router_expert_w[l],
                             router_expert_b[l], expert_w1, expert_w3, expert_w2,
                             g_final=norm_final if l == depth - 1 else None)
    return h.reshape(b, s, d)
```
